```python
import jax, jax.numpy as jnp
from jax import lax
import numpy as np

D_MODEL = 1024
BATCH = 8
SEQ = 2048
DEPTH = 1

CHUNK = 64
Q_BLOCK = 128
SB_HEAD_DIM = 64
SB_WIDTH = D_MODEL // 2
SB_HEADS = SB_WIDTH // SB_HEAD_DIM
GLA_HEADS = 4
GLA_K_WIDTH = D_MODEL // 4
GLA_V_WIDTH = D_MODEL // 2
GLA_KEY_DIM = GLA_K_WIDTH // GLA_HEADS
GLA_VAL_DIM = GLA_V_WIDTH // GLA_HEADS
GLA_RANK = 16
GLA_GATE_TEMP = 16.0
N_BRANCH = 2
EPS = 1e-6

IN_SPLITS = (SB_WIDTH, SB_WIDTH, SB_WIDTH, SB_WIDTH,
             GLA_K_WIDTH, GLA_K_WIDTH, GLA_V_WIDTH, GLA_V_WIDTH,
             GLA_RANK,
             D_MODEL, D_MODEL)
N_IN = 4 * SB_WIDTH + 2 * GLA_K_WIDTH + 2 * GLA_V_WIDTH + GLA_RANK + N_BRANCH * D_MODEL

kernel_name = "stickbreak_gla_gated_hybrid"


def _rmsnorm(x, g):
    xf = x.astype(jnp.float32)
    y = xf * lax.rsqrt(jnp.mean(xf * xf, axis=-1, keepdims=True) + EPS)
    return y.astype(x.dtype) * g


def _heads(t, n_heads):
    b, s, w = t.shape
    return t.reshape(b, s, n_heads, w // n_heads).transpose(0, 2, 1, 3)


def _merge_heads(t):
    b, h, s, d = t.shape
    return t.transpose(0, 2, 1, 3).reshape(b, s, h * d)


def _stick_breaking(q, k, v):
    seq = q.shape[2]
    scale = q.shape[-1] ** -0.5
    outs = []
    for blk in range(seq // Q_BLOCK):
        start = blk * Q_BLOCK
        end = start + Q_BLOCK
        qb = q[:, :, start:end]
        kb = k[:, :, :end]
        vb = v[:, :, :end]
        z = jnp.einsum('bhqd,bhkd->bhqk', qb, kb).astype(jnp.float32) * scale
        t_idx = start + jnp.arange(Q_BLOCK)[:, None]
        s_idx = jnp.arange(end)[None, :]
        mask = s_idx < t_idx
        log_one_minus = jnp.where(mask, jax.nn.log_sigmoid(-z), 0.0)
        suffix = lax.cumsum(log_one_minus, axis=3, reverse=True) - log_one_minus
        weights = jnp.where(mask, jnp.exp(jax.nn.log_sigmoid(z) + suffix), 0.0)
        outs.append(jnp.einsum('bhqk,bhkd->bhqd', weights.astype(vb.dtype), vb))
    return jnp.concatenate(outs, axis=2)


def _gla_chunked(q, k, v, log_alpha):
    b_, h_, s_, dk = q.shape
    dv = v.shape[-1]
    n = s_ // CHUNK
    scale = dk ** -0.5
    q = q.reshape(b_, h_, n, CHUNK, dk)
    k = k.reshape(b_, h_, n, CHUNK, dk)
    v = v.reshape(b_, h_, n, CHUNK, dv)
    cum = lax.cumsum(log_alpha.reshape(b_, h_, n, CHUNK, dk), axis=3)
    cum_last = cum[:, :, :, -1:, :]
    q_dec = q * jnp.exp(cum) * scale
    k_inv = k * jnp.exp(-cum)
    k_to_end = k * jnp.exp(cum_last - cum)
    causal = jnp.tril(jnp.ones((CHUNK, CHUNK), dtype=bool))
    scores = jnp.where(causal, jnp.einsum('bhnqd,bhnsd->bhnqs', q_dec, k_inv), 0.0)
    o_intra = jnp.einsum('bhnqs,bhnsv->bhnqv', scores, v)
    kv_chunk = jnp.einsum('bhnsd,bhnsv->bhndv', k_to_end, v)
    chunk_decay = jnp.exp(cum_last[:, :, :, 0, :])

    def step(state, inp):
        dec, kv = inp
        return dec[..., None] * state + kv, state

    init = jnp.zeros((b_, h_, dk, dv), kv_chunk.dtype)
    _, s_prev = lax.scan(step, init, (jnp.moveaxis(chunk_decay, 2, 0), jnp.moveaxis(kv_chunk, 2, 0)))
    s_prev = jnp.moveaxis(s_prev, 0, 2)
    o_inter = jnp.einsum('bhnqd,bhndv->bhnqv', q_dec, s_prev)
    return (o_intra + o_inter).reshape(b_, h_, s_, dv)


def setup_inputs(seed: int = 0) -> dict:
    key = jax.random.key(seed)
    ks = jax.random.split(key, 12)
    f32 = jnp.float32
    x = jax.random.normal(ks[0], (BATCH, SEQ, D_MODEL), f32)
    g_pre = 1.0 + 0.02 * jax.random.normal(ks[1], (DEPTH, D_MODEL), f32)
    w_in = jax.random.normal(ks[2], (DEPTH, D_MODEL, N_IN), f32) * D_MODEL ** -0.5
    b_gate = 0.01 * jax.random.normal(ks[3], (DEPTH, N_BRANCH * D_MODEL), f32)
    w_alpha_up = jax.random.normal(ks[4], (DEPTH, GLA_RANK, GLA_K_WIDTH), f32) * GLA_RANK ** -0.5
    b_alpha_up = 0.01 * jax.random.normal(ks[5], (DEPTH, GLA_K_WIDTH), f32)
    g_gla_norm = 1.0 + 0.02 * jax.random.normal(ks[6], (DEPTH, GLA_VAL_DIM), f32)
    w_o_sb = jax.random.normal(ks[7], (DEPTH, SB_WIDTH, D_MODEL), f32) * SB_WIDTH ** -0.5
    w_o_gla = jax.random.normal(ks[8], (DEPTH, GLA_V_WIDTH, D_MODEL), f32) * GLA_V_WIDTH ** -0.5
    w_out = jax.random.normal(ks[9], (DEPTH, D_MODEL, D_MODEL), f32) * D_MODEL ** -0.5
    g_final = 1.0 + 0.02 * jax.random.normal(ks[10], (D_MODEL,), f32)
    return {"x": x, "g_pre": g_pre, "w_in": w_in, "b_gate": b_gate,
            "w_alpha_up": w_alpha_up, "b_alpha_up": b_alpha_up, "g_gla_norm": g_gla_norm,
            "w_o_sb": w_o_sb, "w_o_gla": w_o_gla, "w_out": w_out, "g_final": g_final}


def reference(x, g_pre, w_in, b_gate, w_alpha_up, b_alpha_up, g_gla_norm,
              w_o_sb, w_o_gla, w_out, g_final):
    split_at = [int(c) for c in np.cumsum(IN_SPLITS)[:-1]]
    for layer in range(DEPTH):
        h = _rmsnorm(x, g_pre[layer])
        proj = jnp.einsum('bsd,dn->bsn', h, w_in[layer])
        (sb_q, sb_k, sb_v, sb_z, gla_q, gla_k, gla_v, gla_z,
         gla_lr, gate_logits) = jnp.split(proj, split_at, axis=-1)[:10] + [None] if False else \
            tuple(jnp.split(proj[..., :split_at[-2]], split_at[:-2], axis=-1)) + (proj[..., split_at[-2]:],)

        o_sb = _merge_heads(_stick_breaking(_heads(sb_q, SB_HEADS), _heads(sb_k, SB_HEADS),
                                            _heads(sb_v, SB_HEADS)))
        y_sb = jnp.einsum('bsw,wd->bsd', (o_sb * jax.nn.silu(sb_z)).astype(x.dtype), w_o_sb[layer])

        alpha_logit = jnp.einsum('bsr,rk->bsk', gla_lr, w_alpha_up[layer]) + b_alpha_up[layer]
        log_alpha = jax.nn.log_sigmoid(alpha_logit.astype(jnp.float32)) / GLA_GATE_TEMP
        o_gla = _gla_chunked(_heads(gla_q, GLA_HEADS), _heads(gla_k, GLA_HEADS),
                             _heads(gla_v, GLA_HEADS), _heads(log_alpha, GLA_HEADS))
        o_gla = _merge_heads(_rmsnorm(o_gla, g_gla_norm[layer]))
        y_gla = jnp.einsum('bsw,wd->bsd', (o_gla * jax.nn.silu(gla_z)).astype(x.dtype), w_o_gla[layer])

        gates = jax.nn.sigmoid(gate_logits + b_gate[layer])
        g_sb, g_gla = gates[..., :D_MODEL], gates[..., D_MODEL:]
        merged = g_sb * y_sb + g_gla * y_gla
        x = x + jnp.einsum('bsd,de->bse', merged, w_out[layer])
    return _rmsnorm(x, g_final)
```

```python
import functools

import jax
import jax.numpy as jnp
from jax import lax
from jax.experimental import pallas as pl
from jax.experimental.pallas import tpu as pltpu

D_MODEL = 1024
CHUNK = 64
SB_HEAD_DIM = 64
SB_WIDTH = D_MODEL // 2
GLA_HEADS = 4
GLA_K_WIDTH = D_MODEL // 4
GLA_V_WIDTH = D_MODEL // 2
GLA_KEY_DIM = GLA_K_WIDTH // GLA_HEADS
GLA_VAL_DIM = GLA_V_WIDTH // GLA_HEADS
GLA_RANK = 16
GLA_GATE_TEMP = 16.0
EPS = 1e-6

LANES = 128
TQ = 256
LR_PAD = LANES

OFF_SB_Q = 0
OFF_SB_K = OFF_SB_Q + SB_WIDTH
OFF_SB_V = OFF_SB_K + SB_WIDTH
OFF_SB_Z = OFF_SB_V + SB_WIDTH
OFF_GLA_Q = OFF_SB_Z + SB_WIDTH
OFF_GLA_K = OFF_GLA_Q + GLA_K_WIDTH
OFF_GLA_V = OFF_GLA_K + GLA_K_WIDTH
OFF_GLA_Z = OFF_GLA_V + GLA_V_WIDTH
OFF_LR = OFF_GLA_Z + GLA_V_WIDTH
OFF_GATE = OFF_LR + LR_PAD
N_PACKED = OFF_GATE + 2 * D_MODEL

VMEM_LIMIT_BYTES = 56 * 1024 * 1024

_F32 = jnp.float32
_BF16 = jnp.bfloat16
_NT = (((1,), (1,)), ((), ()))
_TN = (((0,), (0,)), ((), ()))


def _dot(a, b):
    return jnp.dot(a, b, preferred_element_type=_F32)


def _log_sigmoid(x):
    return jnp.minimum(x, 0.0) - jnp.log(1.0 + jnp.exp(-jnp.abs(x)))


def _silu(x):
    return x * jax.nn.sigmoid(x)


def _rms(x, gain):
    return x * lax.rsqrt(jnp.mean(x * x, axis=-1, keepdims=True) + EPS) * gain


def _sb_block(q_half, k2, v2, later, carry_ref, acc_ref, slot, mask):
    z = lax.dot_general(q_half, k2, _NT, preferred_element_type=_F32)
    lom = _log_sigmoid(-z)
    if mask is not None:
        lom = jnp.where(mask, lom, 0.0)
    suffix = _dot(lom.astype(_BF16), later)
    w = jnp.exp(z + lom + suffix + carry_ref[slot])
    if mask is not None:
        w = jnp.where(mask, w, 0.0)
    acc_ref[slot] += _dot(w.astype(_BF16), v2)
    carry_ref[slot] += jnp.sum(lom, axis=1, keepdims=True)


def _stick_breaking(q, k_all, v_all, step, carry_ref, acc_ref):
    row = lax.broadcasted_iota(jnp.int32, (TQ, TQ), 0)
    col = lax.broadcasted_iota(jnp.int32, (TQ, TQ), 1)
    later = (row > col).astype(_BF16)
    causal = col < row
    lane = lax.broadcasted_iota(jnp.int32, (1, LANES), 1)
    first_head = lane < SB_HEAD_DIM
    outs = []
    for pair in range(SB_WIDTH // LANES):
        cols = slice(pair * LANES, (pair + 1) * LANES)
        q2 = q[:, cols]
        halves = (jnp.where(first_head, q2, 0), jnp.where(first_head, 0, q2))
        carry_ref[...] = jnp.zeros_like(carry_ref)
        acc_ref[...] = jnp.zeros_like(acc_ref)
        diag = pl.multiple_of(step * TQ, TQ)
        k2 = k_all[pl.ds(diag, TQ), cols]
        v2 = v_all[pl.ds(diag, TQ), cols]
        for slot in range(2):
            _sb_block(halves[slot], k2, v2, later, carry_ref, acc_ref, slot, causal)

        def earlier_block(t, _):
            start = pl.multiple_of((step - 1 - t) * TQ, TQ)
            kb = k_all[pl.ds(start, TQ), cols]
            vb = v_all[pl.ds(start, TQ), cols]
            for slot in range(2):
                _sb_block(halves[slot], kb, vb, later, carry_ref, acc_ref, slot, None)
            return 0

        lax.fori_loop(0, step, earlier_block, 0)
        outs.append(jnp.where(first_head, acc_ref[0], acc_ref[1]))
    return jnp.concatenate(outs, axis=1)


def _gla(q, k, v, log_alpha, state_ref, gain):
    row = lax.broadcasted_iota(jnp.int32, (TQ, TQ), 0)
    col = lax.broadcasted_iota(jnp.int32, (TQ, TQ), 1)
    same_chunk = (row // CHUNK) == (col // CHUNK)
    causal = jnp.logical_and(same_chunk, col <= row)
    la_hi = log_alpha.astype(_BF16)
    la_lo = (log_alpha - la_hi.astype(_F32)).astype(_BF16)
    prefix = causal.astype(_BF16)
    total = same_chunk.astype(_BF16)
    cum = _dot(prefix, la_hi) + _dot(prefix, la_lo)
    cum_last = _dot(total, la_hi) + _dot(total, la_lo)
    q_dec = (q * jnp.exp(cum) * (GLA_KEY_DIM ** -0.5)).astype(_BF16)
    k_inv = (k * jnp.exp(-cum)).astype(_BF16)
    k_end = (k * jnp.exp(cum_last - cum)).astype(_BF16)
    decay = jnp.exp(cum_last)
    v_bf = v.astype(_BF16)

    lane = lax.broadcasted_iota(jnp.int32, (1, LANES), 1)
    first_head = lane < GLA_KEY_DIM
    outs = [None] * GLA_HEADS
    for pair in range(GLA_HEADS // 2):
        cols = slice(pair * LANES, (pair + 1) * LANES)
        qd2 = q_dec[:, cols]
        halves = (jnp.where(first_head, qd2, 0), jnp.where(first_head, 0, qd2))
        ki2 = k_inv[:, cols]
        heads = (2 * pair, 2 * pair + 1)
        vals = [v_bf[:, h * GLA_VAL_DIM:(h + 1) * GLA_VAL_DIM] for h in heads]
        intra = []
        for slot in range(2):
            scores = lax.dot_general(halves[slot], ki2, _NT, preferred_element_type=_F32)
            scores = jnp.where(causal, scores, 0.0).astype(_BF16)
            intra.append(_dot(scores, vals[slot]))
        inter = ([], [])
        state = state_ref[pair]
        for c in range(TQ // CHUNK):
            rows = slice(c * CHUNK, (c + 1) * CHUNK)
            state_bf = state.astype(_BF16)
            for slot in range(2):
                inter[slot].append(
                    lax.dot_general(halves[slot][rows], state_bf, _NT, preferred_element_type=_F32))
            ke = k_end[rows, cols]
            kv = [lax.dot_general(vals[slot][rows], ke, _TN, preferred_element_type=_F32) for slot in range(2)]
            state = state * decay[c * CHUNK:c * CHUNK + 1, cols] + jnp.where(first_head, kv[0], kv[1])
        state_ref[pair] = state
        for slot in range(2):
            o = intra[slot] + jnp.concatenate(inter[slot], axis=0)
            outs[heads[slot]] = _rms(o, gain)
    return jnp.concatenate(outs, axis=1)


def _block_kernel(x_ref, g_pre_ref, w_in_ref, b_gate_ref, w_up_ref, b_up_ref, g_gla_ref,
                  w_o_sb_ref, w_o_gla_ref, w_out_ref, g_final_ref, out_ref,
                  k_all, v_all, state_ref, carry_ref, acc_ref):
    step = pl.program_id(1)

    @pl.when(step == 0)
    def _():
        state_ref[...] = jnp.zeros_like(state_ref)

    x = x_ref[0]
    h = _rms(x, g_pre_ref[...]).astype(_BF16)

    def proj(offset, width):
        return _dot(h, w_in_ref[:, offset:offset + width])

    rows = pl.ds(pl.multiple_of(step * TQ, TQ), TQ)
    k_all[rows, :] = proj(OFF_SB_K, SB_WIDTH).astype(_BF16)
    v_all[rows, :] = proj(OFF_SB_V, SB_WIDTH).astype(_BF16)
    sb_q = (proj(OFF_SB_Q, SB_WIDTH) * (SB_HEAD_DIM ** -0.5)).astype(_BF16)
    o_sb = _stick_breaking(sb_q, k_all, v_all, step, carry_ref, acc_ref)
    y_sb = _dot((o_sb * _silu(proj(OFF_SB_Z, SB_WIDTH))).astype(_BF16), w_o_sb_ref[...])

    lr = proj(OFF_LR, LR_PAD).astype(_BF16)
    alpha_logit = _dot(lr, w_up_ref[...]) + b_up_ref[...]
    log_alpha = _log_sigmoid(alpha_logit) / GLA_GATE_TEMP
    o_gla = _gla(proj(OFF_GLA_Q, GLA_K_WIDTH), proj(OFF_GLA_K, GLA_K_WIDTH), proj(OFF_GLA_V, GLA_V_WIDTH),
                 log_alpha, state_ref, g_gla_ref[...])
    y_gla = _dot((o_gla * _silu(proj(OFF_GLA_Z, GLA_V_WIDTH))).astype(_BF16), w_o_gla_ref[...])

    gates = jax.nn.sigmoid(proj(OFF_GATE, 2 * D_MODEL) + b_gate_ref[...])
    merged = gates[:, :D_MODEL] * y_sb + gates[:, D_MODEL:] * y_gla
    res = x + _dot(merged.astype(_BF16), w_out_ref[...])
    out_ref[0] = _rms(res, g_final_ref[...])


def _resident(shape):
    return pl.BlockSpec(shape, lambda b, i: (0,) * len(shape), pipeline_mode=pl.Buffered(1))


@jax.jit
def kernel(x, g_pre, w_in, b_gate, w_alpha_up, b_alpha_up, g_gla_norm, w_o_sb, w_o_gla, w_out, g_final):
    batch, seq, d_model = x.shape
    depth = w_in.shape[0]
    assert d_model == D_MODEL and seq % TQ == 0 and depth == 1
    n_lr = OFF_LR + GLA_RANK
    w = w_in[0]
    w_packed = jnp.concatenate(
        [w[:, :n_lr], jnp.zeros((D_MODEL, LR_PAD - GLA_RANK), w.dtype), w[:, n_lr:]], axis=1).astype(_BF16)
    w_up = jnp.concatenate(
        [w_alpha_up[0], jnp.zeros((LR_PAD - GLA_RANK, GLA_K_WIDTH), w_alpha_up.dtype)], axis=0).astype(_BF16)

    row_block = pl.BlockSpec((1, TQ, D_MODEL), lambda b, i: (b, i, 0))
    return pl.pallas_call(
        _block_kernel,
        grid=(batch, seq // TQ),
        in_specs=[
            row_block,
            _resident((1, D_MODEL)),
            _resident((D_MODEL, N_PACKED)),
            _resident((1, 2 * D_MODEL)),
            _resident((LR_PAD, GLA_K_WIDTH)),
            _resident((1, GLA_K_WIDTH)),
            _resident((1, GLA_VAL_DIM)),
            _resident((SB_WIDTH, D_MODEL)),
            _resident((GLA_V_WIDTH, D_MODEL)),
            _resident((D_MODEL, D_MODEL)),
            _resident((1, D_MODEL)),
        ],
        out_specs=row_block,
        out_shape=jax.ShapeDtypeStruct(x.shape, x.dtype),
        scratch_shapes=[
            pltpu.VMEM((seq, SB_WIDTH), _BF16),
            pltpu.VMEM((seq, SB_WIDTH), _BF16),
            pltpu.VMEM((GLA_HEADS // 2, GLA_VAL_DIM, LANES), _F32),
            pltpu.VMEM((2, TQ, 1), _F32),
            pltpu.VMEM((2, TQ, LANES), _F32),
        ],
        compiler_params=pltpu.CompilerParams(
            dimension_semantics=("arbitrary", "arbitrary"), vmem_limit_bytes=VMEM_LIMIT_BYTES),
        name="hybrid_mixer_block",
    )(x, g_pre, w_packed, b_gate, w_up, b_alpha_up, g_gla_norm,
      w_o_sb[0].astype(_BF16), w_o_gla[0].astype(_BF16), w_out[0].astype(_BF16), g_final.reshape(1, D_MODEL))
```

```python
import math

import jax
import jax.numpy as jnp
from jax import lax
from jax.experimental import pallas as pl
from jax.experimental.pallas import tpu as pltpu

D_MODEL = 1024
CHUNK = 64
SB_HEAD_DIM = 64
SB_WIDTH = D_MODEL // 2
GLA_HEADS = 4
GLA_K_WIDTH = D_MODEL // 4
GLA_V_WIDTH = D_MODEL // 2
GLA_KEY_DIM = GLA_K_WIDTH // GLA_HEADS
GLA_VAL_DIM = GLA_V_WIDTH // GLA_HEADS
GLA_RANK = 16
GLA_GATE_TEMP = 16.0
EPS = 1e-6
LOG2_E = math.log2(math.e)

LANES = 128
SB_PAIRS = SB_WIDTH // LANES
TQ = 256
LR_PAD = LANES

OFF_SB_Q = 0
OFF_SB_K = OFF_SB_Q + SB_WIDTH
OFF_SB_V = OFF_SB_K + SB_WIDTH
OFF_SB_Z = OFF_SB_V + SB_WIDTH
OFF_GLA_Q = OFF_SB_Z + SB_WIDTH
OFF_GLA_K = OFF_GLA_Q + GLA_K_WIDTH
OFF_GLA_V = OFF_GLA_K + GLA_K_WIDTH
OFF_GLA_Z = OFF_GLA_V + GLA_V_WIDTH
OFF_LR = OFF_GLA_Z + GLA_V_WIDTH
OFF_GATE = OFF_LR + LR_PAD
N_PACKED = OFF_GATE + 2 * D_MODEL

VMEM_LIMIT_BYTES = 56 * 1024 * 1024

_F32 = jnp.float32
_BF16 = jnp.bfloat16
_NT = (((1,), (1,)), ((), ()))
_TN = (((0,), (0,)), ((), ()))


def _dot(a, b):
    return jnp.dot(a, b, preferred_element_type=_F32)


def _log_sigmoid(x):
    return jnp.minimum(x, 0.0) - jnp.log(1.0 + jnp.exp(-jnp.abs(x)))


def _silu(x):
    return x * jax.nn.sigmoid(x)


def _rms(x, gain):
    return x * lax.rsqrt(jnp.mean(x * x, axis=-1, keepdims=True) + EPS) * gain


def _sb_key_block(start, qs_ref, k_all, vm_all, incl, nc_ref, acc_ref, mask):
    for pair in range(SB_PAIRS):
        cols = slice(pair * LANES, (pair + 1) * LANES)
        k2 = k_all[pl.ds(start, TQ), cols]
        zz = lax.dot_general(qs_ref[pair], k2, _NT, preferred_element_type=_F32)
        neg_abs = lax.bitcast_convert_type(
            lax.bitcast_convert_type(zz, jnp.uint32) | jnp.uint32(0x80000000), _F32)
        nl = jnp.maximum(zz, 0.0) + jnp.log(1.0 + jnp.exp2(neg_abs)) * LOG2_E
        if mask is not None:
            nl = jnp.where(mask, nl, 0.0)
        nc = nc_ref[pair]
        tot = _dot(nl.astype(_BF16), incl) + jnp.concatenate([nc] * (TQ // LANES), axis=1)
        w = jnp.exp2(zz - tot)
        if mask is not None:
            w = jnp.where(mask, w, 0.0)
        w_cat = jnp.concatenate([w[:TQ], w[TQ:]], axis=1).astype(_BF16)
        v_cat = jnp.concatenate([vm_all[0, pl.ds(start, TQ), cols], vm_all[1, pl.ds(start, TQ), cols]], axis=0)
        acc_ref[pair] += _dot(w_cat, v_cat)
        nc_ref[pair] = jnp.broadcast_to(tot[:, 0:1], (2 * TQ, LANES))


def _stick_breaking(q, k_all, vm_all, step, qs_ref, nc_ref, acc_ref):
    row = lax.broadcasted_iota(jnp.int32, (TQ, TQ), 0)
    col = lax.broadcasted_iota(jnp.int32, (TQ, TQ), 1)
    incl = (row >= col).astype(_BF16)
    row2 = lax.broadcasted_iota(jnp.int32, (2 * TQ, TQ), 0) & (TQ - 1)
    col2 = lax.broadcasted_iota(jnp.int32, (2 * TQ, TQ), 1)
    causal = col2 < row2
    first_head = lax.broadcasted_iota(jnp.int32, (1, LANES), 1) < SB_HEAD_DIM
    for pair in range(SB_PAIRS):
        q2 = q[:, pair * LANES:(pair + 1) * LANES]
        qs_ref[pair] = jnp.concatenate([jnp.where(first_head, q2, 0), jnp.where(first_head, 0, q2)], axis=0)
    nc_ref[...] = jnp.zeros_like(nc_ref)
    acc_ref[...] = jnp.zeros_like(acc_ref)
    _sb_key_block(pl.multiple_of(step * TQ, TQ), qs_ref, k_all, vm_all, incl, nc_ref, acc_ref, causal)

    def earlier_block(t, _):
        _sb_key_block(pl.multiple_of((step - 1 - t) * TQ, TQ), qs_ref, k_all, vm_all, incl, nc_ref, acc_ref, None)
        return 0

    lax.fori_loop(0, step, earlier_block, 0)
    return jnp.concatenate([acc_ref[pair] for pair in range(SB_PAIRS)], axis=1)


def _gla(q, k, v, log_alpha, state_ref, gain):
    row = lax.broadcasted_iota(jnp.int32, (TQ, TQ), 0)
    col = lax.broadcasted_iota(jnp.int32, (TQ, TQ), 1)
    same_chunk = (row // CHUNK) == (col // CHUNK)
    causal = jnp.logical_and(same_chunk, col <= row)
    la_hi = log_alpha.astype(_BF16)
    la_lo = (log_alpha - la_hi.astype(_F32)).astype(_BF16)
    prefix = causal.astype(_BF16)
    total = same_chunk.astype(_BF16)
    cum = _dot(prefix, la_hi) + _dot(prefix, la_lo)
    cum_last = _dot(total, la_hi) + _dot(total, la_lo)
    q_dec = (q * jnp.exp(cum) * (GLA_KEY_DIM ** -0.5)).astype(_BF16)
    k_inv = (k * jnp.exp(-cum)).astype(_BF16)
    k_end = (k * jnp.exp(cum_last - cum)).astype(_BF16)
    decay = jnp.exp(cum_last)
    v_bf = v.astype(_BF16)

    lane = lax.broadcasted_iota(jnp.int32, (1, LANES), 1)
    first_head = lane < GLA_KEY_DIM
    outs = [None] * GLA_HEADS
    for pair in range(GLA_HEADS // 2):
        cols = slice(pair * LANES, (pair + 1) * LANES)
        qd2 = q_dec[:, cols]
        halves = (jnp.where(first_head, qd2, 0), jnp.where(first_head, 0, qd2))
        ki2 = k_inv[:, cols]
        heads = (2 * pair, 2 * pair + 1)
        vals = [v_bf[:, h * GLA_VAL_DIM:(h + 1) * GLA_VAL_DIM] for h in heads]
        intra = []
        for slot in range(2):
            scores = lax.dot_general(halves[slot], ki2, _NT, preferred_element_type=_F32)
            scores = jnp.where(causal, scores, 0.0).astype(_BF16)
            intra.append(_dot(scores, vals[slot]))
        inter = ([], [])
        state = state_ref[pair]
        for c in range(TQ // CHUNK):
            rows = slice(c * CHUNK, (c + 1) * CHUNK)
            state_bf = state.astype(_BF16)
            for slot in range(2):
                inter[slot].append(
                    lax.dot_general(halves[slot][rows], state_bf, _NT, preferred_element_type=_F32))
            ke = k_end[rows, cols]
            kv = [lax.dot_general(vals[slot][rows], ke, _TN, preferred_element_type=_F32) for slot in range(2)]
            state = state * decay[c * CHUNK:c * CHUNK + 1, cols] + jnp.where(first_head, kv[0], kv[1])
        state_ref[pair] = state
        for slot in range(2):
            o = intra[slot] + jnp.concatenate(inter[slot], axis=0)
            outs[heads[slot]] = _rms(o, gain)
    return jnp.concatenate(outs, axis=1)


def _block_kernel(x_ref, g_pre_ref, w_in_ref, b_gate_ref, w_up_ref, b_up_ref, g_gla_ref,
                  w_o_sb_ref, w_o_gla_ref, w_out_ref, g_final_ref, out_ref,
                  k_all, vm_all, state_ref, qs_ref, nc_ref, acc_ref):
    step = pl.program_id(1)

    @pl.when(step == 0)
    def _():
        state_ref[...] = jnp.zeros_like(state_ref)

    x = x_ref[0]
    h = _rms(x, g_pre_ref[...]).astype(_BF16)

    def proj(offset, width):
        return _dot(h, w_in_ref[:, offset:offset + width])

    rows = pl.ds(pl.multiple_of(step * TQ, TQ), TQ)
    k_all[rows, :] = proj(OFF_SB_K, SB_WIDTH).astype(_BF16)
    sb_v = proj(OFF_SB_V, SB_WIDTH).astype(_BF16)
    first_head = (lax.broadcasted_iota(jnp.int32, (1, SB_WIDTH), 1) & (LANES - 1)) < SB_HEAD_DIM
    vm_all[0, rows, :] = jnp.where(first_head, sb_v, 0)
    vm_all[1, rows, :] = jnp.where(first_head, 0, sb_v)
    sb_q = (proj(OFF_SB_Q, SB_WIDTH) * (LOG2_E * SB_HEAD_DIM ** -0.5)).astype(_BF16)
    o_sb = _stick_breaking(sb_q, k_all, vm_all, step, qs_ref, nc_ref, acc_ref)
    y_sb = _dot((o_sb * _silu(proj(OFF_SB_Z, SB_WIDTH))).astype(_BF16), w_o_sb_ref[...])

    lr = proj(OFF_LR, LR_PAD).astype(_BF16)
    alpha_logit = _dot(lr, w_up_ref[...]) + b_up_ref[...]
    log_alpha = _log_sigmoid(alpha_logit) / GLA_GATE_TEMP
    o_gla = _gla(proj(OFF_GLA_Q, GLA_K_WIDTH), proj(OFF_GLA_K, GLA_K_WIDTH), proj(OFF_GLA_V, GLA_V_WIDTH),
                 log_alpha, state_ref, g_gla_ref[...])
    y_gla = _dot((o_gla * _silu(proj(OFF_GLA_Z, GLA_V_WIDTH))).astype(_BF16), w_o_gla_ref[...])

    gates = jax.nn.sigmoid(proj(OFF_GATE, 2 * D_MODEL) + b_gate_ref[...])
    merged = gates[:, :D_MODEL] * y_sb + gates[:, D_MODEL:] * y_gla
    res = x + _dot(merged.astype(_BF16), w_out_ref[...])
    out_ref[0] = _rms(res, g_final_ref[...])


def _resident(shape):
    return pl.BlockSpec(shape, lambda b, i: (0,) * len(shape), pipeline_mode=pl.Buffered(1))


@jax.jit
def kernel(x, g_pre, w_in, b_gate, w_alpha_up, b_alpha_up, g_gla_norm, w_o_sb, w_o_gla, w_out, g_final):
    batch, seq, d_model = x.shape
    depth = w_in.shape[0]
    assert d_model == D_MODEL and seq % TQ == 0 and depth == 1
    n_lr = OFF_LR + GLA_RANK
    w = w_in[0]
    w_packed = jnp.concatenate(
        [w[:, :n_lr], jnp.zeros((D_MODEL, LR_PAD - GLA_RANK), w.dtype), w[:, n_lr:]], axis=1).astype(_BF16)
    w_up = jnp.concatenate(
        [w_alpha_up[0], jnp.zeros((LR_PAD - GLA_RANK, GLA_K_WIDTH), w_alpha_up.dtype)], axis=0).astype(_BF16)

    row_block = pl.BlockSpec((1, TQ, D_MODEL), lambda b, i: (b, i, 0))
    return pl.pallas_call(
        _block_kernel,
        grid=(batch, seq // TQ),
        in_specs=[
            row_block,
            _resident((1, D_MODEL)),
            _resident((D_MODEL, N_PACKED)),
            _resident((1, 2 * D_MODEL)),
            _resident((LR_PAD, GLA_K_WIDTH)),
            _resident((1, GLA_K_WIDTH)),
            _resident((1, GLA_VAL_DIM)),
            _resident((SB_WIDTH, D_MODEL)),
            _resident((GLA_V_WIDTH, D_MODEL)),
            _resident((D_MODEL, D_MODEL)),
            _resident((1, D_MODEL)),
        ],
        out_specs=row_block,
        out_shape=jax.ShapeDtypeStruct(x.shape, x.dtype),
        scratch_shapes=[
            pltpu.VMEM((seq, SB_WIDTH), _BF16),
            pltpu.VMEM((2, seq, SB_WIDTH), _BF16),
            pltpu.VMEM((GLA_HEADS // 2, GLA_VAL_DIM, LANES), _F32),
            pltpu.VMEM((SB_PAIRS, 2 * TQ, LANES), _BF16),
            pltpu.VMEM((SB_PAIRS, 2 * TQ, LANES), _F32),
            pltpu.VMEM((SB_PAIRS, TQ, LANES), _F32),
        ],
        compiler_params=pltpu.CompilerParams(
            dimension_semantics=("arbitrary", "arbitrary"), vmem_limit_bytes=VMEM_LIMIT_BYTES),
        name="hybrid_mixer_block",
    )(x, g_pre, w_packed, b_gate, w_up, b_alpha_up, g_gla_norm,
      w_o_sb[0].astype(_BF16), w_o_gla[0].astype(_BF16), w_out[0].astype(_BF16), g_final.reshape(1, D_MODEL))
```

```python
import math

import jax
import jax.numpy as jnp
from jax import lax
from jax.experimental import pallas as pl
from jax.experimental.pallas import tpu as pltpu

D_MODEL = 1024
CHUNK = 64
SB_HEAD_DIM = 64
SB_WIDTH = D_MODEL // 2
GLA_HEADS = 4
GLA_K_WIDTH = D_MODEL // 4
GLA_V_WIDTH = D_MODEL // 2
GLA_KEY_DIM = GLA_K_WIDTH // GLA_HEADS
GLA_VAL_DIM = GLA_V_WIDTH // GLA_HEADS
GLA_RANK = 16
GLA_GATE_TEMP = 16.0
EPS = 1e-6
LOG2_E = math.log2(math.e)
EXP2_CLAMP = 126.0
SB_DEAD_LOG2 = 151.0

LANES = 128
SB_PAIRS = SB_WIDTH // LANES
TQ = 256
LR_PAD = LANES

OFF_SB_Q = 0
OFF_SB_K = OFF_SB_Q + SB_WIDTH
OFF_SB_V = OFF_SB_K + SB_WIDTH
OFF_SB_Z = OFF_SB_V + SB_WIDTH
OFF_GLA_Q = OFF_SB_Z + SB_WIDTH
OFF_GLA_K = OFF_GLA_Q + GLA_K_WIDTH
OFF_GLA_V = OFF_GLA_K + GLA_K_WIDTH
OFF_GLA_Z = OFF_GLA_V + GLA_V_WIDTH
OFF_LR = OFF_GLA_Z + GLA_V_WIDTH
OFF_GATE = OFF_LR + LR_PAD
N_PACKED = OFF_GATE + 2 * D_MODEL

VMEM_LIMIT_BYTES = 56 * 1024 * 1024

_F32 = jnp.float32
_BF16 = jnp.bfloat16
_NT = (((1,), (1,)), ((), ()))
_TN = (((0,), (0,)), ((), ()))


def _dot(a, b):
    return jnp.dot(a, b, preferred_element_type=_F32)


def _log_sigmoid(x):
    return jnp.minimum(x, 0.0) - jnp.log(1.0 + jnp.exp(-jnp.abs(x)))


def _silu(x):
    return x * jax.nn.sigmoid(x)


def _rms(x, gain):
    return x * lax.rsqrt(jnp.mean(x * x, axis=-1, keepdims=True) + EPS) * gain


def _sb_key_block(start, qs_ref, k_all, vm_all, incl, nc_ref, acc_ref, mask):
    for pair in range(SB_PAIRS):
        cols = slice(pair * LANES, (pair + 1) * LANES)
        k2 = k_all[pl.ds(start, TQ), cols]
        zz = lax.dot_general(qs_ref[pair], k2, _NT, preferred_element_type=_F32)
        nl = jnp.maximum(jnp.log(1.0 + jnp.exp2(jnp.minimum(zz, EXP2_CLAMP))) * LOG2_E, zz)
        if mask is not None:
            nl = jnp.where(mask, nl, 0.0)
        nc = nc_ref[pair]
        tot = _dot(nl.astype(_BF16), incl) + jnp.concatenate([nc] * (TQ // LANES), axis=1)
        w = jnp.exp2(zz - tot)
        if mask is not None:
            w = jnp.where(mask, w, 0.0)
        w_cat = jnp.concatenate([w[:TQ], w[TQ:]], axis=1).astype(_BF16)
        v_cat = jnp.concatenate([vm_all[0, pl.ds(start, TQ), cols], vm_all[1, pl.ds(start, TQ), cols]], axis=0)
        acc_ref[pair] += _dot(w_cat, v_cat)
        nc_ref[pair] = jnp.broadcast_to(tot[:, 0:1], (2 * TQ, LANES))


def _stick_breaking(q, k_all, vm_all, step, qs_ref, nc_ref, acc_ref):
    row = lax.broadcasted_iota(jnp.int32, (TQ, TQ), 0)
    col = lax.broadcasted_iota(jnp.int32, (TQ, TQ), 1)
    incl = (row >= col).astype(_BF16)
    row2 = lax.broadcasted_iota(jnp.int32, (2 * TQ, TQ), 0) & (TQ - 1)
    col2 = lax.broadcasted_iota(jnp.int32, (2 * TQ, TQ), 1)
    causal = col2 < row2
    first_head = lax.broadcasted_iota(jnp.int32, (1, LANES), 1) < SB_HEAD_DIM
    for pair in range(SB_PAIRS):
        q2 = q[:, pair * LANES:(pair + 1) * LANES]
        qs_ref[pair] = jnp.concatenate([jnp.where(first_head, q2, 0), jnp.where(first_head, 0, q2)], axis=0)
    nc_ref[...] = jnp.zeros_like(nc_ref)
    acc_ref[...] = jnp.zeros_like(acc_ref)
    _sb_key_block(pl.multiple_of(step * TQ, TQ), qs_ref, k_all, vm_all, incl, nc_ref, acc_ref, causal)

    def more_blocks(carry):
        t, live = carry
        return jnp.logical_and(t < step, live)

    def earlier_block(carry):
        t, _ = carry
        _sb_key_block(pl.multiple_of((step - 1 - t) * TQ, TQ), qs_ref, k_all, vm_all, incl, nc_ref, acc_ref, None)
        return t + 1, jnp.min(nc_ref[...]) < SB_DEAD_LOG2

    lax.while_loop(more_blocks, earlier_block, (jnp.int32(0), jnp.bool_(True)))
    return jnp.concatenate([acc_ref[pair] for pair in range(SB_PAIRS)], axis=1)


def _gla(q, k, v, log_alpha, state_ref, gain):
    row = lax.broadcasted_iota(jnp.int32, (TQ, TQ), 0)
    col = lax.broadcasted_iota(jnp.int32, (TQ, TQ), 1)
    same_chunk = (row // CHUNK) == (col // CHUNK)
    causal = jnp.logical_and(same_chunk, col <= row)
    la_hi = log_alpha.astype(_BF16)
    la_lo = (log_alpha - la_hi.astype(_F32)).astype(_BF16)
    prefix = causal.astype(_BF16)
    total = same_chunk.astype(_BF16)
    cum = _dot(prefix, la_hi) + _dot(prefix, la_lo)
    cum_last = _dot(total, la_hi) + _dot(total, la_lo)
    q_dec = (q * jnp.exp(cum) * (GLA_KEY_DIM ** -0.5)).astype(_BF16)
    k_inv = (k * jnp.exp(-cum)).astype(_BF16)
    k_end = (k * jnp.exp(cum_last - cum)).astype(_BF16)
    decay = jnp.exp(cum_last)
    v_bf = v.astype(_BF16)

    lane = lax.broadcasted_iota(jnp.int32, (1, LANES), 1)
    first_head = lane < GLA_KEY_DIM
    outs = [None] * GLA_HEADS
    for pair in range(GLA_HEADS // 2):
        cols = slice(pair * LANES, (pair + 1) * LANES)
        qd2 = q_dec[:, cols]
        halves = (jnp.where(first_head, qd2, 0), jnp.where(first_head, 0, qd2))
        ki2 = k_inv[:, cols]
        heads = (2 * pair, 2 * pair + 1)
        vals = [v_bf[:, h * GLA_VAL_DIM:(h + 1) * GLA_VAL_DIM] for h in heads]
        intra = []
        for slot in range(2):
            scores = lax.dot_general(halves[slot], ki2, _NT, preferred_element_type=_F32)
            scores = jnp.where(causal, scores, 0.0).astype(_BF16)
            intra.append(_dot(scores, vals[slot]))
        inter = ([], [])
        state = state_ref[pair]
        for c in range(TQ // CHUNK):
            rows = slice(c * CHUNK, (c + 1) * CHUNK)
            state_bf = state.astype(_BF16)
            for slot in range(2):
                inter[slot].append(
                    lax.dot_general(halves[slot][rows], state_bf, _NT, preferred_element_type=_F32))
            ke = k_end[rows, cols]
            kv = [lax.dot_general(vals[slot][rows], ke, _TN, preferred_element_type=_F32) for slot in range(2)]
            state = state * decay[c * CHUNK:c * CHUNK + 1, cols] + jnp.where(first_head, kv[0], kv[1])
        state_ref[pair] = state
        for slot in range(2):
            o = intra[slot] + jnp.concatenate(inter[slot], axis=0)
            outs[heads[slot]] = _rms(o, gain)
    return jnp.concatenate(outs, axis=1)


def _block_kernel(x_ref, g_pre_ref, w_in_ref, b_gate_ref, w_up_ref, b_up_ref, g_gla_ref,
                  w_o_sb_ref, w_o_gla_ref, w_out_ref, g_final_ref, out_ref,
                  k_all, vm_all, state_ref, qs_ref, nc_ref, acc_ref):
    step = pl.program_id(1)

    @pl.when(step == 0)
    def _():
        state_ref[...] = jnp.zeros_like(state_ref)

    x = x_ref[0]
    h = _rms(x, g_pre_ref[...]).astype(_BF16)

    def proj(offset, width):
        return _dot(h, w_in_ref[:, offset:offset + width])

    rows = pl.ds(pl.multiple_of(step * TQ, TQ), TQ)
    k_all[rows, :] = proj(OFF_SB_K, SB_WIDTH).astype(_BF16)
    sb_v = proj(OFF_SB_V, SB_WIDTH).astype(_BF16)
    first_head = (lax.broadcasted_iota(jnp.int32, (1, SB_WIDTH), 1) & (LANES - 1)) < SB_HEAD_DIM
    vm_all[0, rows, :] = jnp.where(first_head, sb_v, 0)
    vm_all[1, rows, :] = jnp.where(first_head, 0, sb_v)
    sb_q = (proj(OFF_SB_Q, SB_WIDTH) * (LOG2_E * SB_HEAD_DIM ** -0.5)).astype(_BF16)
    o_sb = _stick_breaking(sb_q, k_all, vm_all, step, qs_ref, nc_ref, acc_ref)
    y_sb = _dot((o_sb * _silu(proj(OFF_SB_Z, SB_WIDTH))).astype(_BF16), w_o_sb_ref[...])

    lr = proj(OFF_LR, LR_PAD).astype(_BF16)
    alpha_logit = _dot(lr, w_up_ref[...]) + b_up_ref[...]
    log_alpha = _log_sigmoid(alpha_logit) / GLA_GATE_TEMP
    o_gla = _gla(proj(OFF_GLA_Q, GLA_K_WIDTH), proj(OFF_GLA_K, GLA_K_WIDTH), proj(OFF_GLA_V, GLA_V_WIDTH),
                 log_alpha, state_ref, g_gla_ref[...])
    y_gla = _dot((o_gla * _silu(proj(OFF_GLA_Z, GLA_V_WIDTH))).astype(_BF16), w_o_gla_ref[...])

    gates = jax.nn.sigmoid(proj(OFF_GATE, 2 * D_MODEL) + b_gate_ref[...])
    merged = gates[:, :D_MODEL] * y_sb + gates[:, D_MODEL:] * y_gla
    res = x + _dot(merged.astype(_BF16), w_out_ref[...])
    out_ref[0] = _rms(res, g_final_ref[...])


def _resident(shape):
    return pl.BlockSpec(shape, lambda b, i: (0,) * len(shape), pipeline_mode=pl.Buffered(1))


@jax.jit
def kernel(x, g_pre, w_in, b_gate, w_alpha_up, b_alpha_up, g_gla_norm, w_o_sb, w_o_gla, w_out, g_final):
    batch, seq, d_model = x.shape
    depth = w_in.shape[0]
    assert d_model == D_MODEL and seq % TQ == 0 and depth == 1
    n_lr = OFF_LR + GLA_RANK
    w = w_in[0]
    w_packed = jnp.concatenate(
        [w[:, :n_lr], jnp.zeros((D_MODEL, LR_PAD - GLA_RANK), w.dtype), w[:, n_lr:]], axis=1).astype(_BF16)
    w_up = jnp.concatenate(
        [w_alpha_up[0], jnp.zeros((LR_PAD - GLA_RANK, GLA_K_WIDTH), w_alpha_up.dtype)], axis=0).astype(_BF16)

    row_block = pl.BlockSpec((1, TQ, D_MODEL), lambda b, i: (b, i, 0))
    return pl.pallas_call(
        _block_kernel,
        grid=(batch, seq // TQ),
        in_specs=[
            row_block,
            _resident((1, D_MODEL)),
            _resident((D_MODEL, N_PACKED)),
            _resident((1, 2 * D_MODEL)),
            _resident((LR_PAD, GLA_K_WIDTH)),
            _resident((1, GLA_K_WIDTH)),
            _resident((1, GLA_VAL_DIM)),
            _resident((SB_WIDTH, D_MODEL)),
            _resident((GLA_V_WIDTH, D_MODEL)),
            _resident((D_MODEL, D_MODEL)),
            _resident((1, D_MODEL)),
        ],
        out_specs=row_block,
        out_shape=jax.ShapeDtypeStruct(x.shape, x.dtype),
        scratch_shapes=[
            pltpu.VMEM((seq, SB_WIDTH), _BF16),
            pltpu.VMEM((2, seq, SB_WIDTH), _BF16),
            pltpu.VMEM((GLA_HEADS // 2, GLA_VAL_DIM, LANES), _F32),
            pltpu.VMEM((SB_PAIRS, 2 * TQ, LANES), _BF16),
            pltpu.VMEM((SB_PAIRS, 2 * TQ, LANES), _F32),
            pltpu.VMEM((SB_PAIRS, TQ, LANES), _F32),
        ],
        compiler_params=pltpu.CompilerParams(
            dimension_semantics=("arbitrary", "arbitrary"), vmem_limit_bytes=VMEM_LIMIT_BYTES),
        name="hybrid_mixer_block",
    )(x, g_pre, w_packed, b_gate, w_up, b_alpha_up, g_gla_norm,
      w_o_sb[0].astype(_BF16), w_o_gla[0].astype(_BF16), w_out[0].astype(_BF16), g_final.reshape(1, D_MODEL))
```

```python
import math

import jax
import jax.numpy as jnp
from jax import lax
from jax.experimental import pallas as pl
from jax.experimental.pallas import tpu as pltpu

D_MODEL = 1024
CHUNK = 64
SB_HEAD_DIM = 64
SB_WIDTH = D_MODEL // 2
GLA_HEADS = 4
GLA_K_WIDTH = D_MODEL // 4
GLA_V_WIDTH = D_MODEL // 2
GLA_KEY_DIM = GLA_K_WIDTH // GLA_HEADS
GLA_VAL_DIM = GLA_V_WIDTH // GLA_HEADS
GLA_RANK = 16
GLA_GATE_TEMP = 16.0
EPS = 1e-6
LOG2_E = math.log2(math.e)
EXP2_CLAMP = 126.0
SB_DEAD_LOG2 = 151.0

LANES = 128
SB_PAIRS = SB_WIDTH // LANES
TQ = 256
N_CHUNKS = TQ // CHUNK
LR_PAD = LANES

OFF_SB_Q = 0
OFF_SB_K = OFF_SB_Q + SB_WIDTH
OFF_SB_V = OFF_SB_K + SB_WIDTH
OFF_SB_Z = OFF_SB_V + SB_WIDTH
OFF_GLA_Q = OFF_SB_Z + SB_WIDTH
OFF_GLA_K = OFF_GLA_Q + GLA_K_WIDTH
OFF_GLA_V = OFF_GLA_K + GLA_K_WIDTH
OFF_GLA_Z = OFF_GLA_V + GLA_V_WIDTH
OFF_LR = OFF_GLA_Z + GLA_V_WIDTH
OFF_GATE = OFF_LR + LR_PAD
N_PACKED = OFF_GATE + 2 * D_MODEL

VMEM_LIMIT_BYTES = 56 * 1024 * 1024

_F32 = jnp.float32
_BF16 = jnp.bfloat16
_NT = (((1,), (1,)), ((), ()))
_TN = (((0,), (0,)), ((), ()))


def _dot(a, b):
    return jnp.dot(a, b, preferred_element_type=_F32)


def _dot_nt(a, b):
    return lax.dot_general(a, b, _NT, preferred_element_type=_F32)


def _log_sigmoid(x):
    return jnp.minimum(x, 0.0) - jnp.log(1.0 + jnp.exp(-jnp.abs(x)))


def _silu(x):
    return x * jax.nn.sigmoid(x)


def _rms(x, gain):
    return x * lax.rsqrt(jnp.mean(x * x, axis=-1, keepdims=True) + EPS) * gain


def _stack_heads(x2):
    first_head = lax.broadcasted_iota(jnp.int32, (1, LANES), 1) < SB_HEAD_DIM
    return jnp.concatenate([jnp.where(first_head, x2, 0), jnp.where(first_head, 0, x2)], axis=0)


def _sb_key_block(start, qs_ref, k_all, vm_all, incl, nc_ref, acc_ref, mask):
    for pair in range(SB_PAIRS):
        cols = slice(pair * LANES, (pair + 1) * LANES)
        zz = _dot_nt(qs_ref[pair], k_all[pl.ds(start, TQ), cols])
        nl = jnp.maximum(jnp.log(1.0 + jnp.exp2(jnp.minimum(zz, EXP2_CLAMP))) * LOG2_E, zz)
        if mask is not None:
            nl = jnp.where(mask, nl, 0.0)
        nc = nc_ref[pair]
        tot = _dot(nl.astype(_BF16), incl) + jnp.concatenate([nc] * (TQ // LANES), axis=1)
        w = jnp.exp2(zz - tot)
        if mask is not None:
            w = jnp.where(mask, w, 0.0)
        w_cat = jnp.concatenate([w[:TQ], w[TQ:]], axis=1).astype(_BF16)
        v_cat = jnp.concatenate([vm_all[0, pl.ds(start, TQ), cols], vm_all[1, pl.ds(start, TQ), cols]], axis=0)
        acc_ref[pair] += _dot(w_cat, v_cat)
        nc_ref[pair] = jnp.broadcast_to(tot[:, 0:1], (2 * TQ, LANES))


def _sb_incl():
    row = lax.broadcasted_iota(jnp.int32, (TQ, TQ), 0)
    col = lax.broadcasted_iota(jnp.int32, (TQ, TQ), 1)
    return (row >= col).astype(_BF16)


def _sb_near_blocks(q, k_all, vm_all, step, qs_ref, nc_ref, acc_ref):
    row2 = lax.broadcasted_iota(jnp.int32, (2 * TQ, TQ), 0) & (TQ - 1)
    col2 = lax.broadcasted_iota(jnp.int32, (2 * TQ, TQ), 1)
    for pair in range(SB_PAIRS):
        qs_ref[pair] = _stack_heads(q[:, pair * LANES:(pair + 1) * LANES])
    nc_ref[...] = jnp.zeros_like(nc_ref)
    acc_ref[...] = jnp.zeros_like(acc_ref)
    incl = _sb_incl()
    _sb_key_block(pl.multiple_of((step + 1) * TQ, TQ), qs_ref, k_all, vm_all, incl, nc_ref, acc_ref, col2 < row2)
    _sb_key_block(pl.multiple_of(step * TQ, TQ), qs_ref, k_all, vm_all, incl, nc_ref, acc_ref, None)


def _sb_far_blocks(k_all, vm_all, step, qs_ref, nc_ref, acc_ref):
    def more_blocks(carry):
        t, live = carry
        return jnp.logical_and(t < step - 1, live)

    def earlier_block(carry):
        t, _ = carry
        _sb_key_block(pl.multiple_of((step - 1 - t) * TQ, TQ), qs_ref, k_all, vm_all, _sb_incl(),
                      nc_ref, acc_ref, None)
        return t + 1, jnp.min(nc_ref[...]) < SB_DEAD_LOG2

    lax.while_loop(more_blocks, earlier_block, (jnp.int32(0), jnp.min(nc_ref[...]) < SB_DEAD_LOG2))


def _chunk_block_diag(x):
    zero = jnp.zeros((CHUNK, LANES), x.dtype)
    out_rows = []
    for r in range(x.shape[0] // CHUNK):
        blocks = [x[r * CHUNK:(r + 1) * CHUNK] if c == r % N_CHUNKS else zero for c in range(N_CHUNKS)]
        out_rows.append(jnp.concatenate(blocks, axis=1))
    return jnp.concatenate(out_rows, axis=0)


def _gla(q, k, v, log_alpha, state_ref, gain):
    row = lax.broadcasted_iota(jnp.int32, (TQ, TQ), 0)
    col = lax.broadcasted_iota(jnp.int32, (TQ, TQ), 1)
    prefix = jnp.logical_and(row // CHUNK == col // CHUNK, col <= row).astype(_BF16)
    la_hi = log_alpha.astype(_BF16)
    la_lo = (log_alpha - la_hi.astype(_F32)).astype(_BF16)
    cum = _dot(prefix, la_hi) + _dot(prefix, la_lo)
    chunk_total = [cum[(c + 1) * CHUNK - 1:(c + 1) * CHUNK] for c in range(N_CHUNKS)]
    cum_last = jnp.concatenate([jnp.broadcast_to(t, (CHUNK, GLA_K_WIDTH)) for t in chunk_total], axis=0)
    q_dec = (q * jnp.exp(cum) * (GLA_KEY_DIM ** -0.5)).astype(_BF16)
    k_inv = (k * jnp.exp(-cum)).astype(_BF16)
    k_end = (k * jnp.exp(cum_last - cum)).astype(_BF16)
    decay = [jnp.exp(t) for t in chunk_total]
    v_bf = v.astype(_BF16)

    row2 = lax.broadcasted_iota(jnp.int32, (2 * TQ, TQ), 0) & (TQ - 1)
    col2 = lax.broadcasted_iota(jnp.int32, (2 * TQ, TQ), 1)
    causal2 = jnp.logical_and(row2 // CHUNK == col2 // CHUNK, col2 <= row2)
    first_head = lax.broadcasted_iota(jnp.int32, (1, LANES), 1) < GLA_KEY_DIM
    outs = []
    for pair in range(GLA_HEADS // 2):
        cols = slice(pair * LANES, (pair + 1) * LANES)
        v_pair = v_bf[:, 2 * pair * GLA_VAL_DIM:(2 * pair + 2) * GLA_VAL_DIM]
        qs = _stack_heads(q_dec[:, cols])
        scores = jnp.where(causal2, _dot_nt(qs, k_inv[:, cols]), 0.0).astype(_BF16)
        intra = (_dot(scores[:TQ], v_pair[:, :GLA_VAL_DIM]), _dot(scores[TQ:], v_pair[:, GLA_VAL_DIM:]))
        kv_all = lax.dot_general(v_pair, _chunk_block_diag(k_end[:, cols]), _TN, preferred_element_type=_F32)
        states = [state_ref[pair]]
        for c in range(N_CHUNKS):
            blk = slice(c * LANES, (c + 1) * LANES)
            kv = jnp.where(first_head, kv_all[:GLA_VAL_DIM, blk], kv_all[GLA_VAL_DIM:, blk])
            states.append(states[-1] * decay[c][:, cols] + kv)
        state_ref[pair] = states[-1]
        state_cat = jnp.concatenate(states[:N_CHUNKS], axis=1).astype(_BF16)
        inter = _dot_nt(_chunk_block_diag(qs), state_cat)
        outs.append(_rms(intra[0] + inter[:TQ], gain))
        outs.append(_rms(intra[1] + inter[TQ:], gain))
    return jnp.concatenate(outs, axis=1)


def _block_kernel(x_ref, g_pre_ref, w_in_ref, b_gate_ref, w_up_ref, b_up_ref, g_gla_ref,
                  w_o_sb_ref, w_o_gla_ref, w_out_ref, g_final_ref, out_ref,
                  k_all, vm_all, state_ref, qs_ref, nc_ref, acc_ref):
    step = pl.program_id(1)

    @pl.when(step == 0)
    def _():
        state_ref[...] = jnp.zeros_like(state_ref)
        k_all[0:TQ, :] = jnp.zeros((TQ, SB_WIDTH), _BF16)
        vm_all[:, 0:TQ, :] = jnp.zeros((2, TQ, SB_WIDTH), _BF16)

    x = x_ref[0]
    h = _rms(x, g_pre_ref[...]).astype(_BF16)

    def proj(offset, width):
        return _dot(h, w_in_ref[:, offset:offset + width])

    rows = pl.ds(pl.multiple_of((step + 1) * TQ, TQ), TQ)
    k_all[rows, :] = proj(OFF_SB_K, SB_WIDTH).astype(_BF16)
    sb_v = proj(OFF_SB_V, SB_WIDTH).astype(_BF16)
    first_head = (lax.broadcasted_iota(jnp.int32, (1, SB_WIDTH), 1) & (LANES - 1)) < SB_HEAD_DIM
    vm_all[0, rows, :] = jnp.where(first_head, sb_v, 0)
    vm_all[1, rows, :] = jnp.where(first_head, 0, sb_v)
    sb_q = (proj(OFF_SB_Q, SB_WIDTH) * (LOG2_E * SB_HEAD_DIM ** -0.5)).astype(_BF16)
    _sb_near_blocks(sb_q, k_all, vm_all, step, qs_ref, nc_ref, acc_ref)

    lr = proj(OFF_LR, LR_PAD).astype(_BF16)
    alpha_logit = _dot(lr, w_up_ref[...]) + b_up_ref[...]
    log_alpha = _log_sigmoid(alpha_logit) / GLA_GATE_TEMP
    o_gla = _gla(proj(OFF_GLA_Q, GLA_K_WIDTH), proj(OFF_GLA_K, GLA_K_WIDTH), proj(OFF_GLA_V, GLA_V_WIDTH),
                 log_alpha, state_ref, g_gla_ref[...])
    y_gla = _dot((o_gla * _silu(proj(OFF_GLA_Z, GLA_V_WIDTH))).astype(_BF16), w_o_gla_ref[...])
    gates = jax.nn.sigmoid(proj(OFF_GATE, 2 * D_MODEL) + b_gate_ref[...])
    gated_gla = gates[:, D_MODEL:] * y_gla
    gate_sb = gates[:, :D_MODEL]
    sb_zs = _silu(proj(OFF_SB_Z, SB_WIDTH))

    _sb_far_blocks(k_all, vm_all, step, qs_ref, nc_ref, acc_ref)
    o_sb = jnp.concatenate([acc_ref[pair] for pair in range(SB_PAIRS)], axis=1)
    y_sb = _dot((o_sb * sb_zs).astype(_BF16), w_o_sb_ref[...])
    merged = gate_sb * y_sb + gated_gla
    res = x + _dot(merged.astype(_BF16), w_out_ref[...])
    out_ref[0] = _rms(res, g_final_ref[...])


def _resident(shape):
    return pl.BlockSpec(shape, lambda b, i: (0,) * len(shape), pipeline_mode=pl.Buffered(1))


@jax.jit
def kernel(x, g_pre, w_in, b_gate, w_alpha_up, b_alpha_up, g_gla_norm, w_o_sb, w_o_gla, w_out, g_final):
    batch, seq, d_model = x.shape
    depth = w_in.shape[0]
    assert d_model == D_MODEL and seq % TQ == 0 and depth == 1
    n_lr = OFF_LR + GLA_RANK
    w = w_in[0]
    w_packed = jnp.concatenate(
        [w[:, :n_lr], jnp.zeros((D_MODEL, LR_PAD - GLA_RANK), w.dtype), w[:, n_lr:]], axis=1).astype(_BF16)
    w_up = jnp.concatenate(
        [w_alpha_up[0], jnp.zeros((LR_PAD - GLA_RANK, GLA_K_WIDTH), w_alpha_up.dtype)], axis=0).astype(_BF16)

    row_block = pl.BlockSpec((1, TQ, D_MODEL), lambda b, i: (b, i, 0))
    return pl.pallas_call(
        _block_kernel,
        grid=(batch, seq // TQ),
        in_specs=[
            row_block,
            _resident((1, D_MODEL)),
            _resident((D_MODEL, N_PACKED)),
            _resident((1, 2 * D_MODEL)),
            _resident((LR_PAD, GLA_K_WIDTH)),
            _resident((1, GLA_K_WIDTH)),
            _resident((1, GLA_VAL_DIM)),
            _resident((SB_WIDTH, D_MODEL)),
            _resident((GLA_V_WIDTH, D_MODEL)),
            _resident((D_MODEL, D_MODEL)),
            _resident((1, D_MODEL)),
        ],
        out_specs=row_block,
        out_shape=jax.ShapeDtypeStruct(x.shape, x.dtype),
        scratch_shapes=[
            pltpu.VMEM((seq + TQ, SB_WIDTH), _BF16),
            pltpu.VMEM((2, seq + TQ, SB_WIDTH), _BF16),
            pltpu.VMEM((GLA_HEADS // 2, GLA_VAL_DIM, LANES), _F32),
            pltpu.VMEM((SB_PAIRS, 2 * TQ, LANES), _BF16),
            pltpu.VMEM((SB_PAIRS, 2 * TQ, LANES), _F32),
            pltpu.VMEM((SB_PAIRS, TQ, LANES), _F32),
        ],
        compiler_params=pltpu.CompilerParams(
            dimension_semantics=("arbitrary", "arbitrary"), vmem_limit_bytes=VMEM_LIMIT_BYTES),
        name="hybrid_mixer_block",
    )(x, g_pre, w_packed, b_gate, w_up, b_alpha_up, g_gla_norm,
      w_o_sb[0].astype(_BF16), w_o_gla[0].astype(_BF16), w_out[0].astype(_BF16), g_final.reshape(1, D_MODEL))
```

```python
import math

import jax
import jax.numpy as jnp
from jax import lax
from jax.experimental import pallas as pl
from jax.experimental.pallas import tpu as pltpu

D_MODEL = 1024
CHUNK = 64
SB_HEAD_DIM = 64
SB_WIDTH = D_MODEL // 2
GLA_HEADS = 4
GLA_K_WIDTH = D_MODEL // 4
GLA_V_WIDTH = D_MODEL // 2
GLA_KEY_DIM = GLA_K_WIDTH // GLA_HEADS
GLA_VAL_DIM = GLA_V_WIDTH // GLA_HEADS
GLA_RANK = 16
GLA_GATE_TEMP = 16.0
EPS = 1e-6
LOG2_E = math.log2(math.e)
EXP2_CLAMP = 126.0
SB_DEAD_LOG2 = 151.0

LANES = 128
SB_PAIRS = SB_WIDTH // LANES
TQ = 256
N_CHUNKS = TQ // CHUNK
LR_PAD = LANES

OFF_SB_Q = 0
OFF_SB_K = OFF_SB_Q + SB_WIDTH
OFF_SB_V = OFF_SB_K + SB_WIDTH
OFF_SB_Z = OFF_SB_V + SB_WIDTH
OFF_GLA_Q = OFF_SB_Z + SB_WIDTH
OFF_GLA_K = OFF_GLA_Q + GLA_K_WIDTH
OFF_GLA_V = OFF_GLA_K + GLA_K_WIDTH
OFF_GLA_Z = OFF_GLA_V + GLA_V_WIDTH
OFF_LR = OFF_GLA_Z + GLA_V_WIDTH

VMEM_LIMIT_BYTES = 56 * 1024 * 1024

_F32 = jnp.float32
_BF16 = jnp.bfloat16
_NT = (((1,), (1,)), ((), ()))
_TN = (((0,), (0,)), ((), ()))


def _dot(a, b):
    return jnp.dot(a, b, preferred_element_type=_F32)


def _dot_nt(a, b):
    return lax.dot_general(a, b, _NT, preferred_element_type=_F32)


def _log_sigmoid(x):
    return jnp.minimum(x, 0.0) - jnp.log(1.0 + jnp.exp(-jnp.abs(x)))


def _silu(x):
    return x * jax.nn.sigmoid(x)


def _rms(x, gain):
    return x * lax.rsqrt(jnp.mean(x * x, axis=-1, keepdims=True) + EPS) * gain


def _stack_heads(x2):
    first_head = lax.broadcasted_iota(jnp.int32, (1, LANES), 1) < SB_HEAD_DIM
    return jnp.concatenate([jnp.where(first_head, x2, 0), jnp.where(first_head, 0, x2)], axis=0)


def _sb_key_block(start, qs_ref, k_all, vm_all, incl, nc_ref, acc_ref, mask):
    for pair in range(SB_PAIRS):
        cols = slice(pair * LANES, (pair + 1) * LANES)
        zz = _dot_nt(qs_ref[pair], k_all[pl.ds(start, TQ), cols])
        nl = jnp.maximum(jnp.log(1.0 + jnp.exp2(jnp.minimum(zz, EXP2_CLAMP))) * LOG2_E, zz)
        if mask is not None:
            nl = jnp.where(mask, nl, 0.0)
        nc = nc_ref[pair]
        tot = _dot(nl.astype(_BF16), incl) + jnp.concatenate([nc] * (TQ // LANES), axis=1)
        w = jnp.exp2(zz - tot)
        if mask is not None:
            w = jnp.where(mask, w, 0.0)
        w_cat = jnp.concatenate([w[:TQ], w[TQ:]], axis=1).astype(_BF16)
        v_cat = jnp.concatenate([vm_all[0, pl.ds(start, TQ), cols], vm_all[1, pl.ds(start, TQ), cols]], axis=0)
        acc_ref[pair] += _dot(w_cat, v_cat)
        nc_ref[pair] = jnp.broadcast_to(tot[:, 0:1], (2 * TQ, LANES))


def _sb_incl():
    row = lax.broadcasted_iota(jnp.int32, (TQ, TQ), 0)
    col = lax.broadcasted_iota(jnp.int32, (TQ, TQ), 1)
    return (row >= col).astype(_BF16)


def _sb_near_blocks(q, k_all, vm_all, step, qs_ref, nc_ref, acc_ref):
    row2 = lax.broadcasted_iota(jnp.int32, (2 * TQ, TQ), 0) & (TQ - 1)
    col2 = lax.broadcasted_iota(jnp.int32, (2 * TQ, TQ), 1)
    for pair in range(SB_PAIRS):
        qs_ref[pair] = _stack_heads(q[:, pair * LANES:(pair + 1) * LANES])
    nc_ref[...] = jnp.zeros_like(nc_ref)
    acc_ref[...] = jnp.zeros_like(acc_ref)
    incl = _sb_incl()
    _sb_key_block(pl.multiple_of((step + 1) * TQ, TQ), qs_ref, k_all, vm_all, incl, nc_ref, acc_ref, col2 < row2)
    _sb_key_block(pl.multiple_of(step * TQ, TQ), qs_ref, k_all, vm_all, incl, nc_ref, acc_ref, None)


def _sb_far_blocks(k_all, vm_all, step, qs_ref, nc_ref, acc_ref):
    def more_blocks(carry):
        t, live = carry
        return jnp.logical_and(t < step - 1, live)

    def earlier_block(carry):
        t, _ = carry
        _sb_key_block(pl.multiple_of((step - 1 - t) * TQ, TQ), qs_ref, k_all, vm_all, _sb_incl(),
                      nc_ref, acc_ref, None)
        return t + 1, jnp.min(nc_ref[...]) < SB_DEAD_LOG2

    lax.while_loop(more_blocks, earlier_block, (jnp.int32(0), jnp.min(nc_ref[...]) < SB_DEAD_LOG2))


def _chunk_block_diag(x):
    zero = jnp.zeros((CHUNK, LANES), x.dtype)
    out_rows = []
    for r in range(x.shape[0] // CHUNK):
        blocks = [x[r * CHUNK:(r + 1) * CHUNK] if c == r % N_CHUNKS else zero for c in range(N_CHUNKS)]
        out_rows.append(jnp.concatenate(blocks, axis=1))
    return jnp.concatenate(out_rows, axis=0)


def _gla(q, k, v, log_alpha, state_ref, gain):
    row = lax.broadcasted_iota(jnp.int32, (TQ, TQ), 0)
    col = lax.broadcasted_iota(jnp.int32, (TQ, TQ), 1)
    prefix = jnp.logical_and(row // CHUNK == col // CHUNK, col <= row).astype(_BF16)
    la_hi = log_alpha.astype(_BF16)
    la_lo = (log_alpha - la_hi.astype(_F32)).astype(_BF16)
    cum = _dot(prefix, la_hi) + _dot(prefix, la_lo)
    chunk_total = [cum[(c + 1) * CHUNK - 1:(c + 1) * CHUNK] for c in range(N_CHUNKS)]
    cum_last = jnp.concatenate([jnp.broadcast_to(t, (CHUNK, GLA_K_WIDTH)) for t in chunk_total], axis=0)
    q_dec = (q * jnp.exp(cum) * (GLA_KEY_DIM ** -0.5)).astype(_BF16)
    k_inv = (k * jnp.exp(-cum)).astype(_BF16)
    k_end = (k * jnp.exp(cum_last - cum)).astype(_BF16)
    decay = [jnp.exp(t) for t in chunk_total]
    v_bf = v.astype(_BF16)

    row2 = lax.broadcasted_iota(jnp.int32, (2 * TQ, TQ), 0) & (TQ - 1)
    col2 = lax.broadcasted_iota(jnp.int32, (2 * TQ, TQ), 1)
    causal2 = jnp.logical_and(row2 // CHUNK == col2 // CHUNK, col2 <= row2)
    first_head = lax.broadcasted_iota(jnp.int32, (1, LANES), 1) < GLA_KEY_DIM
    outs = []
    for pair in range(GLA_HEADS // 2):
        cols = slice(pair * LANES, (pair + 1) * LANES)
        v_pair = v_bf[:, 2 * pair * GLA_VAL_DIM:(2 * pair + 2) * GLA_VAL_DIM]
        qs = _stack_heads(q_dec[:, cols])
        scores = jnp.where(causal2, _dot_nt(qs, k_inv[:, cols]), 0.0).astype(_BF16)
        intra = (_dot(scores[:TQ], v_pair[:, :GLA_VAL_DIM]), _dot(scores[TQ:], v_pair[:, GLA_VAL_DIM:]))
        kv_all = lax.dot_general(v_pair, _chunk_block_diag(k_end[:, cols]), _TN, preferred_element_type=_F32)
        states = [state_ref[pair]]
        for c in range(N_CHUNKS):
            blk = slice(c * LANES, (c + 1) * LANES)
            kv = jnp.where(first_head, kv_all[:GLA_VAL_DIM, blk], kv_all[GLA_VAL_DIM:, blk])
            states.append(states[-1] * decay[c][:, cols] + kv)
        state_ref[pair] = states[-1]
        state_cat = jnp.concatenate(states[:N_CHUNKS], axis=1).astype(_BF16)
        inter = _dot_nt(_chunk_block_diag(qs), state_cat)
        outs.append(_rms(intra[0] + inter[:TQ], gain))
        outs.append(_rms(intra[1] + inter[TQ:], gain))
    return jnp.concatenate(outs, axis=1)


def _block_kernel(x_ref, g_pre_ref, w_in_ref, w_lr_ref, w_gate_ref, b_gate_ref, w_up_ref, b_up_ref, g_gla_ref,
                  w_o_sb_ref, w_o_gla_ref, w_out_ref, g_final_ref, out_ref,
                  k_all, vm_all, state_ref, qs_ref, nc_ref, acc_ref):
    step = pl.program_id(1)

    @pl.when(step == 0)
    def _():
        state_ref[...] = jnp.zeros_like(state_ref)
        k_all[0:TQ, :] = jnp.zeros((TQ, SB_WIDTH), _BF16)
        vm_all[:, 0:TQ, :] = jnp.zeros((2, TQ, SB_WIDTH), _BF16)

    x = x_ref[0]
    h = _rms(x, g_pre_ref[...]).astype(_BF16)

    def proj(offset, width):
        return _dot(h, w_in_ref[:, offset:offset + width])

    rows = pl.ds(pl.multiple_of((step + 1) * TQ, TQ), TQ)
    k_all[rows, :] = proj(OFF_SB_K, SB_WIDTH).astype(_BF16)
    sb_v = proj(OFF_SB_V, SB_WIDTH).astype(_BF16)
    first_head = (lax.broadcasted_iota(jnp.int32, (1, SB_WIDTH), 1) & (LANES - 1)) < SB_HEAD_DIM
    vm_all[0, rows, :] = jnp.where(first_head, sb_v, 0)
    vm_all[1, rows, :] = jnp.where(first_head, 0, sb_v)
    sb_q = (proj(OFF_SB_Q, SB_WIDTH) * (LOG2_E * SB_HEAD_DIM ** -0.5)).astype(_BF16)
    _sb_near_blocks(sb_q, k_all, vm_all, step, qs_ref, nc_ref, acc_ref)

    lr = _dot(h, w_lr_ref[...]).astype(_BF16)
    alpha_logit = _dot(lr, w_up_ref[...]) + b_up_ref[...]
    log_alpha = _log_sigmoid(alpha_logit) / GLA_GATE_TEMP
    o_gla = _gla(proj(OFF_GLA_Q, GLA_K_WIDTH), proj(OFF_GLA_K, GLA_K_WIDTH), proj(OFF_GLA_V, GLA_V_WIDTH),
                 log_alpha, state_ref, g_gla_ref[...])
    y_gla = _dot((o_gla * _silu(proj(OFF_GLA_Z, GLA_V_WIDTH))).astype(_BF16), w_o_gla_ref[...])
    gates = jax.nn.sigmoid(_dot(h, w_gate_ref[...]) + b_gate_ref[...])
    gated_gla = gates[:, D_MODEL:] * y_gla
    gate_sb = gates[:, :D_MODEL]
    sb_zs = _silu(proj(OFF_SB_Z, SB_WIDTH))

    _sb_far_blocks(k_all, vm_all, step, qs_ref, nc_ref, acc_ref)
    o_sb = jnp.concatenate([acc_ref[pair] for pair in range(SB_PAIRS)], axis=1)
    y_sb = _dot((o_sb * sb_zs).astype(_BF16), w_o_sb_ref[...])
    merged = gate_sb * y_sb + gated_gla
    res = x + _dot(merged.astype(_BF16), w_out_ref[...])
    out_ref[0] = _rms(res, g_final_ref[...])


def _resident(shape):
    return pl.BlockSpec(shape, lambda b, i: (0,) * len(shape), pipeline_mode=pl.Buffered(1))


@jax.jit
def kernel(x, g_pre, w_in, b_gate, w_alpha_up, b_alpha_up, g_gla_norm, w_o_sb, w_o_gla, w_out, g_final):
    batch, seq, d_model = x.shape
    depth = w_in.shape[0]
    assert d_model == D_MODEL and seq % TQ == 0 and depth == 1
    w = w_in[0]
    w_main = w[:, :OFF_LR].astype(_BF16)
    w_lr = jnp.pad(w[:, OFF_LR:OFF_LR + GLA_RANK], ((0, 0), (0, LR_PAD - GLA_RANK))).astype(_BF16)
    w_gate = w[:, OFF_LR + GLA_RANK:].astype(_BF16)
    w_up = jnp.pad(w_alpha_up[0], ((0, LR_PAD - GLA_RANK), (0, 0))).astype(_BF16)

    row_block = pl.BlockSpec((1, TQ, D_MODEL), lambda b, i: (b, i, 0))
    return pl.pallas_call(
        _block_kernel,
        grid=(batch, seq // TQ),
        in_specs=[
            row_block,
            _resident((1, D_MODEL)),
            _resident((D_MODEL, OFF_LR)),
            _resident((D_MODEL, LR_PAD)),
            _resident((D_MODEL, 2 * D_MODEL)),
            _resident((1, 2 * D_MODEL)),
            _resident((LR_PAD, GLA_K_WIDTH)),
            _resident((1, GLA_K_WIDTH)),
            _resident((1, GLA_VAL_DIM)),
            _resident((SB_WIDTH, D_MODEL)),
            _resident((GLA_V_WIDTH, D_MODEL)),
            _resident((D_MODEL, D_MODEL)),
            _resident((1, D_MODEL)),
        ],
        out_specs=row_block,
        out_shape=jax.ShapeDtypeStruct(x.shape, x.dtype),
        scratch_shapes=[
            pltpu.VMEM((seq + TQ, SB_WIDTH), _BF16),
            pltpu.VMEM((2, seq + TQ, SB_WIDTH), _BF16),
            pltpu.VMEM((GLA_HEADS // 2, GLA_VAL_DIM, LANES), _F32),
            pltpu.VMEM((SB_PAIRS, 2 * TQ, LANES), _BF16),
            pltpu.VMEM((SB_PAIRS, 2 * TQ, LANES), _F32),
            pltpu.VMEM((SB_PAIRS, TQ, LANES), _F32),
        ],
        compiler_params=pltpu.CompilerParams(
            dimension_semantics=("arbitrary", "arbitrary"), vmem_limit_bytes=VMEM_LIMIT_BYTES),
        name="hybrid_mixer_block",
    )(x, g_pre, w_main, w_lr, w_gate, b_gate, w_up, b_alpha_up, g_gla_norm,
      w_o_sb[0].astype(_BF16), w_o_gla[0].astype(_BF16), w_out[0].astype(_BF16), g_final.reshape(1, D_MODEL))
```

```python
import math

import jax
import jax.numpy as jnp
from jax import lax
from jax.experimental import pallas as pl
from jax.experimental.pallas import tpu as pltpu

D_MODEL = 1024
CHUNK = 64
SB_HEAD_DIM = 64
SB_WIDTH = D_MODEL // 2
GLA_HEADS = 4
GLA_K_WIDTH = D_MODEL // 4
GLA_V_WIDTH = D_MODEL // 2
GLA_KEY_DIM = GLA_K_WIDTH // GLA_HEADS
GLA_VAL_DIM = GLA_V_WIDTH // GLA_HEADS
GLA_RANK = 16
GLA_GATE_TEMP = 16.0
EPS = 1e-6
LOG2_E = math.log2(math.e)
EXP2_CLAMP = 126.0
SB_DEAD_LOG2 = 151.0

LANES = 128
SB_PAIRS = SB_WIDTH // LANES
TQ = 256
N_CHUNKS = TQ // CHUNK
LR_PAD = LANES
WEIGHT_CHUNK_ROWS = 128

OFF_SB_Q = 0
OFF_SB_K = OFF_SB_Q + SB_WIDTH
OFF_SB_V = OFF_SB_K + SB_WIDTH
OFF_SB_Z = OFF_SB_V + SB_WIDTH
OFF_GLA_Q = OFF_SB_Z + SB_WIDTH
OFF_GLA_K = OFF_GLA_Q + GLA_K_WIDTH
OFF_GLA_V = OFF_GLA_K + GLA_K_WIDTH
OFF_GLA_Z = OFF_GLA_V + GLA_V_WIDTH
OFF_LR = OFF_GLA_Z + GLA_V_WIDTH

VMEM_LIMIT_BYTES = 56 * 1024 * 1024

_F32 = jnp.float32
_BF16 = jnp.bfloat16
_NT = (((1,), (1,)), ((), ()))
_TN = (((0,), (0,)), ((), ()))


def _dot(a, b):
    return jnp.dot(a, b, preferred_element_type=_F32)


def _dot_nt(a, b):
    return lax.dot_general(a, b, _NT, preferred_element_type=_F32)


def _log_sigmoid(x):
    return jnp.minimum(x, 0.0) - jnp.log(1.0 + jnp.exp(-jnp.abs(x)))


def _silu(x):
    return x * jax.nn.sigmoid(x)


def _rms(x, gain):
    return x * lax.rsqrt(jnp.mean(x * x, axis=-1, keepdims=True) + EPS) * gain


def _stack_heads(x2):
    first_head = lax.broadcasted_iota(jnp.int32, (1, LANES), 1) < SB_HEAD_DIM
    return jnp.concatenate([jnp.where(first_head, x2, 0), jnp.where(first_head, 0, x2)], axis=0)


def _sb_key_block(start, qs_ref, k_all, vm_all, incl, nc_ref, acc_ref, mask):
    for pair in range(SB_PAIRS):
        cols = slice(pair * LANES, (pair + 1) * LANES)
        zz = _dot_nt(qs_ref[pair], k_all[pl.ds(start, TQ), cols])
        nl = jnp.maximum(jnp.log(1.0 + jnp.exp2(jnp.minimum(zz, EXP2_CLAMP))) * LOG2_E, zz)
        if mask is not None:
            nl = jnp.where(mask, nl, 0.0)
        nc = nc_ref[pair]
        tot = _dot(nl.astype(_BF16), incl) + jnp.concatenate([nc] * (TQ // LANES), axis=1)
        w = jnp.exp2(zz - tot)
        if mask is not None:
            w = jnp.where(mask, w, 0.0)
        w_cat = jnp.concatenate([w[:TQ], w[TQ:]], axis=1).astype(_BF16)
        v_cat = jnp.concatenate([vm_all[0, pl.ds(start, TQ), cols], vm_all[1, pl.ds(start, TQ), cols]], axis=0)
        acc_ref[pair] += _dot(w_cat, v_cat)
        nc_ref[pair] = jnp.broadcast_to(tot[:, 0:1], (2 * TQ, LANES))


def _sb_incl():
    row = lax.broadcasted_iota(jnp.int32, (TQ, TQ), 0)
    col = lax.broadcasted_iota(jnp.int32, (TQ, TQ), 1)
    return (row >= col).astype(_BF16)


def _sb_near_blocks(q, k_all, vm_all, step, qs_ref, nc_ref, acc_ref):
    row2 = lax.broadcasted_iota(jnp.int32, (2 * TQ, TQ), 0) & (TQ - 1)
    col2 = lax.broadcasted_iota(jnp.int32, (2 * TQ, TQ), 1)
    for pair in range(SB_PAIRS):
        qs_ref[pair] = _stack_heads(q[:, pair * LANES:(pair + 1) * LANES])
    nc_ref[...] = jnp.zeros_like(nc_ref)
    acc_ref[...] = jnp.zeros_like(acc_ref)
    incl = _sb_incl()
    _sb_key_block(pl.multiple_of((step + 1) * TQ, TQ), qs_ref, k_all, vm_all, incl, nc_ref, acc_ref, col2 < row2)
    _sb_key_block(pl.multiple_of(step * TQ, TQ), qs_ref, k_all, vm_all, incl, nc_ref, acc_ref, None)


def _sb_far_blocks(k_all, vm_all, step, qs_ref, nc_ref, acc_ref):
    def more_blocks(carry):
        t, live = carry
        return jnp.logical_and(t < step - 1, live)

    def earlier_block(carry):
        t, _ = carry
        _sb_key_block(pl.multiple_of((step - 1 - t) * TQ, TQ), qs_ref, k_all, vm_all, _sb_incl(),
                      nc_ref, acc_ref, None)
        return t + 1, jnp.min(nc_ref[...]) < SB_DEAD_LOG2

    lax.while_loop(more_blocks, earlier_block, (jnp.int32(0), jnp.min(nc_ref[...]) < SB_DEAD_LOG2))


def _chunk_block_diag(x):
    zero = jnp.zeros((CHUNK, LANES), x.dtype)
    out_rows = []
    for r in range(x.shape[0] // CHUNK):
        blocks = [x[r * CHUNK:(r + 1) * CHUNK] if c == r % N_CHUNKS else zero for c in range(N_CHUNKS)]
        out_rows.append(jnp.concatenate(blocks, axis=1))
    return jnp.concatenate(out_rows, axis=0)


def _gla(q, k, v, log_alpha, state_ref, gain):
    row = lax.broadcasted_iota(jnp.int32, (TQ, TQ), 0)
    col = lax.broadcasted_iota(jnp.int32, (TQ, TQ), 1)
    prefix = jnp.logical_and(row // CHUNK == col // CHUNK, col <= row).astype(_BF16)
    la_hi = log_alpha.astype(_BF16)
    la_lo = (log_alpha - la_hi.astype(_F32)).astype(_BF16)
    cum = _dot(prefix, la_hi) + _dot(prefix, la_lo)
    chunk_total = [cum[(c + 1) * CHUNK - 1:(c + 1) * CHUNK] for c in range(N_CHUNKS)]
    cum_last = jnp.concatenate([jnp.broadcast_to(t, (CHUNK, GLA_K_WIDTH)) for t in chunk_total], axis=0)
    q_dec = (q * jnp.exp(cum) * (GLA_KEY_DIM ** -0.5)).astype(_BF16)
    k_inv = (k * jnp.exp(-cum)).astype(_BF16)
    k_end = (k * jnp.exp(cum_last - cum)).astype(_BF16)
    decay = [jnp.exp(t) for t in chunk_total]
    v_bf = v.astype(_BF16)

    row2 = lax.broadcasted_iota(jnp.int32, (2 * TQ, TQ), 0) & (TQ - 1)
    col2 = lax.broadcasted_iota(jnp.int32, (2 * TQ, TQ), 1)
    causal2 = jnp.logical_and(row2 // CHUNK == col2 // CHUNK, col2 <= row2)
    first_head = lax.broadcasted_iota(jnp.int32, (1, LANES), 1) < GLA_KEY_DIM
    outs = []
    for pair in range(GLA_HEADS // 2):
        cols = slice(pair * LANES, (pair + 1) * LANES)
        v_pair = v_bf[:, 2 * pair * GLA_VAL_DIM:(2 * pair + 2) * GLA_VAL_DIM]
        qs = _stack_heads(q_dec[:, cols])
        scores = jnp.where(causal2, _dot_nt(qs, k_inv[:, cols]), 0.0).astype(_BF16)
        intra = (_dot(scores[:TQ], v_pair[:, :GLA_VAL_DIM]), _dot(scores[TQ:], v_pair[:, GLA_VAL_DIM:]))
        kv_all = lax.dot_general(v_pair, _chunk_block_diag(k_end[:, cols]), _TN, preferred_element_type=_F32)
        states = [state_ref[pair]]
        for c in range(N_CHUNKS):
            blk = slice(c * LANES, (c + 1) * LANES)
            kv = jnp.where(first_head, kv_all[:GLA_VAL_DIM, blk], kv_all[GLA_VAL_DIM:, blk])
            states.append(states[-1] * decay[c][:, cols] + kv)
        state_ref[pair] = states[-1]
        state_cat = jnp.concatenate(states[:N_CHUNKS], axis=1).astype(_BF16)
        inter = _dot_nt(_chunk_block_diag(qs), state_cat)
        outs.append(_rms(intra[0] + inter[:TQ], gain))
        outs.append(_rms(intra[1] + inter[TQ:], gain))
    return jnp.concatenate(outs, axis=1)


def _stream_rows(src_hbm, stage_ref, sem_ref, consume):
    chunk_rows = stage_ref.shape[1]
    n_chunks = src_hbm.shape[0] // chunk_rows

    def copy(c):
        return pltpu.make_async_copy(src_hbm.at[pl.ds(c * chunk_rows, chunk_rows), :],
                                     stage_ref.at[c % 2], sem_ref.at[c % 2])

    copy(0).start()
    for c in range(n_chunks):
        if c + 1 < n_chunks:
            copy(c + 1).start()
        copy(c).wait()
        consume(c * chunk_rows, stage_ref[c % 2])


def _load_weights(w_in_hbm, w_o_sb_hbm, w_o_gla_hbm, w_out_hbm, w_in_ref, w_lr_ref, w_gate_ref,
                  w_o_sb_ref, w_o_gla_ref, w_out_ref, stage_in, stage_out, sem_ref):
    w_lr_ref[...] = jnp.zeros_like(w_lr_ref)

    def split_w_in(row0, chunk):
        rows = pl.ds(row0, chunk.shape[0])
        w_in_ref[rows, :] = chunk[:, :OFF_LR].astype(_BF16)
        w_lr_ref[rows, 0:GLA_RANK] = chunk[:, OFF_LR:OFF_LR + GLA_RANK].astype(_BF16)
        w_gate_ref[rows, :] = chunk[:, OFF_LR + GLA_RANK:].astype(_BF16)

    _stream_rows(w_in_hbm.at[0], stage_in, sem_ref, split_w_in)
    for src, dst in ((w_o_sb_hbm, w_o_sb_ref), (w_o_gla_hbm, w_o_gla_ref), (w_out_hbm, w_out_ref)):
        def cast(row0, chunk, dst=dst):
            dst[pl.ds(row0, chunk.shape[0]), :] = chunk.astype(_BF16)

        _stream_rows(src.at[0], stage_out, sem_ref, cast)


def _block_kernel(x_ref, g_pre_ref, w_in_hbm, b_gate_ref, w_up_ref, b_up_ref, g_gla_ref,
                  w_o_sb_hbm, w_o_gla_hbm, w_out_hbm, g_final_ref, out_ref,
                  w_in_ref, w_lr_ref, w_gate_ref, w_o_sb_ref, w_o_gla_ref, w_out_ref, stage_in, stage_out, sem_ref,
                  k_all, vm_all, state_ref, qs_ref, nc_ref, acc_ref):
    step = pl.program_id(1)

    @pl.when(jnp.logical_and(pl.program_id(0) == 0, step == 0))
    def _():
        _load_weights(w_in_hbm, w_o_sb_hbm, w_o_gla_hbm, w_out_hbm, w_in_ref, w_lr_ref, w_gate_ref,
                      w_o_sb_ref, w_o_gla_ref, w_out_ref, stage_in, stage_out, sem_ref)

    @pl.when(step == 0)
    def _():
        state_ref[...] = jnp.zeros_like(state_ref)
        k_all[0:TQ, :] = jnp.zeros((TQ, SB_WIDTH), _BF16)
        vm_all[:, 0:TQ, :] = jnp.zeros((2, TQ, SB_WIDTH), _BF16)

    x = x_ref[0]
    h = _rms(x, g_pre_ref[...]).astype(_BF16)

    def proj(offset, width):
        return _dot(h, w_in_ref[:, offset:offset + width])

    rows = pl.ds(pl.multiple_of((step + 1) * TQ, TQ), TQ)
    k_all[rows, :] = proj(OFF_SB_K, SB_WIDTH).astype(_BF16)
    sb_v = proj(OFF_SB_V, SB_WIDTH).astype(_BF16)
    first_head = (lax.broadcasted_iota(jnp.int32, (1, SB_WIDTH), 1) & (LANES - 1)) < SB_HEAD_DIM
    vm_all[0, rows, :] = jnp.where(first_head, sb_v, 0)
    vm_all[1, rows, :] = jnp.where(first_head, 0, sb_v)
    sb_q = (proj(OFF_SB_Q, SB_WIDTH) * (LOG2_E * SB_HEAD_DIM ** -0.5)).astype(_BF16)
    _sb_near_blocks(sb_q, k_all, vm_all, step, qs_ref, nc_ref, acc_ref)

    lr = _dot(h, w_lr_ref[...]).astype(_BF16)
    alpha_logit = _dot(lr, w_up_ref[...]) + b_up_ref[...]
    log_alpha = _log_sigmoid(alpha_logit) / GLA_GATE_TEMP
    o_gla = _gla(proj(OFF_GLA_Q, GLA_K_WIDTH), proj(OFF_GLA_K, GLA_K_WIDTH), proj(OFF_GLA_V, GLA_V_WIDTH),
                 log_alpha, state_ref, g_gla_ref[...])
    y_gla = _dot((o_gla * _silu(proj(OFF_GLA_Z, GLA_V_WIDTH))).astype(_BF16), w_o_gla_ref[...])
    gates = jax.nn.sigmoid(_dot(h, w_gate_ref[...]) + b_gate_ref[...])
    gated_gla = gates[:, D_MODEL:] * y_gla
    gate_sb = gates[:, :D_MODEL]
    sb_zs = _silu(proj(OFF_SB_Z, SB_WIDTH))

    _sb_far_blocks(k_all, vm_all, step, qs_ref, nc_ref, acc_ref)
    o_sb = jnp.concatenate([acc_ref[pair] for pair in range(SB_PAIRS)], axis=1)
    y_sb = _dot((o_sb * sb_zs).astype(_BF16), w_o_sb_ref[...])
    merged = gate_sb * y_sb + gated_gla
    res = x + _dot(merged.astype(_BF16), w_out_ref[...])
    out_ref[0] = _rms(res, g_final_ref[...])


def _resident(shape):
    return pl.BlockSpec(shape, lambda b, i: (0,) * len(shape), pipeline_mode=pl.Buffered(1))


@jax.jit
def kernel(x, g_pre, w_in, b_gate, w_alpha_up, b_alpha_up, g_gla_norm, w_o_sb, w_o_gla, w_out, g_final):
    batch, seq, d_model = x.shape
    depth = w_in.shape[0]
    assert d_model == D_MODEL and seq % TQ == 0 and depth == 1
    w_up = jnp.pad(w_alpha_up[0], ((0, LR_PAD - GLA_RANK), (0, 0))).astype(_BF16)
    n_in = w_in.shape[2]

    row_block = pl.BlockSpec((1, TQ, D_MODEL), lambda b, i: (b, i, 0))
    in_hbm = pl.BlockSpec(memory_space=pl.ANY)
    return pl.pallas_call(
        _block_kernel,
        grid=(batch, seq // TQ),
        in_specs=[
            row_block,
            _resident((1, D_MODEL)),
            in_hbm,
            _resident((1, 2 * D_MODEL)),
            _resident((LR_PAD, GLA_K_WIDTH)),
            _resident((1, GLA_K_WIDTH)),
            _resident((1, GLA_VAL_DIM)),
            in_hbm,
            in_hbm,
            in_hbm,
            _resident((1, D_MODEL)),
        ],
        out_specs=row_block,
        out_shape=jax.ShapeDtypeStruct(x.shape, x.dtype),
        scratch_shapes=[
            pltpu.VMEM((D_MODEL, OFF_LR), _BF16),
            pltpu.VMEM((D_MODEL, LR_PAD), _BF16),
            pltpu.VMEM((D_MODEL, 2 * D_MODEL), _BF16),
            pltpu.VMEM((SB_WIDTH, D_MODEL), _BF16),
            pltpu.VMEM((GLA_V_WIDTH, D_MODEL), _BF16),
            pltpu.VMEM((D_MODEL, D_MODEL), _BF16),
            pltpu.VMEM((2, WEIGHT_CHUNK_ROWS, n_in), _F32),
            pltpu.VMEM((2, WEIGHT_CHUNK_ROWS, D_MODEL), _F32),
            pltpu.SemaphoreType.DMA((2,)),
            pltpu.VMEM((seq + TQ, SB_WIDTH), _BF16),
            pltpu.VMEM((2, seq + TQ, SB_WIDTH), _BF16),
            pltpu.VMEM((GLA_HEADS // 2, GLA_VAL_DIM, LANES), _F32),
            pltpu.VMEM((SB_PAIRS, 2 * TQ, LANES), _BF16),
            pltpu.VMEM((SB_PAIRS, 2 * TQ, LANES), _F32),
            pltpu.VMEM((SB_PAIRS, TQ, LANES), _F32),
        ],
        compiler_params=pltpu.CompilerParams(
            dimension_semantics=("arbitrary", "arbitrary"), vmem_limit_bytes=VMEM_LIMIT_BYTES),
        name="hybrid_mixer_block",
    )(x, g_pre, w_in, b_gate, w_up, b_alpha_up, g_gla_norm, w_o_sb, w_o_gla, w_out, g_final.reshape(1, D_MODEL))
```

```python
import math

import jax
import jax.numpy as jnp
from jax import lax
from jax.experimental import pallas as pl
from jax.experimental.pallas import tpu as pltpu

D_MODEL = 1024
CHUNK = 64
SB_HEAD_DIM = 64
SB_WIDTH = D_MODEL // 2
GLA_HEADS = 4
GLA_K_WIDTH = D_MODEL // 4
GLA_V_WIDTH = D_MODEL // 2
GLA_KEY_DIM = GLA_K_WIDTH // GLA_HEADS
GLA_VAL_DIM = GLA_V_WIDTH // GLA_HEADS
GLA_RANK = 16
GLA_GATE_TEMP = 16.0
EPS = 1e-6
LOG2_E = math.log2(math.e)
EXP2_CLAMP = 126.0
SB_DEAD_LOG2 = 151.0

LANES = 128
SB_PAIRS = SB_WIDTH // LANES
TQ = 256
N_CHUNKS = TQ // CHUNK
LR_PAD = LANES
WEIGHT_CHUNK_ROWS = 128

OFF_SB_Q = 0
OFF_SB_K = OFF_SB_Q + SB_WIDTH
OFF_SB_V = OFF_SB_K + SB_WIDTH
OFF_SB_Z = OFF_SB_V + SB_WIDTH
OFF_GLA_Q = OFF_SB_Z + SB_WIDTH
OFF_GLA_K = OFF_GLA_Q + GLA_K_WIDTH
OFF_GLA_V = OFF_GLA_K + GLA_K_WIDTH
OFF_GLA_Z = OFF_GLA_V + GLA_V_WIDTH
OFF_LR = OFF_GLA_Z + GLA_V_WIDTH

VMEM_LIMIT_BYTES = 56 * 1024 * 1024

_F32 = jnp.float32
_BF16 = jnp.bfloat16
_NT = (((1,), (1,)), ((), ()))
_TN = (((0,), (0,)), ((), ()))


def _dot(a, b):
    return jnp.dot(a, b, preferred_element_type=_F32)


def _dot_nt(a, b):
    return lax.dot_general(a, b, _NT, preferred_element_type=_F32)


def _log_sigmoid(x):
    return jnp.minimum(x, 0.0) - jnp.log(1.0 + jnp.exp(-jnp.abs(x)))


def _silu(x):
    return x * jax.nn.sigmoid(x)


def _rms(x, gain):
    return x * lax.rsqrt(jnp.mean(x * x, axis=-1, keepdims=True) + EPS) * gain


def _stack_heads(x2):
    first_head = lax.broadcasted_iota(jnp.int32, (1, LANES), 1) < SB_HEAD_DIM
    return jnp.concatenate([jnp.where(first_head, x2, 0), jnp.where(first_head, 0, x2)], axis=0)


def _sb_key_block(start, qs_ref, k_all, vm_all, incl, nc_ref, acc_ref, mask):
    for pair in range(SB_PAIRS):
        cols = slice(pair * LANES, (pair + 1) * LANES)
        zz = _dot_nt(qs_ref[pair], k_all[pl.ds(start, TQ), cols])
        nl = jnp.maximum(jnp.log(1.0 + jnp.exp2(jnp.minimum(zz, EXP2_CLAMP))) * LOG2_E, zz)
        if mask is not None:
            nl = jnp.where(mask, nl, 0.0)
        nc = nc_ref[pair]
        tot = _dot(nl.astype(_BF16), incl) + jnp.concatenate([nc] * (TQ // LANES), axis=1)
        w = jnp.exp2(zz - tot)
        if mask is not None:
            w = jnp.where(mask, w, 0.0)
        w_cat = jnp.concatenate([w[:TQ], w[TQ:]], axis=1).astype(_BF16)
        v_cat = jnp.concatenate([vm_all[0, pl.ds(start, TQ), cols], vm_all[1, pl.ds(start, TQ), cols]], axis=0)
        acc_ref[pair] += _dot(w_cat, v_cat)
        nc_ref[pair] = jnp.broadcast_to(tot[:, 0:1], (2 * TQ, LANES))


def _sb_incl():
    row = lax.broadcasted_iota(jnp.int32, (TQ, TQ), 0)
    col = lax.broadcasted_iota(jnp.int32, (TQ, TQ), 1)
    return (row >= col).astype(_BF16)


def _sb_near_blocks(q, k_all, vm_all, step, qs_ref, nc_ref, acc_ref):
    row2 = lax.broadcasted_iota(jnp.int32, (2 * TQ, TQ), 0) & (TQ - 1)
    col2 = lax.broadcasted_iota(jnp.int32, (2 * TQ, TQ), 1)
    for pair in range(SB_PAIRS):
        qs_ref[pair] = _stack_heads(q[:, pair * LANES:(pair + 1) * LANES])
    nc_ref[...] = jnp.zeros_like(nc_ref)
    acc_ref[...] = jnp.zeros_like(acc_ref)
    incl = _sb_incl()
    _sb_key_block(pl.multiple_of((step + 1) * TQ, TQ), qs_ref, k_all, vm_all, incl, nc_ref, acc_ref, col2 < row2)
    _sb_key_block(pl.multiple_of(step * TQ, TQ), qs_ref, k_all, vm_all, incl, nc_ref, acc_ref, None)


def _sb_far_blocks(k_all, vm_all, step, qs_ref, nc_ref, acc_ref):
    def more_blocks(carry):
        t, live = carry
        return jnp.logical_and(t < step - 1, live)

    def earlier_block(carry):
        t, _ = carry
        _sb_key_block(pl.multiple_of((step - 1 - t) * TQ, TQ), qs_ref, k_all, vm_all, _sb_incl(),
                      nc_ref, acc_ref, None)
        return t + 1, jnp.min(nc_ref[...]) < SB_DEAD_LOG2

    lax.while_loop(more_blocks, earlier_block, (jnp.int32(0), jnp.min(nc_ref[...]) < SB_DEAD_LOG2))


def _chunk_block_diag(x):
    zero = jnp.zeros((CHUNK, LANES), x.dtype)
    out_rows = []
    for r in range(x.shape[0] // CHUNK):
        blocks = [x[r * CHUNK:(r + 1) * CHUNK] if c == r % N_CHUNKS else zero for c in range(N_CHUNKS)]
        out_rows.append(jnp.concatenate(blocks, axis=1))
    return jnp.concatenate(out_rows, axis=0)


def _gla(q, k, v, log_alpha, state_ref, gain):
    row = lax.broadcasted_iota(jnp.int32, (TQ, TQ), 0)
    col = lax.broadcasted_iota(jnp.int32, (TQ, TQ), 1)
    prefix = jnp.logical_and(row // CHUNK == col // CHUNK, col <= row).astype(_BF16)
    la_hi = log_alpha.astype(_BF16)
    la_lo = (log_alpha - la_hi.astype(_F32)).astype(_BF16)
    cum = _dot(prefix, la_hi) + _dot(prefix, la_lo)
    chunk_total = [cum[(c + 1) * CHUNK - 1:(c + 1) * CHUNK] for c in range(N_CHUNKS)]
    cum_last = jnp.concatenate([jnp.broadcast_to(t, (CHUNK, GLA_K_WIDTH)) for t in chunk_total], axis=0)
    q_dec = (q * jnp.exp(cum) * (GLA_KEY_DIM ** -0.5)).astype(_BF16)
    k_inv = (k * jnp.exp(-cum)).astype(_BF16)
    k_end = (k * jnp.exp(cum_last - cum)).astype(_BF16)
    decay = [jnp.exp(t) for t in chunk_total]
    v_bf = v.astype(_BF16)

    row2 = lax.broadcasted_iota(jnp.int32, (2 * TQ, TQ), 0) & (TQ - 1)
    col2 = lax.broadcasted_iota(jnp.int32, (2 * TQ, TQ), 1)
    causal2 = jnp.logical_and(row2 // CHUNK == col2 // CHUNK, col2 <= row2)
    first_head = lax.broadcasted_iota(jnp.int32, (1, LANES), 1) < GLA_KEY_DIM
    outs = []
    for pair in range(GLA_HEADS // 2):
        cols = slice(pair * LANES, (pair + 1) * LANES)
        v_pair = v_bf[:, 2 * pair * GLA_VAL_DIM:(2 * pair + 2) * GLA_VAL_DIM]
        qs = _stack_heads(q_dec[:, cols])
        scores = jnp.where(causal2, _dot_nt(qs, k_inv[:, cols]), 0.0).astype(_BF16)
        intra = (_dot(scores[:TQ], v_pair[:, :GLA_VAL_DIM]), _dot(scores[TQ:], v_pair[:, GLA_VAL_DIM:]))
        kv_all = lax.dot_general(v_pair, _chunk_block_diag(k_end[:, cols]), _TN, preferred_element_type=_F32)
        states = [state_ref[pair]]
        for c in range(N_CHUNKS):
            blk = slice(c * LANES, (c + 1) * LANES)
            kv = jnp.where(first_head, kv_all[:GLA_VAL_DIM, blk], kv_all[GLA_VAL_DIM:, blk])
            states.append(states[-1] * decay[c][:, cols] + kv)
        state_ref[pair] = states[-1]
        state_cat = jnp.concatenate(states[:N_CHUNKS], axis=1).astype(_BF16)
        inter = _dot_nt(_chunk_block_diag(qs), state_cat)
        outs.append(_rms(intra[0] + inter[:TQ], gain))
        outs.append(_rms(intra[1] + inter[TQ:], gain))
    return jnp.concatenate(outs, axis=1)


def _load_weights(w_in_t_hbm, w_o_sb_hbm, w_o_gla_hbm, w_out_hbm, w_in_ref, w_lr_ref, w_gate_ref,
                  w_o_sb_ref, w_o_gla_ref, w_out_ref, stage_ref, sem_ref):
    rows = WEIGHT_CHUNK_ROWS
    jobs = []

    def into_columns(dst, block):
        def consume(chunk):
            dst[:, block * rows:(block + 1) * rows] = chunk.T.astype(_BF16)
        return consume

    def into_rows(dst, block):
        def consume(chunk):
            dst[block * rows:(block + 1) * rows, :] = chunk.astype(_BF16)
        return consume

    def low_rank(chunk):
        lane = lax.broadcasted_iota(jnp.int32, (1, rows), 1)
        w_lr_ref[...] = jnp.where(lane < GLA_RANK, chunk.T, 0.0).astype(_BF16)

    w_in_t = w_in_t_hbm.at[0]
    for block in range(OFF_LR // rows):
        jobs.append((w_in_t, block * rows, into_columns(w_in_ref, block)))
    jobs.append((w_in_t, OFF_LR, low_rank))
    for block in range(2 * D_MODEL // rows):
        jobs.append((w_in_t, OFF_LR + GLA_RANK + block * rows, into_columns(w_gate_ref, block)))
    for src, dst in ((w_o_sb_hbm, w_o_sb_ref), (w_o_gla_hbm, w_o_gla_ref), (w_out_hbm, w_out_ref)):
        for block in range(dst.shape[0] // rows):
            jobs.append((src.at[0], block * rows, into_rows(dst, block)))

    def copy(j):
        src, row0, _ = jobs[j]
        return pltpu.make_async_copy(src.at[pl.ds(row0, rows), :], stage_ref.at[j % 2], sem_ref.at[j % 2])

    copy(0).start()
    for j in range(len(jobs)):
        if j + 1 < len(jobs):
            copy(j + 1).start()
        copy(j).wait()
        jobs[j][2](stage_ref[j % 2])


def _block_kernel(x_ref, g_pre_ref, w_in_t_hbm, b_gate_ref, w_up_ref, b_up_ref, g_gla_ref,
                  w_o_sb_hbm, w_o_gla_hbm, w_out_hbm, g_final_ref, out_ref,
                  w_in_ref, w_lr_ref, w_gate_ref, w_o_sb_ref, w_o_gla_ref, w_out_ref, stage_ref, sem_ref,
                  k_all, vm_all, state_ref, qs_ref, nc_ref, acc_ref):
    step = pl.program_id(1)

    @pl.when(jnp.logical_and(pl.program_id(0) == 0, step == 0))
    def _():
        _load_weights(w_in_t_hbm, w_o_sb_hbm, w_o_gla_hbm, w_out_hbm, w_in_ref, w_lr_ref, w_gate_ref,
                      w_o_sb_ref, w_o_gla_ref, w_out_ref, stage_ref, sem_ref)

    @pl.when(step == 0)
    def _():
        state_ref[...] = jnp.zeros_like(state_ref)
        k_all[0:TQ, :] = jnp.zeros((TQ, SB_WIDTH), _BF16)
        vm_all[:, 0:TQ, :] = jnp.zeros((2, TQ, SB_WIDTH), _BF16)

    x = x_ref[0]
    h = _rms(x, g_pre_ref[...]).astype(_BF16)

    def proj(offset, width):
        return _dot(h, w_in_ref[:, offset:offset + width])

    rows = pl.ds(pl.multiple_of((step + 1) * TQ, TQ), TQ)
    k_all[rows, :] = proj(OFF_SB_K, SB_WIDTH).astype(_BF16)
    sb_v = proj(OFF_SB_V, SB_WIDTH).astype(_BF16)
    first_head = (lax.broadcasted_iota(jnp.int32, (1, SB_WIDTH), 1) & (LANES - 1)) < SB_HEAD_DIM
    vm_all[0, rows, :] = jnp.where(first_head, sb_v, 0)
    vm_all[1, rows, :] = jnp.where(first_head, 0, sb_v)
    sb_q = (proj(OFF_SB_Q, SB_WIDTH) * (LOG2_E * SB_HEAD_DIM ** -0.5)).astype(_BF16)
    _sb_near_blocks(sb_q, k_all, vm_all, step, qs_ref, nc_ref, acc_ref)

    lr = _dot(h, w_lr_ref[...]).astype(_BF16)
    alpha_logit = _dot(lr, w_up_ref[...]) + b_up_ref[...]
    log_alpha = _log_sigmoid(alpha_logit) / GLA_GATE_TEMP
    o_gla = _gla(proj(OFF_GLA_Q, GLA_K_WIDTH), proj(OFF_GLA_K, GLA_K_WIDTH), proj(OFF_GLA_V, GLA_V_WIDTH),
                 log_alpha, state_ref, g_gla_ref[...])
    y_gla = _dot((o_gla * _silu(proj(OFF_GLA_Z, GLA_V_WIDTH))).astype(_BF16), w_o_gla_ref[...])
    gates = jax.nn.sigmoid(_dot(h, w_gate_ref[...]) + b_gate_ref[...])
    gated_gla = gates[:, D_MODEL:] * y_gla
    gate_sb = gates[:, :D_MODEL]
    sb_zs = _silu(proj(OFF_SB_Z, SB_WIDTH))

    _sb_far_blocks(k_all, vm_all, step, qs_ref, nc_ref, acc_ref)
    o_sb = jnp.concatenate([acc_ref[pair] for pair in range(SB_PAIRS)], axis=1)
    y_sb = _dot((o_sb * sb_zs).astype(_BF16), w_o_sb_ref[...])
    merged = gate_sb * y_sb + gated_gla
    res = x + _dot(merged.astype(_BF16), w_out_ref[...])
    out_ref[0] = _rms(res, g_final_ref[...])


def _resident(shape):
    return pl.BlockSpec(shape, lambda b, i: (0,) * len(shape), pipeline_mode=pl.Buffered(1))


@jax.jit
def kernel(x, g_pre, w_in, b_gate, w_alpha_up, b_alpha_up, g_gla_norm, w_o_sb, w_o_gla, w_out, g_final):
    batch, seq, d_model = x.shape
    depth = w_in.shape[0]
    assert d_model == D_MODEL and seq % TQ == 0 and depth == 1
    w_up = jnp.pad(w_alpha_up[0], ((0, LR_PAD - GLA_RANK), (0, 0))).astype(_BF16)
    w_in_t = jnp.transpose(w_in, (0, 2, 1))

    row_block = pl.BlockSpec((1, TQ, D_MODEL), lambda b, i: (b, i, 0))
    in_hbm = pl.BlockSpec(memory_space=pl.ANY)
    return pl.pallas_call(
        _block_kernel,
        grid=(batch, seq // TQ),
        in_specs=[
            row_block,
            _resident((1, D_MODEL)),
            in_hbm,
            _resident((1, 2 * D_MODEL)),
            _resident((LR_PAD, GLA_K_WIDTH)),
            _resident((1, GLA_K_WIDTH)),
            _resident((1, GLA_VAL_DIM)),
            in_hbm,
            in_hbm,
            in_hbm,
            _resident((1, D_MODEL)),
        ],
        out_specs=row_block,
        out_shape=jax.ShapeDtypeStruct(x.shape, x.dtype),
        scratch_shapes=[
            pltpu.VMEM((D_MODEL, OFF_LR), _BF16),
            pltpu.VMEM((D_MODEL, LR_PAD), _BF16),
            pltpu.VMEM((D_MODEL, 2 * D_MODEL), _BF16),
            pltpu.VMEM((SB_WIDTH, D_MODEL), _BF16),
            pltpu.VMEM((GLA_V_WIDTH, D_MODEL), _BF16),
            pltpu.VMEM((D_MODEL, D_MODEL), _BF16),
            pltpu.VMEM((2, WEIGHT_CHUNK_ROWS, D_MODEL), _F32),
            pltpu.SemaphoreType.DMA((2,)),
            pltpu.VMEM((seq + TQ, SB_WIDTH), _BF16),
            pltpu.VMEM((2, seq + TQ, SB_WIDTH), _BF16),
            pltpu.VMEM((GLA_HEADS // 2, GLA_VAL_DIM, LANES), _F32),
            pltpu.VMEM((SB_PAIRS, 2 * TQ, LANES), _BF16),
            pltpu.VMEM((SB_PAIRS, 2 * TQ, LANES), _F32),
            pltpu.VMEM((SB_PAIRS, TQ, LANES), _F32),
        ],
        compiler_params=pltpu.CompilerParams(
            dimension_semantics=("arbitrary", "arbitrary"), vmem_limit_bytes=VMEM_LIMIT_BYTES),
        name="hybrid_mixer_block",
    )(x, g_pre, w_in_t, b_gate, w_up, b_alpha_up, g_gla_norm, w_o_sb, w_o_gla, w_out, g_final.reshape(1, D_MODEL))
```

```python
import math

import jax
import jax.numpy as jnp
from jax import lax
from jax.experimental import pallas as pl
from jax.experimental.pallas import tpu as pltpu

D_MODEL = 1024
CHUNK = 64
SB_HEAD_DIM = 64
SB_WIDTH = D_MODEL // 2
GLA_HEADS = 4
GLA_K_WIDTH = D_MODEL // 4
GLA_V_WIDTH = D_MODEL // 2
GLA_KEY_DIM = GLA_K_WIDTH // GLA_HEADS
GLA_VAL_DIM = GLA_V_WIDTH // GLA_HEADS
GLA_RANK = 16
GLA_GATE_TEMP = 16.0
EPS = 1e-6
LOG2_E = math.log2(math.e)
EXP2_CLAMP = 126.0
SB_DEAD_LOG2 = 151.0

LANES = 128
SB_PAIRS = SB_WIDTH // LANES
TQ = 256
N_CHUNKS = TQ // CHUNK
LR_PAD = LANES
WEIGHT_CHUNK_ROWS = 512
WEIGHT_BUFFERS = 3

OFF_SB_Q = 0
OFF_SB_K = OFF_SB_Q + SB_WIDTH
OFF_SB_V = OFF_SB_K + SB_WIDTH
OFF_SB_Z = OFF_SB_V + SB_WIDTH
OFF_GLA_Q = OFF_SB_Z + SB_WIDTH
OFF_GLA_K = OFF_GLA_Q + GLA_K_WIDTH
OFF_GLA_V = OFF_GLA_K + GLA_K_WIDTH
OFF_GLA_Z = OFF_GLA_V + GLA_V_WIDTH
OFF_LR = OFF_GLA_Z + GLA_V_WIDTH

VMEM_LIMIT_BYTES = 56 * 1024 * 1024

_F32 = jnp.float32
_BF16 = jnp.bfloat16
_NT = (((1,), (1,)), ((), ()))
_TN = (((0,), (0,)), ((), ()))


def _dot(a, b):
    return jnp.dot(a, b, preferred_element_type=_F32)


def _dot_nt(a, b):
    return lax.dot_general(a, b, _NT, preferred_element_type=_F32)


def _log_sigmoid(x):
    return jnp.minimum(x, 0.0) - jnp.log(1.0 + jnp.exp(-jnp.abs(x)))


def _silu(x):
    return x * jax.nn.sigmoid(x)


def _rms(x, gain):
    return x * lax.rsqrt(jnp.mean(x * x, axis=-1, keepdims=True) + EPS) * gain


def _stack_heads(x2):
    first_head = lax.broadcasted_iota(jnp.int32, (1, LANES), 1) < SB_HEAD_DIM
    return jnp.concatenate([jnp.where(first_head, x2, 0), jnp.where(first_head, 0, x2)], axis=0)


def _sb_key_block(start, qs_ref, k_all, vm_all, incl, nc_ref, acc_ref, mask):
    for pair in range(SB_PAIRS):
        cols = slice(pair * LANES, (pair + 1) * LANES)
        zz = _dot_nt(qs_ref[pair], k_all[pl.ds(start, TQ), cols])
        nl = jnp.maximum(jnp.log(1.0 + jnp.exp2(jnp.minimum(zz, EXP2_CLAMP))) * LOG2_E, zz)
        if mask is not None:
            nl = jnp.where(mask, nl, 0.0)
        nc = nc_ref[pair]
        tot = _dot(nl.astype(_BF16), incl) + jnp.concatenate([nc] * (TQ // LANES), axis=1)
        w = jnp.exp2(zz - tot)
        if mask is not None:
            w = jnp.where(mask, w, 0.0)
        w_cat = jnp.concatenate([w[:TQ], w[TQ:]], axis=1).astype(_BF16)
        v_cat = jnp.concatenate([vm_all[0, pl.ds(start, TQ), cols], vm_all[1, pl.ds(start, TQ), cols]], axis=0)
        acc_ref[pair] += _dot(w_cat, v_cat)
        nc_ref[pair] = jnp.broadcast_to(tot[:, 0:1], (2 * TQ, LANES))


def _sb_incl():
    row = lax.broadcasted_iota(jnp.int32, (TQ, TQ), 0)
    col = lax.broadcasted_iota(jnp.int32, (TQ, TQ), 1)
    return (row >= col).astype(_BF16)


def _sb_near_blocks(q, k_all, vm_all, step, qs_ref, nc_ref, acc_ref):
    row2 = lax.broadcasted_iota(jnp.int32, (2 * TQ, TQ), 0) & (TQ - 1)
    col2 = lax.broadcasted_iota(jnp.int32, (2 * TQ, TQ), 1)
    for pair in range(SB_PAIRS):
        qs_ref[pair] = _stack_heads(q[:, pair * LANES:(pair + 1) * LANES])
    nc_ref[...] = jnp.zeros_like(nc_ref)
    acc_ref[...] = jnp.zeros_like(acc_ref)
    incl = _sb_incl()
    _sb_key_block(pl.multiple_of((step + 1) * TQ, TQ), qs_ref, k_all, vm_all, incl, nc_ref, acc_ref, col2 < row2)
    _sb_key_block(pl.multiple_of(step * TQ, TQ), qs_ref, k_all, vm_all, incl, nc_ref, acc_ref, None)


def _sb_far_blocks(k_all, vm_all, step, qs_ref, nc_ref, acc_ref):
    def more_blocks(carry):
        t, live = carry
        return jnp.logical_and(t < step - 1, live)

    def earlier_block(carry):
        t, _ = carry
        _sb_key_block(pl.multiple_of((step - 1 - t) * TQ, TQ), qs_ref, k_all, vm_all, _sb_incl(),
                      nc_ref, acc_ref, None)
        return t + 1, jnp.min(nc_ref[...]) < SB_DEAD_LOG2

    lax.while_loop(more_blocks, earlier_block, (jnp.int32(0), jnp.min(nc_ref[...]) < SB_DEAD_LOG2))


def _chunk_block_diag(x):
    zero = jnp.zeros((CHUNK, LANES), x.dtype)
    out_rows = []
    for r in range(x.shape[0] // CHUNK):
        blocks = [x[r * CHUNK:(r + 1) * CHUNK] if c == r % N_CHUNKS else zero for c in range(N_CHUNKS)]
        out_rows.append(jnp.concatenate(blocks, axis=1))
    return jnp.concatenate(out_rows, axis=0)


def _gla(q, k, v, log_alpha, state_ref, gain):
    row = lax.broadcasted_iota(jnp.int32, (TQ, TQ), 0)
    col = lax.broadcasted_iota(jnp.int32, (TQ, TQ), 1)
    prefix = jnp.logical_and(row // CHUNK == col // CHUNK, col <= row).astype(_BF16)
    la_hi = log_alpha.astype(_BF16)
    la_lo = (log_alpha - la_hi.astype(_F32)).astype(_BF16)
    cum = _dot(prefix, la_hi) + _dot(prefix, la_lo)
    chunk_total = [cum[(c + 1) * CHUNK - 1:(c + 1) * CHUNK] for c in range(N_CHUNKS)]
    cum_last = jnp.concatenate([jnp.broadcast_to(t, (CHUNK, GLA_K_WIDTH)) for t in chunk_total], axis=0)
    q_dec = (q * jnp.exp(cum) * (GLA_KEY_DIM ** -0.5)).astype(_BF16)
    k_inv = (k * jnp.exp(-cum)).astype(_BF16)
    k_end = (k * jnp.exp(cum_last - cum)).astype(_BF16)
    decay = [jnp.exp(t) for t in chunk_total]
    v_bf = v.astype(_BF16)

    row2 = lax.broadcasted_iota(jnp.int32, (2 * TQ, TQ), 0) & (TQ - 1)
    col2 = lax.broadcasted_iota(jnp.int32, (2 * TQ, TQ), 1)
    causal2 = jnp.logical_and(row2 // CHUNK == col2 // CHUNK, col2 <= row2)
    first_head = lax.broadcasted_iota(jnp.int32, (1, LANES), 1) < GLA_KEY_DIM
    outs = []
    for pair in range(GLA_HEADS // 2):
        cols = slice(pair * LANES, (pair + 1) * LANES)
        v_pair = v_bf[:, 2 * pair * GLA_VAL_DIM:(2 * pair + 2) * GLA_VAL_DIM]
        qs = _stack_heads(q_dec[:, cols])
        scores = jnp.where(causal2, _dot_nt(qs, k_inv[:, cols]), 0.0).astype(_BF16)
        intra = (_dot(scores[:TQ], v_pair[:, :GLA_VAL_DIM]), _dot(scores[TQ:], v_pair[:, GLA_VAL_DIM:]))
        kv_all = lax.dot_general(v_pair, _chunk_block_diag(k_end[:, cols]), _TN, preferred_element_type=_F32)
        states = [state_ref[pair]]
        for c in range(N_CHUNKS):
            blk = slice(c * LANES, (c + 1) * LANES)
            kv = jnp.where(first_head, kv_all[:GLA_VAL_DIM, blk], kv_all[GLA_VAL_DIM:, blk])
            states.append(states[-1] * decay[c][:, cols] + kv)
        state_ref[pair] = states[-1]
        state_cat = jnp.concatenate(states[:N_CHUNKS], axis=1).astype(_BF16)
        inter = _dot_nt(_chunk_block_diag(qs), state_cat)
        outs.append(_rms(intra[0] + inter[:TQ], gain))
        outs.append(_rms(intra[1] + inter[TQ:], gain))
    return jnp.concatenate(outs, axis=1)


def _load_weights(w_in_t_hbm, w_o_sb_hbm, w_o_gla_hbm, w_out_hbm, w_in_ref, w_lr_ref, w_gate_ref,
                  w_o_sb_ref, w_o_gla_ref, w_out_ref, stage_ref, sem_ref):
    rows = WEIGHT_CHUNK_ROWS
    jobs = []

    def into_columns(dst, block):
        def consume(chunk):
            dst[:, block * rows:(block + 1) * rows] = chunk.T.astype(_BF16)
        return consume

    def into_rows(dst, block):
        def consume(chunk):
            dst[block * rows:(block + 1) * rows, :] = chunk.astype(_BF16)
        return consume

    def low_rank(chunk):
        lane = lax.broadcasted_iota(jnp.int32, (1, LR_PAD), 1)
        w_lr_ref[...] = jnp.where(lane < GLA_RANK, chunk[:LR_PAD].T, 0.0).astype(_BF16)

    w_in_t = w_in_t_hbm.at[0]
    for block in range(OFF_LR // rows):
        jobs.append((w_in_t, block * rows, into_columns(w_in_ref, block)))
    jobs.append((w_in_t, OFF_LR, low_rank))
    for block in range(2 * D_MODEL // rows):
        jobs.append((w_in_t, OFF_LR + GLA_RANK + block * rows, into_columns(w_gate_ref, block)))
    for src, dst in ((w_o_sb_hbm, w_o_sb_ref), (w_o_gla_hbm, w_o_gla_ref), (w_out_hbm, w_out_ref)):
        for block in range(dst.shape[0] // rows):
            jobs.append((src.at[0], block * rows, into_rows(dst, block)))

    n_buf = stage_ref.shape[0]

    def copy(j):
        src, row0, _ = jobs[j]
        return pltpu.make_async_copy(src.at[pl.ds(row0, rows), :], stage_ref.at[j % n_buf], sem_ref.at[j % n_buf])

    for j in range(min(n_buf - 1, len(jobs))):
        copy(j).start()
    for j in range(len(jobs)):
        if j + n_buf - 1 < len(jobs):
            copy(j + n_buf - 1).start()
        copy(j).wait()
        jobs[j][2](stage_ref[j % n_buf])


def _block_kernel(x_ref, g_pre_ref, w_in_t_hbm, b_gate_ref, w_up_ref, b_up_ref, g_gla_ref,
                  w_o_sb_hbm, w_o_gla_hbm, w_out_hbm, g_final_ref, out_ref,
                  w_in_ref, w_lr_ref, w_gate_ref, w_o_sb_ref, w_o_gla_ref, w_out_ref, stage_ref, sem_ref,
                  k_all, vm_all, state_ref, qs_ref, nc_ref, acc_ref):
    step = pl.program_id(1)

    @pl.when(jnp.logical_and(pl.program_id(0) == 0, step == 0))
    def _():
        _load_weights(w_in_t_hbm, w_o_sb_hbm, w_o_gla_hbm, w_out_hbm, w_in_ref, w_lr_ref, w_gate_ref,
                      w_o_sb_ref, w_o_gla_ref, w_out_ref, stage_ref, sem_ref)

    @pl.when(step == 0)
    def _():
        state_ref[...] = jnp.zeros_like(state_ref)
        k_all[0:TQ, :] = jnp.zeros((TQ, SB_WIDTH), _BF16)
        vm_all[:, 0:TQ, :] = jnp.zeros((2, TQ, SB_WIDTH), _BF16)

    x = x_ref[0]
    h = _rms(x, g_pre_ref[...]).astype(_BF16)

    def proj(offset, width):
        return _dot(h, w_in_ref[:, offset:offset + width])

    rows = pl.ds(pl.multiple_of((step + 1) * TQ, TQ), TQ)
    k_all[rows, :] = proj(OFF_SB_K, SB_WIDTH).astype(_BF16)
    sb_v = proj(OFF_SB_V, SB_WIDTH).astype(_BF16)
    first_head = (lax.broadcasted_iota(jnp.int32, (1, SB_WIDTH), 1) & (LANES - 1)) < SB_HEAD_DIM
    vm_all[0, rows, :] = jnp.where(first_head, sb_v, 0)
    vm_all[1, rows, :] = jnp.where(first_head, 0, sb_v)
    sb_q = (proj(OFF_SB_Q, SB_WIDTH) * (LOG2_E * SB_HEAD_DIM ** -0.5)).astype(_BF16)
    _sb_near_blocks(sb_q, k_all, vm_all, step, qs_ref, nc_ref, acc_ref)

    lr = _dot(h, w_lr_ref[...]).astype(_BF16)
    alpha_logit = _dot(lr, w_up_ref[...]) + b_up_ref[...]
    log_alpha = _log_sigmoid(alpha_logit) / GLA_GATE_TEMP
    o_gla = _gla(proj(OFF_GLA_Q, GLA_K_WIDTH), proj(OFF_GLA_K, GLA_K_WIDTH), proj(OFF_GLA_V, GLA_V_WIDTH),
                 log_alpha, state_ref, g_gla_ref[...])
    y_gla = _dot((o_gla * _silu(proj(OFF_GLA_Z, GLA_V_WIDTH))).astype(_BF16), w_o_gla_ref[...])
    gates = jax.nn.sigmoid(_dot(h, w_gate_ref[...]) + b_gate_ref[...])
    gated_gla = gates[:, D_MODEL:] * y_gla
    gate_sb = gates[:, :D_MODEL]
    sb_zs = _silu(proj(OFF_SB_Z, SB_WIDTH))

    _sb_far_blocks(k_all, vm_all, step, qs_ref, nc_ref, acc_ref)
    o_sb = jnp.concatenate([acc_ref[pair] for pair in range(SB_PAIRS)], axis=1)
    y_sb = _dot((o_sb * sb_zs).astype(_BF16), w_o_sb_ref[...])
    merged = gate_sb * y_sb + gated_gla
    res = x + _dot(merged.astype(_BF16), w_out_ref[...])
    out_ref[0] = _rms(res, g_final_ref[...])


def _resident(shape):
    return pl.BlockSpec(shape, lambda b, i: (0,) * len(shape), pipeline_mode=pl.Buffered(1))


@jax.jit
def kernel(x, g_pre, w_in, b_gate, w_alpha_up, b_alpha_up, g_gla_norm, w_o_sb, w_o_gla, w_out, g_final):
    batch, seq, d_model = x.shape
    depth = w_in.shape[0]
    assert d_model == D_MODEL and seq % TQ == 0 and depth == 1
    w_up = jnp.pad(w_alpha_up[0], ((0, LR_PAD - GLA_RANK), (0, 0))).astype(_BF16)
    w_in_t = jnp.transpose(w_in, (0, 2, 1))

    row_block = pl.BlockSpec((1, TQ, D_MODEL), lambda b, i: (b, i, 0))
    in_hbm = pl.BlockSpec(memory_space=pl.ANY)
    return pl.pallas_call(
        _block_kernel,
        grid=(batch, seq // TQ),
        in_specs=[
            row_block,
            _resident((1, D_MODEL)),
            in_hbm,
            _resident((1, 2 * D_MODEL)),
            _resident((LR_PAD, GLA_K_WIDTH)),
            _resident((1, GLA_K_WIDTH)),
            _resident((1, GLA_VAL_DIM)),
            in_hbm,
            in_hbm,
            in_hbm,
            _resident((1, D_MODEL)),
        ],
        out_specs=row_block,
        out_shape=jax.ShapeDtypeStruct(x.shape, x.dtype),
        scratch_shapes=[
            pltpu.VMEM((D_MODEL, OFF_LR), _BF16),
            pltpu.VMEM((D_MODEL, LR_PAD), _BF16),
            pltpu.VMEM((D_MODEL, 2 * D_MODEL), _BF16),
            pltpu.VMEM((SB_WIDTH, D_MODEL), _BF16),
            pltpu.VMEM((GLA_V_WIDTH, D_MODEL), _BF16),
            pltpu.VMEM((D_MODEL, D_MODEL), _BF16),
            pltpu.VMEM((WEIGHT_BUFFERS, WEIGHT_CHUNK_ROWS, D_MODEL), _F32),
            pltpu.SemaphoreType.DMA((WEIGHT_BUFFERS,)),
            pltpu.VMEM((seq + TQ, SB_WIDTH), _BF16),
            pltpu.VMEM((2, seq + TQ, SB_WIDTH), _BF16),
            pltpu.VMEM((GLA_HEADS // 2, GLA_VAL_DIM, LANES), _F32),
            pltpu.VMEM((SB_PAIRS, 2 * TQ, LANES), _BF16),
            pltpu.VMEM((SB_PAIRS, 2 * TQ, LANES), _F32),
            pltpu.VMEM((SB_PAIRS, TQ, LANES), _F32),
        ],
        compiler_params=pltpu.CompilerParams(
            dimension_semantics=("arbitrary", "arbitrary"), vmem_limit_bytes=VMEM_LIMIT_BYTES),
        name="hybrid_mixer_block",
    )(x, g_pre, w_in_t, b_gate, w_up, b_alpha_up, g_gla_norm, w_o_sb, w_o_gla, w_out, g_final.reshape(1, D_MODEL))
```

```python
import math

import jax
import jax.numpy as jnp
from jax import lax
from jax.experimental import pallas as pl
from jax.experimental.pallas import tpu as pltpu

D_MODEL = 1024
CHUNK = 64
SB_HEAD_DIM = 64
SB_WIDTH = D_MODEL // 2
GLA_HEADS = 4
GLA_K_WIDTH = D_MODEL // 4
GLA_V_WIDTH = D_MODEL // 2
GLA_KEY_DIM = GLA_K_WIDTH // GLA_HEADS
GLA_VAL_DIM = GLA_V_WIDTH // GLA_HEADS
GLA_RANK = 16
GLA_GATE_TEMP = 16.0
EPS = 1e-6
LOG2_E = math.log2(math.e)
EXP2_CLAMP = 126.0
SB_DEAD_LOG2 = 151.0

LANES = 128
SB_PAIRS = SB_WIDTH // LANES
TQ = 256
N_CHUNKS = TQ // CHUNK
LR_PAD = LANES
WEIGHT_CHUNK_ROWS = 512
WEIGHT_BUFFERS = 3

OFF_SB_Q = 0
OFF_SB_K = OFF_SB_Q + SB_WIDTH
OFF_SB_V = OFF_SB_K + SB_WIDTH
OFF_SB_Z = OFF_SB_V + SB_WIDTH
OFF_GLA_Q = OFF_SB_Z + SB_WIDTH
OFF_GLA_K = OFF_GLA_Q + GLA_K_WIDTH
OFF_GLA_V = OFF_GLA_K + GLA_K_WIDTH
OFF_GLA_Z = OFF_GLA_V + GLA_V_WIDTH
OFF_LR = OFF_GLA_Z + GLA_V_WIDTH

VMEM_LIMIT_BYTES = 56 * 1024 * 1024

_F32 = jnp.float32
_BF16 = jnp.bfloat16
_NT = (((1,), (1,)), ((), ()))
_TN = (((0,), (0,)), ((), ()))


def _dot(a, b):
    return jnp.dot(a, b, preferred_element_type=_F32)


def _dot_nt(a, b):
    return lax.dot_general(a, b, _NT, preferred_element_type=_F32)


def _log_sigmoid(x):
    return jnp.minimum(x, 0.0) - jnp.log(1.0 + jnp.exp(-jnp.abs(x)))


def _silu(x):
    return x * jax.nn.sigmoid(x)


def _rms(x, gain):
    return x * lax.rsqrt(jnp.mean(x * x, axis=-1, keepdims=True) + EPS) * gain


def _stack_heads(x2):
    first_head = lax.broadcasted_iota(jnp.int32, (1, LANES), 1) < SB_HEAD_DIM
    return jnp.concatenate([jnp.where(first_head, x2, 0), jnp.where(first_head, 0, x2)], axis=0)


def _sb_neg_log2_miss(zz):
    return jnp.maximum(jnp.log(1.0 + jnp.exp2(jnp.minimum(zz, EXP2_CLAMP))) * LOG2_E, zz)


def _sb_incl():
    row = lax.broadcasted_iota(jnp.int32, (TQ, TQ), 0)
    col = lax.broadcasted_iota(jnp.int32, (TQ, TQ), 1)
    return (row >= col).astype(_BF16)


def _sb_near_blocks(q, k_all, vm_all, step, qs_ref, nc_ref, acc_ref):
    row = lax.broadcasted_iota(jnp.int32, (2 * TQ, TQ), 0) & (TQ - 1)
    col = lax.broadcasted_iota(jnp.int32, (2 * TQ, TQ), 1)
    causal = col < row
    incl = _sb_incl()
    keys = pl.ds(pl.multiple_of(step * TQ, TQ), 2 * TQ)
    for pair in range(SB_PAIRS):
        cols = slice(pair * LANES, (pair + 1) * LANES)
        qs = _stack_heads(q[:, cols])
        qs_ref[pair] = qs
        zz = _dot_nt(qs, k_all[keys, cols])
        zz_prev, zz_own = zz[:, :TQ], zz[:, TQ:]
        nl_prev = _sb_neg_log2_miss(zz_prev)
        nl_own = jnp.where(causal, _sb_neg_log2_miss(zz_own), 0.0)
        tot_own = _dot(nl_own.astype(_BF16), incl)
        tot_prev = _dot(nl_prev.astype(_BF16), incl) + tot_own[:, 0:1]
        w_prev = jnp.exp2(zz_prev - tot_prev)
        w_own = jnp.where(causal, jnp.exp2(zz_own - tot_own), 0.0)
        nc_ref[pair] = jnp.broadcast_to(tot_prev[:, 0:1], (2 * TQ, LANES))
        w_cat = jnp.concatenate([w_prev[:TQ], w_own[:TQ], w_prev[TQ:], w_own[TQ:]], axis=1).astype(_BF16)
        v_cat = jnp.concatenate([vm_all[0, keys, cols], vm_all[1, keys, cols]], axis=0)
        acc_ref[pair] = _dot(w_cat, v_cat)


def _sb_far_blocks(k_all, vm_all, step, qs_ref, nc_ref, acc_ref):
    def more_blocks(carry):
        t, live = carry
        return jnp.logical_and(t < step - 1, live)

    def earlier_block(carry):
        t, _ = carry
        keys = pl.ds(pl.multiple_of((step - 1 - t) * TQ, TQ), TQ)
        incl = _sb_incl()
        for pair in range(SB_PAIRS):
            cols = slice(pair * LANES, (pair + 1) * LANES)
            zz = _dot_nt(qs_ref[pair], k_all[keys, cols])
            nl = _sb_neg_log2_miss(zz)
            tot = _dot(nl.astype(_BF16), incl) + jnp.concatenate([nc_ref[pair]] * (TQ // LANES), axis=1)
            w = jnp.exp2(zz - tot)
            w_cat = jnp.concatenate([w[:TQ], w[TQ:]], axis=1).astype(_BF16)
            v_cat = jnp.concatenate([vm_all[0, keys, cols], vm_all[1, keys, cols]], axis=0)
            acc_ref[pair] += _dot(w_cat, v_cat)
            nc_ref[pair] = jnp.broadcast_to(tot[:, 0:1], (2 * TQ, LANES))
        return t + 1, jnp.min(nc_ref[...]) < SB_DEAD_LOG2

    lax.while_loop(more_blocks, earlier_block, (jnp.int32(0), jnp.min(nc_ref[...]) < SB_DEAD_LOG2))


def _chunk_block_diag(x):
    zero = jnp.zeros((CHUNK, LANES), x.dtype)
    out_rows = []
    for r in range(x.shape[0] // CHUNK):
        blocks = [x[r * CHUNK:(r + 1) * CHUNK] if c == r % N_CHUNKS else zero for c in range(N_CHUNKS)]
        out_rows.append(jnp.concatenate(blocks, axis=1))
    return jnp.concatenate(out_rows, axis=0)


def _gla(q, k, v, log_alpha, state_ref, gain):
    row = lax.broadcasted_iota(jnp.int32, (TQ, TQ), 0)
    col = lax.broadcasted_iota(jnp.int32, (TQ, TQ), 1)
    prefix = jnp.logical_and(row // CHUNK == col // CHUNK, col <= row).astype(_BF16)
    la_hi = log_alpha.astype(_BF16)
    la_lo = (log_alpha - la_hi.astype(_F32)).astype(_BF16)
    cum = _dot(prefix, la_hi) + _dot(prefix, la_lo)
    chunk_total = [cum[(c + 1) * CHUNK - 1:(c + 1) * CHUNK] for c in range(N_CHUNKS)]
    cum_last = jnp.concatenate([jnp.broadcast_to(t, (CHUNK, GLA_K_WIDTH)) for t in chunk_total], axis=0)
    q_dec = (q * jnp.exp(cum) * (GLA_KEY_DIM ** -0.5)).astype(_BF16)
    k_inv = (k * jnp.exp(-cum)).astype(_BF16)
    k_end = (k * jnp.exp(cum_last - cum)).astype(_BF16)
    decay = [jnp.exp(t) for t in chunk_total]
    v_bf = v.astype(_BF16)

    row2 = lax.broadcasted_iota(jnp.int32, (2 * TQ, TQ), 0) & (TQ - 1)
    col2 = lax.broadcasted_iota(jnp.int32, (2 * TQ, TQ), 1)
    causal2 = jnp.logical_and(row2 // CHUNK == col2 // CHUNK, col2 <= row2)
    first_head = lax.broadcasted_iota(jnp.int32, (1, LANES), 1) < GLA_KEY_DIM
    outs = []
    for pair in range(GLA_HEADS // 2):
        cols = slice(pair * LANES, (pair + 1) * LANES)
        v_pair = v_bf[:, 2 * pair * GLA_VAL_DIM:(2 * pair + 2) * GLA_VAL_DIM]
        qs = _stack_heads(q_dec[:, cols])
        scores = jnp.where(causal2, _dot_nt(qs, k_inv[:, cols]), 0.0).astype(_BF16)
        intra = (_dot(scores[:TQ], v_pair[:, :GLA_VAL_DIM]), _dot(scores[TQ:], v_pair[:, GLA_VAL_DIM:]))
        kv_all = lax.dot_general(v_pair, _chunk_block_diag(k_end[:, cols]), _TN, preferred_element_type=_F32)
        states = [state_ref[pair]]
        for c in range(N_CHUNKS):
            blk = slice(c * LANES, (c + 1) * LANES)
            kv = jnp.where(first_head, kv_all[:GLA_VAL_DIM, blk], kv_all[GLA_VAL_DIM:, blk])
            states.append(states[-1] * decay[c][:, cols] + kv)
        state_ref[pair] = states[-1]
        state_cat = jnp.concatenate(states[:N_CHUNKS], axis=1).astype(_BF16)
        inter = _dot_nt(_chunk_block_diag(qs), state_cat)
        outs.append(_rms(intra[0] + inter[:TQ], gain))
        outs.append(_rms(intra[1] + inter[TQ:], gain))
    return jnp.concatenate(outs, axis=1)


def _load_weights(w_in_t_hbm, w_o_sb_hbm, w_o_gla_hbm, w_out_hbm, w_in_ref, w_lr_ref, w_gate_ref,
                  w_o_sb_ref, w_o_gla_ref, w_out_ref, stage_ref, sem_ref):
    rows = WEIGHT_CHUNK_ROWS
    jobs = []

    def into_columns(dst, block):
        def consume(chunk):
            dst[:, block * rows:(block + 1) * rows] = chunk.T.astype(_BF16)
        return consume

    def into_rows(dst, block):
        def consume(chunk):
            dst[block * rows:(block + 1) * rows, :] = chunk.astype(_BF16)
        return consume

    def low_rank(chunk):
        lane = lax.broadcasted_iota(jnp.int32, (1, LR_PAD), 1)
        w_lr_ref[...] = jnp.where(lane < GLA_RANK, chunk[:LR_PAD].T, 0.0).astype(_BF16)

    w_in_t = w_in_t_hbm.at[0]
    for block in range(OFF_LR // rows):
        jobs.append((w_in_t, block * rows, into_columns(w_in_ref, block)))
    jobs.append((w_in_t, OFF_LR, low_rank))
    for block in range(2 * D_MODEL // rows):
        jobs.append((w_in_t, OFF_LR + GLA_RANK + block * rows, into_columns(w_gate_ref, block)))
    for src, dst in ((w_o_sb_hbm, w_o_sb_ref), (w_o_gla_hbm, w_o_gla_ref), (w_out_hbm, w_out_ref)):
        for block in range(dst.shape[0] // rows):
            jobs.append((src.at[0], block * rows, into_rows(dst, block)))

    n_buf = stage_ref.shape[0]

    def copy(j):
        src, row0, _ = jobs[j]
        return pltpu.make_async_copy(src.at[pl.ds(row0, rows), :], stage_ref.at[j % n_buf], sem_ref.at[j % n_buf])

    for j in range(min(n_buf - 1, len(jobs))):
        copy(j).start()
    for j in range(len(jobs)):
        if j + n_buf - 1 < len(jobs):
            copy(j + n_buf - 1).start()
        copy(j).wait()
        jobs[j][2](stage_ref[j % n_buf])


def _block_kernel(x_ref, g_pre_ref, w_in_t_hbm, b_gate_ref, w_up_ref, b_up_ref, g_gla_ref,
                  w_o_sb_hbm, w_o_gla_hbm, w_out_hbm, g_final_ref, out_ref,
                  w_in_ref, w_lr_ref, w_gate_ref, w_o_sb_ref, w_o_gla_ref, w_out_ref, stage_ref, sem_ref,
                  k_all, vm_all, state_ref, qs_ref, nc_ref, acc_ref):
    step = pl.program_id(1)

    @pl.when(jnp.logical_and(pl.program_id(0) == 0, step == 0))
    def _():
        _load_weights(w_in_t_hbm, w_o_sb_hbm, w_o_gla_hbm, w_out_hbm, w_in_ref, w_lr_ref, w_gate_ref,
                      w_o_sb_ref, w_o_gla_ref, w_out_ref, stage_ref, sem_ref)

    @pl.when(step == 0)
    def _():
        state_ref[...] = jnp.zeros_like(state_ref)
        k_all[0:TQ, :] = jnp.zeros((TQ, SB_WIDTH), _BF16)
        vm_all[:, 0:TQ, :] = jnp.zeros((2, TQ, SB_WIDTH), _BF16)

    x = x_ref[0]
    h = _rms(x, g_pre_ref[...]).astype(_BF16)

    def proj(offset, width):
        return _dot(h, w_in_ref[:, offset:offset + width])

    rows = pl.ds(pl.multiple_of((step + 1) * TQ, TQ), TQ)
    k_all[rows, :] = proj(OFF_SB_K, SB_WIDTH).astype(_BF16)
    sb_v = proj(OFF_SB_V, SB_WIDTH).astype(_BF16)
    first_head = (lax.broadcasted_iota(jnp.int32, (1, SB_WIDTH), 1) & (LANES - 1)) < SB_HEAD_DIM
    vm_all[0, rows, :] = jnp.where(first_head, sb_v, 0)
    vm_all[1, rows, :] = jnp.where(first_head, 0, sb_v)
    sb_q = (proj(OFF_SB_Q, SB_WIDTH) * (LOG2_E * SB_HEAD_DIM ** -0.5)).astype(_BF16)
    _sb_near_blocks(sb_q, k_all, vm_all, step, qs_ref, nc_ref, acc_ref)

    lr = _dot(h, w_lr_ref[...]).astype(_BF16)
    alpha_logit = _dot(lr, w_up_ref[...]) + b_up_ref[...]
    log_alpha = _log_sigmoid(alpha_logit) / GLA_GATE_TEMP
    o_gla = _gla(proj(OFF_GLA_Q, GLA_K_WIDTH), proj(OFF_GLA_K, GLA_K_WIDTH), proj(OFF_GLA_V, GLA_V_WIDTH),
                 log_alpha, state_ref, g_gla_ref[...])
    y_gla = _dot((o_gla * _silu(proj(OFF_GLA_Z, GLA_V_WIDTH))).astype(_BF16), w_o_gla_ref[...])
    gates = jax.nn.sigmoid(_dot(h, w_gate_ref[...]) + b_gate_ref[...])
    gated_gla = gates[:, D_MODEL:] * y_gla
    gate_sb = gates[:, :D_MODEL]
    sb_zs = _silu(proj(OFF_SB_Z, SB_WIDTH))

    _sb_far_blocks(k_all, vm_all, step, qs_ref, nc_ref, acc_ref)
    o_sb = jnp.concatenate([acc_ref[pair] for pair in range(SB_PAIRS)], axis=1)
    y_sb = _dot((o_sb * sb_zs).astype(_BF16), w_o_sb_ref[...])
    merged = gate_sb * y_sb + gated_gla
    res = x + _dot(merged.astype(_BF16), w_out_ref[...])
    out_ref[0] = _rms(res, g_final_ref[...])


def _resident(shape):
    return pl.BlockSpec(shape, lambda b, i: (0,) * len(shape), pipeline_mode=pl.Buffered(1))


@jax.jit
def kernel(x, g_pre, w_in, b_gate, w_alpha_up, b_alpha_up, g_gla_norm, w_o_sb, w_o_gla, w_out, g_final):
    batch, seq, d_model = x.shape
    depth = w_in.shape[0]
    assert d_model == D_MODEL and seq % TQ == 0 and depth == 1
    w_up = jnp.pad(w_alpha_up[0], ((0, LR_PAD - GLA_RANK), (0, 0))).astype(_BF16)
    w_in_t = jnp.transpose(w_in, (0, 2, 1))

    row_block = pl.BlockSpec((1, TQ, D_MODEL), lambda b, i: (b, i, 0))
    in_hbm = pl.BlockSpec(memory_space=pl.ANY)
    return pl.pallas_call(
        _block_kernel,
        grid=(batch, seq // TQ),
        in_specs=[
            row_block,
            _resident((1, D_MODEL)),
            in_hbm,
            _resident((1, 2 * D_MODEL)),
            _resident((LR_PAD, GLA_K_WIDTH)),
            _resident((1, GLA_K_WIDTH)),
            _resident((1, GLA_VAL_DIM)),
            in_hbm,
            in_hbm,
            in_hbm,
            _resident((1, D_MODEL)),
        ],
        out_specs=row_block,
        out_shape=jax.ShapeDtypeStruct(x.shape, x.dtype),
        scratch_shapes=[
            pltpu.VMEM((D_MODEL, OFF_LR), _BF16),
            pltpu.VMEM((D_MODEL, LR_PAD), _BF16),
            pltpu.VMEM((D_MODEL, 2 * D_MODEL), _BF16),
            pltpu.VMEM((SB_WIDTH, D_MODEL), _BF16),
            pltpu.VMEM((GLA_V_WIDTH, D_MODEL), _BF16),
            pltpu.VMEM((D_MODEL, D_MODEL), _BF16),
            pltpu.VMEM((WEIGHT_BUFFERS, WEIGHT_CHUNK_ROWS, D_MODEL), _F32),
            pltpu.SemaphoreType.DMA((WEIGHT_BUFFERS,)),
            pltpu.VMEM((seq + TQ, SB_WIDTH), _BF16),
            pltpu.VMEM((2, seq + TQ, SB_WIDTH), _BF16),
            pltpu.VMEM((GLA_HEADS // 2, GLA_VAL_DIM, LANES), _F32),
            pltpu.VMEM((SB_PAIRS, 2 * TQ, LANES), _BF16),
            pltpu.VMEM((SB_PAIRS, 2 * TQ, LANES), _F32),
            pltpu.VMEM((SB_PAIRS, TQ, LANES), _F32),
        ],
        compiler_params=pltpu.CompilerParams(
            dimension_semantics=("arbitrary", "arbitrary"), vmem_limit_bytes=VMEM_LIMIT_BYTES),
        name="hybrid_mixer_block",
    )(x, g_pre, w_in_t, b_gate, w_up, b_alpha_up, g_gla_norm, w_o_sb, w_o_gla, w_out, g_final.reshape(1, D_MODEL))
```

```python
import math

import jax
import jax.numpy as jnp
from jax import lax
from jax.experimental import pallas as pl
from jax.experimental.pallas import tpu as pltpu

D_MODEL = 1024
CHUNK = 64
SB_HEAD_DIM = 64
SB_WIDTH = D_MODEL // 2
GLA_HEADS = 4
GLA_K_WIDTH = D_MODEL // 4
GLA_V_WIDTH = D_MODEL // 2
GLA_KEY_DIM = GLA_K_WIDTH // GLA_HEADS
GLA_VAL_DIM = GLA_V_WIDTH // GLA_HEADS
GLA_RANK = 16
GLA_GATE_TEMP = 16.0
EPS = 1e-6
LOG2_E = math.log2(math.e)
EXP2_CLAMP = 126.0
SB_DEAD_LOG2 = 151.0

LANES = 128
SB_PAIRS = SB_WIDTH // LANES
TQ = 256
N_CHUNKS = TQ // CHUNK
LR_PAD = LANES
WEIGHT_CHUNK_ROWS = 512
WEIGHT_BUFFERS = 3

OFF_SB_Q = 0
OFF_SB_K = OFF_SB_Q + SB_WIDTH
OFF_SB_V = OFF_SB_K + SB_WIDTH
OFF_SB_Z = OFF_SB_V + SB_WIDTH
OFF_GLA_Q = OFF_SB_Z + SB_WIDTH
OFF_GLA_K = OFF_GLA_Q + GLA_K_WIDTH
OFF_GLA_V = OFF_GLA_K + GLA_K_WIDTH
OFF_GLA_Z = OFF_GLA_V + GLA_V_WIDTH
OFF_LR = OFF_GLA_Z + GLA_V_WIDTH

VMEM_LIMIT_BYTES = 56 * 1024 * 1024

_F32 = jnp.float32
_BF16 = jnp.bfloat16
_NT = (((1,), (1,)), ((), ()))
_TN = (((0,), (0,)), ((), ()))


def _dot(a, b):
    return jnp.dot(a, b, preferred_element_type=_F32)


def _dot_nt(a, b):
    return lax.dot_general(a, b, _NT, preferred_element_type=_F32)


def _log_sigmoid(x):
    return jnp.minimum(x, 0.0) - jnp.log(1.0 + jnp.exp(-jnp.abs(x)))


def _silu(x):
    return x * jax.nn.sigmoid(x)


def _rms(x, gain):
    return x * lax.rsqrt(jnp.mean(x * x, axis=-1, keepdims=True) + EPS) * gain


def _stack_heads(x2):
    first_head = lax.broadcasted_iota(jnp.int32, (1, LANES), 1) < SB_HEAD_DIM
    return jnp.concatenate([jnp.where(first_head, x2, 0), jnp.where(first_head, 0, x2)], axis=0)


def _sb_neg_log2_miss(zz):
    return jnp.maximum(jnp.log(1.0 + jnp.exp2(jnp.minimum(zz, EXP2_CLAMP))) * LOG2_E, zz)


def _sb_incl():
    row = lax.broadcasted_iota(jnp.int32, (TQ, TQ), 0)
    col = lax.broadcasted_iota(jnp.int32, (TQ, TQ), 1)
    return (row >= col).astype(_BF16)


def _sb_near_blocks(q, k_all, vm_all, step, qs_ref, nc_ref, acc_ref):
    row = lax.broadcasted_iota(jnp.int32, (2 * TQ, TQ), 0) & (TQ - 1)
    col = lax.broadcasted_iota(jnp.int32, (2 * TQ, TQ), 1)
    causal = col < row
    incl = _sb_incl()
    keys = pl.ds(pl.multiple_of(step * TQ, TQ), 2 * TQ)
    for pair in range(SB_PAIRS):
        cols = slice(pair * LANES, (pair + 1) * LANES)
        qs = _stack_heads(q[:, cols])
        qs_ref[pair] = qs
        zz = _dot_nt(qs, k_all[keys, cols])
        zz_prev, zz_own = zz[:, :TQ], zz[:, TQ:]
        nl_prev = _sb_neg_log2_miss(zz_prev)
        nl_own = jnp.where(causal, _sb_neg_log2_miss(zz_own), 0.0)
        tot_own = _dot(nl_own.astype(_BF16), incl)
        tot_prev = _dot(nl_prev.astype(_BF16), incl) + tot_own[:, 0:1]
        w_prev = jnp.exp2(zz_prev - tot_prev)
        w_own = jnp.where(causal, jnp.exp2(zz_own - tot_own), 0.0)
        nc_ref[pair] = jnp.broadcast_to(tot_prev[:, 0:1], (2 * TQ, LANES))
        w_cat = jnp.concatenate([w_prev[:TQ], w_own[:TQ], w_prev[TQ:], w_own[TQ:]], axis=1).astype(_BF16)
        v_cat = jnp.concatenate([vm_all[0, keys, cols], vm_all[1, keys, cols]], axis=0)
        acc_ref[pair] = _dot(w_cat, v_cat)


def _sb_far_blocks(k_all, vm_all, step, qs_ref, nc_ref, acc_ref):
    def more_blocks(carry):
        t, live = carry
        return jnp.logical_and(t < step - 1, live)

    def earlier_block(carry):
        t, _ = carry
        keys = pl.ds(pl.multiple_of((step - 1 - t) * TQ, TQ), TQ)
        incl = _sb_incl()
        for pair in range(SB_PAIRS):
            cols = slice(pair * LANES, (pair + 1) * LANES)
            zz = _dot_nt(qs_ref[pair], k_all[keys, cols])
            nl = _sb_neg_log2_miss(zz)
            tot = _dot(nl.astype(_BF16), incl) + jnp.concatenate([nc_ref[pair]] * (TQ // LANES), axis=1)
            w = jnp.exp2(zz - tot)
            w_cat = jnp.concatenate([w[:TQ], w[TQ:]], axis=1).astype(_BF16)
            v_cat = jnp.concatenate([vm_all[0, keys, cols], vm_all[1, keys, cols]], axis=0)
            acc_ref[pair] += _dot(w_cat, v_cat)
            nc_ref[pair] = jnp.broadcast_to(tot[:, 0:1], (2 * TQ, LANES))
        return t + 1, jnp.min(nc_ref[...]) < SB_DEAD_LOG2

    lax.while_loop(more_blocks, earlier_block, (jnp.int32(0), jnp.min(nc_ref[...]) < SB_DEAD_LOG2))


def _chunk_block_diag(x):
    zero = jnp.zeros((CHUNK, LANES), x.dtype)
    out_rows = []
    for r in range(x.shape[0] // CHUNK):
        blocks = [x[r * CHUNK:(r + 1) * CHUNK] if c == r % N_CHUNKS else zero for c in range(N_CHUNKS)]
        out_rows.append(jnp.concatenate(blocks, axis=1))
    return jnp.concatenate(out_rows, axis=0)


def _gla(q, k, v, log_alpha, state_ref, gain):
    row = lax.broadcasted_iota(jnp.int32, (TQ, TQ), 0)
    col = lax.broadcasted_iota(jnp.int32, (TQ, TQ), 1)
    prefix = jnp.logical_and(row // CHUNK == col // CHUNK, col <= row).astype(_BF16)
    la_hi = log_alpha.astype(_BF16)
    la_lo = (log_alpha - la_hi.astype(_F32)).astype(_BF16)
    cum = _dot(prefix, la_hi) + _dot(prefix, la_lo)
    chunk_total = [cum[(c + 1) * CHUNK - 1:(c + 1) * CHUNK] for c in range(N_CHUNKS)]
    cum_last = jnp.concatenate([jnp.broadcast_to(t, (CHUNK, GLA_K_WIDTH)) for t in chunk_total], axis=0)
    q_dec = (q * jnp.exp(cum) * (GLA_KEY_DIM ** -0.5)).astype(_BF16)
    k_inv = (k * jnp.exp(-cum)).astype(_BF16)
    k_end = (k * jnp.exp(cum_last - cum)).astype(_BF16)
    decay = [jnp.exp(t) for t in chunk_total]
    v_bf = v.astype(_BF16)

    row2 = lax.broadcasted_iota(jnp.int32, (2 * TQ, TQ), 0) & (TQ - 1)
    col2 = lax.broadcasted_iota(jnp.int32, (2 * TQ, TQ), 1)
    causal2 = jnp.logical_and(row2 // CHUNK == col2 // CHUNK, col2 <= row2)
    first_head = lax.broadcasted_iota(jnp.int32, (1, LANES), 1) < GLA_KEY_DIM
    outs = []
    for pair in range(GLA_HEADS // 2):
        cols = slice(pair * LANES, (pair + 1) * LANES)
        v_pair = v_bf[:, 2 * pair * GLA_VAL_DIM:(2 * pair + 2) * GLA_VAL_DIM]
        qs = _stack_heads(q_dec[:, cols])
        scores = jnp.where(causal2, _dot_nt(qs, k_inv[:, cols]), 0.0).astype(_BF16)
        intra = (_dot(scores[:TQ], v_pair[:, :GLA_VAL_DIM]), _dot(scores[TQ:], v_pair[:, GLA_VAL_DIM:]))
        kv_all = lax.dot_general(v_pair, _chunk_block_diag(k_end[:, cols]), _TN, preferred_element_type=_F32)
        states = [state_ref[pair]]
        for c in range(N_CHUNKS):
            blk = slice(c * LANES, (c + 1) * LANES)
            kv = jnp.where(first_head, kv_all[:GLA_VAL_DIM, blk], kv_all[GLA_VAL_DIM:, blk])
            states.append(states[-1] * decay[c][:, cols] + kv)
        state_ref[pair] = states[-1]
        state_cat = jnp.concatenate(states[:N_CHUNKS], axis=1).astype(_BF16)
        inter = _dot_nt(_chunk_block_diag(qs), state_cat)
        outs.append(_rms(intra[0] + inter[:TQ], gain))
        outs.append(_rms(intra[1] + inter[TQ:], gain))
    return jnp.concatenate(outs, axis=1)


def _load_weights(w_in_t_hbm, w_o_sb_hbm, w_o_gla_hbm, w_out_hbm, w_in_ref, w_lr_ref, w_gate_ref,
                  w_o_sb_ref, w_o_gla_ref, w_out_ref, stage_ref, sem_ref):
    rows = WEIGHT_CHUNK_ROWS
    jobs = []

    def into_columns(dst, block):
        def consume(chunk):
            dst[:, block * rows:(block + 1) * rows] = chunk.T.astype(_BF16)
        return consume

    def into_rows(dst, block):
        def consume(chunk):
            dst[block * rows:(block + 1) * rows, :] = chunk.astype(_BF16)
        return consume

    def low_rank(chunk):
        lane = lax.broadcasted_iota(jnp.int32, (1, LR_PAD), 1)
        w_lr_ref[...] = jnp.where(lane < GLA_RANK, chunk[:LR_PAD].T, 0.0).astype(_BF16)

    w_in_t = w_in_t_hbm.at[0]
    for block in range(OFF_LR // rows):
        jobs.append((w_in_t, block * rows, into_columns(w_in_ref, block)))
    jobs.append((w_in_t, OFF_LR, low_rank))
    for block in range(2 * D_MODEL // rows):
        jobs.append((w_in_t, OFF_LR + GLA_RANK + block * rows, into_columns(w_gate_ref, block)))
    for src, dst in ((w_o_sb_hbm, w_o_sb_ref), (w_o_gla_hbm, w_o_gla_ref), (w_out_hbm, w_out_ref)):
        for block in range(dst.shape[0] // rows):
            jobs.append((src.at[0], block * rows, into_rows(dst, block)))

    n_buf = stage_ref.shape[0]

    def copy(j):
        src, row0, _ = jobs[j]
        return pltpu.make_async_copy(src.at[pl.ds(row0, rows), :], stage_ref.at[j % n_buf], sem_ref.at[j % n_buf])

    for j in range(min(n_buf - 1, len(jobs))):
        copy(j).start()
    for j in range(len(jobs)):
        if j + n_buf - 1 < len(jobs):
            copy(j + n_buf - 1).start()
        copy(j).wait()
        jobs[j][2](stage_ref[j % n_buf])


def _block_kernel(x_ref, g_pre_ref, w_in_t_hbm, b_gate_ref, w_up_ref, b_up_ref, g_gla_ref,
                  w_o_sb_hbm, w_o_gla_hbm, w_out_hbm, g_final_ref, out_ref,
                  w_in_ref, w_lr_ref, w_gate_ref, w_o_sb_ref, w_o_gla_ref, w_out_ref, stage_ref, sem_ref,
                  k_all, vm_all, state_ref, qs_ref, nc_ref, acc_ref):
    step = pl.program_id(1)

    @pl.when(jnp.logical_and(pl.program_id(0) == 0, step == 0))
    def _():
        _load_weights(w_in_t_hbm, w_o_sb_hbm, w_o_gla_hbm, w_out_hbm, w_in_ref, w_lr_ref, w_gate_ref,
                      w_o_sb_ref, w_o_gla_ref, w_out_ref, stage_ref, sem_ref)

    @pl.when(step == 0)
    def _():
        state_ref[...] = jnp.zeros_like(state_ref)
        k_all[0:TQ, :] = jnp.zeros((TQ, SB_WIDTH), _BF16)
        vm_all[:, 0:TQ, :] = jnp.zeros((2, TQ, SB_WIDTH), _BF16)

    x = x_ref[0]
    h = (x * g_pre_ref[...]).astype(_BF16)
    inv_rms = jnp.broadcast_to(lax.rsqrt(jnp.mean(x * x, axis=-1, keepdims=True) + EPS), (TQ, LANES))

    def scale_rows(p):
        return p * jnp.concatenate([inv_rms] * (p.shape[1] // LANES), axis=1)

    def proj(offset, width):
        return scale_rows(_dot(h, w_in_ref[:, offset:offset + width]))

    rows = pl.ds(pl.multiple_of((step + 1) * TQ, TQ), TQ)
    k_all[rows, :] = proj(OFF_SB_K, SB_WIDTH).astype(_BF16)
    sb_v = proj(OFF_SB_V, SB_WIDTH).astype(_BF16)
    first_head = (lax.broadcasted_iota(jnp.int32, (1, SB_WIDTH), 1) & (LANES - 1)) < SB_HEAD_DIM
    vm_all[0, rows, :] = jnp.where(first_head, sb_v, 0)
    vm_all[1, rows, :] = jnp.where(first_head, 0, sb_v)
    sb_q = (proj(OFF_SB_Q, SB_WIDTH) * (LOG2_E * SB_HEAD_DIM ** -0.5)).astype(_BF16)
    _sb_near_blocks(sb_q, k_all, vm_all, step, qs_ref, nc_ref, acc_ref)

    lr = scale_rows(_dot(h, w_lr_ref[...])).astype(_BF16)
    alpha_logit = _dot(lr, w_up_ref[...]) + b_up_ref[...]
    log_alpha = _log_sigmoid(alpha_logit) / GLA_GATE_TEMP
    o_gla = _gla(proj(OFF_GLA_Q, GLA_K_WIDTH), proj(OFF_GLA_K, GLA_K_WIDTH), proj(OFF_GLA_V, GLA_V_WIDTH),
                 log_alpha, state_ref, g_gla_ref[...])
    y_gla = _dot((o_gla * _silu(proj(OFF_GLA_Z, GLA_V_WIDTH))).astype(_BF16), w_o_gla_ref[...])
    gates = jax.nn.sigmoid(scale_rows(_dot(h, w_gate_ref[...])) + b_gate_ref[...])
    gated_gla = gates[:, D_MODEL:] * y_gla
    gate_sb = gates[:, :D_MODEL]
    sb_zs = _silu(proj(OFF_SB_Z, SB_WIDTH))

    def finish():
        o_sb = jnp.concatenate([acc_ref[pair] for pair in range(SB_PAIRS)], axis=1)
        y_sb = _dot((o_sb * sb_zs).astype(_BF16), w_o_sb_ref[...])
        merged = gate_sb * y_sb + gated_gla
        res = x + _dot(merged.astype(_BF16), w_out_ref[...])
        out_ref[0] = _rms(res, g_final_ref[...])

    finish()

    @pl.when(jnp.logical_and(step >= 2, jnp.min(nc_ref[...]) < SB_DEAD_LOG2))
    def _():
        _sb_far_blocks(k_all, vm_all, step, qs_ref, nc_ref, acc_ref)
        finish()


def _resident(shape):
    return pl.BlockSpec(shape, lambda b, i: (0,) * len(shape), pipeline_mode=pl.Buffered(1))


@jax.jit
def kernel(x, g_pre, w_in, b_gate, w_alpha_up, b_alpha_up, g_gla_norm, w_o_sb, w_o_gla, w_out, g_final):
    batch, seq, d_model = x.shape
    depth = w_in.shape[0]
    assert d_model == D_MODEL and seq % TQ == 0 and depth == 1
    w_up = jnp.pad(w_alpha_up[0], ((0, LR_PAD - GLA_RANK), (0, 0))).astype(_BF16)
    w_in_t = jnp.transpose(w_in, (0, 2, 1))

    row_block = pl.BlockSpec((1, TQ, D_MODEL), lambda b, i: (b, i, 0))
    in_hbm = pl.BlockSpec(memory_space=pl.ANY)
    return pl.pallas_call(
        _block_kernel,
        grid=(batch, seq // TQ),
        in_specs=[
            row_block,
            _resident((1, D_MODEL)),
            in_hbm,
            _resident((1, 2 * D_MODEL)),
            _resident((LR_PAD, GLA_K_WIDTH)),
            _resident((1, GLA_K_WIDTH)),
            _resident((1, GLA_VAL_DIM)),
            in_hbm,
            in_hbm,
            in_hbm,
            _resident((1, D_MODEL)),
        ],
        out_specs=row_block,
        out_shape=jax.ShapeDtypeStruct(x.shape, x.dtype),
        scratch_shapes=[
            pltpu.VMEM((D_MODEL, OFF_LR), _BF16),
            pltpu.VMEM((D_MODEL, LR_PAD), _BF16),
            pltpu.VMEM((D_MODEL, 2 * D_MODEL), _BF16),
            pltpu.VMEM((SB_WIDTH, D_MODEL), _BF16),
            pltpu.VMEM((GLA_V_WIDTH, D_MODEL), _BF16),
            pltpu.VMEM((D_MODEL, D_MODEL), _BF16),
            pltpu.VMEM((WEIGHT_BUFFERS, WEIGHT_CHUNK_ROWS, D_MODEL), _F32),
            pltpu.SemaphoreType.DMA((WEIGHT_BUFFERS,)),
            pltpu.VMEM((seq + TQ, SB_WIDTH), _BF16),
            pltpu.VMEM((2, seq + TQ, SB_WIDTH), _BF16),
            pltpu.VMEM((GLA_HEADS // 2, GLA_VAL_DIM, LANES), _F32),
            pltpu.VMEM((SB_PAIRS, 2 * TQ, LANES), _BF16),
            pltpu.VMEM((SB_PAIRS, 2 * TQ, LANES), _F32),
            pltpu.VMEM((SB_PAIRS, TQ, LANES), _F32),
        ],
        compiler_params=pltpu.CompilerParams(
            dimension_semantics=("arbitrary", "arbitrary"), vmem_limit_bytes=VMEM_LIMIT_BYTES),
        name="hybrid_mixer_block",
    )(x, g_pre, w_in_t, b_gate, w_up, b_alpha_up, g_gla_norm, w_o_sb, w_o_gla, w_out, g_final.reshape(1, D_MODEL))
```

```python
import math

import jax
import jax.numpy as jnp
from jax import lax
from jax.experimental import pallas as pl
from jax.experimental.pallas import tpu as pltpu

D_MODEL = 1024
CHUNK = 64
SB_HEAD_DIM = 64
SB_WIDTH = D_MODEL // 2
GLA_HEADS = 4
GLA_K_WIDTH = D_MODEL // 4
GLA_V_WIDTH = D_MODEL // 2
GLA_KEY_DIM = GLA_K_WIDTH // GLA_HEADS
GLA_VAL_DIM = GLA_V_WIDTH // GLA_HEADS
GLA_RANK = 16
GLA_GATE_TEMP = 16.0
EPS = 1e-6
LOG2_E = math.log2(math.e)
EXP2_CLAMP = 126.0
SB_DEAD_LOG2 = 151.0

LANES = 128
SB_PAIRS = SB_WIDTH // LANES
SB_GROUP = 2
TQ = 256
N_CHUNKS = TQ // CHUNK
LR_PAD = LANES
WEIGHT_CHUNK_ROWS = 512
WEIGHT_BUFFERS = 3

OFF_SB_Q = 0
OFF_SB_K = OFF_SB_Q + SB_WIDTH
OFF_SB_V = OFF_SB_K + SB_WIDTH
OFF_SB_Z = OFF_SB_V + SB_WIDTH
OFF_GLA_Q = OFF_SB_Z + SB_WIDTH
OFF_GLA_K = OFF_GLA_Q + GLA_K_WIDTH
OFF_GLA_V = OFF_GLA_K + GLA_K_WIDTH
OFF_GLA_Z = OFF_GLA_V + GLA_V_WIDTH
OFF_LR = OFF_GLA_Z + GLA_V_WIDTH

VMEM_LIMIT_BYTES = 56 * 1024 * 1024

_F32 = jnp.float32
_BF16 = jnp.bfloat16
_NT = (((1,), (1,)), ((), ()))
_TN = (((0,), (0,)), ((), ()))


def _dot(a, b):
    return jnp.dot(a, b, preferred_element_type=_F32)


def _dot_nt(a, b):
    return lax.dot_general(a, b, _NT, preferred_element_type=_F32)


def _log_sigmoid(x):
    return jnp.minimum(x, 0.0) - jnp.log(1.0 + jnp.exp(-jnp.abs(x)))


def _silu(x):
    return x * jax.nn.sigmoid(x)


def _rms(x, gain):
    return x * lax.rsqrt(jnp.mean(x * x, axis=-1, keepdims=True) + EPS) * gain


def _stack_heads(x2):
    first_head = lax.broadcasted_iota(jnp.int32, (1, LANES), 1) < SB_HEAD_DIM
    return jnp.concatenate([jnp.where(first_head, x2, 0), jnp.where(first_head, 0, x2)], axis=0)


def _sb_neg_log2_miss(zz):
    return jnp.maximum(jnp.log(1.0 + jnp.exp2(jnp.minimum(zz, EXP2_CLAMP))) * LOG2_E, zz)


def _sb_incl():
    row = lax.broadcasted_iota(jnp.int32, (TQ, TQ), 0)
    col = lax.broadcasted_iota(jnp.int32, (TQ, TQ), 1)
    return (row >= col).astype(_BF16)


def _sb_near_blocks(q, k_all, vm_all, step, qs_ref, nc_ref, acc_ref):
    n_heads = 2 * SB_GROUP
    width = SB_GROUP * LANES
    row = lax.broadcasted_iota(jnp.int32, (n_heads * TQ, TQ), 0) & (TQ - 1)
    col = lax.broadcasted_iota(jnp.int32, (n_heads * TQ, TQ), 1)
    causal = col < row
    incl = _sb_incl()
    keys = pl.ds(pl.multiple_of(step * TQ, TQ), 2 * TQ)
    head_of_lane = lax.broadcasted_iota(jnp.int32, (1, width), 1) // SB_HEAD_DIM
    zero_v = jnp.zeros((2 * TQ, LANES), _BF16)
    for group in range(SB_PAIRS // SB_GROUP):
        cols = slice(group * width, (group + 1) * width)
        q_group = q[:, cols]
        qs = jnp.concatenate([jnp.where(head_of_lane == hd, q_group, 0) for hd in range(n_heads)], axis=0)
        zz = _dot_nt(qs, k_all[keys, cols])
        zz_prev, zz_own = zz[:, :TQ], zz[:, TQ:]
        nl_prev = _sb_neg_log2_miss(zz_prev)
        nl_own = jnp.where(causal, _sb_neg_log2_miss(zz_own), 0.0)
        tot_own = _dot(nl_own.astype(_BF16), incl)
        tot_prev = _dot(nl_prev.astype(_BF16), incl) + tot_own[:, 0:1]
        w_prev = jnp.exp2(zz_prev - tot_prev)
        w_own = jnp.where(causal, jnp.exp2(zz_own - tot_own), 0.0)
        carried = jnp.broadcast_to(tot_prev[:, 0:1], (n_heads * TQ, LANES))
        w_cat = jnp.concatenate(
            [blk[hd * TQ:(hd + 1) * TQ] for hd in range(n_heads) for blk in (w_prev, w_own)], axis=1).astype(_BF16)
        v_cat = jnp.concatenate([
            jnp.concatenate([vm_all[hd % 2, keys, (group * SB_GROUP + p) * LANES:(group * SB_GROUP + p + 1) * LANES]
                             if p == hd // 2 else zero_v for p in range(SB_GROUP)], axis=1)
            for hd in range(n_heads)], axis=0)
        out = _dot(w_cat, v_cat)
        for p in range(SB_GROUP):
            pair = group * SB_GROUP + p
            heads = slice(2 * p * TQ, (2 * p + 2) * TQ)
            qs_ref[pair] = qs[heads, p * LANES:(p + 1) * LANES]
            nc_ref[pair] = carried[heads]
            acc_ref[pair] = out[:, p * LANES:(p + 1) * LANES]


def _sb_far_blocks(k_all, vm_all, step, qs_ref, nc_ref, acc_ref):
    def more_blocks(carry):
        t, live = carry
        return jnp.logical_and(t < step - 1, live)

    def earlier_block(carry):
        t, _ = carry
        keys = pl.ds(pl.multiple_of((step - 1 - t) * TQ, TQ), TQ)
        incl = _sb_incl()
        for pair in range(SB_PAIRS):
            cols = slice(pair * LANES, (pair + 1) * LANES)
            zz = _dot_nt(qs_ref[pair], k_all[keys, cols])
            nl = _sb_neg_log2_miss(zz)
            tot = _dot(nl.astype(_BF16), incl) + jnp.concatenate([nc_ref[pair]] * (TQ // LANES), axis=1)
            w = jnp.exp2(zz - tot)
            w_cat = jnp.concatenate([w[:TQ], w[TQ:]], axis=1).astype(_BF16)
            v_cat = jnp.concatenate([vm_all[0, keys, cols], vm_all[1, keys, cols]], axis=0)
            acc_ref[pair] += _dot(w_cat, v_cat)
            nc_ref[pair] = jnp.broadcast_to(tot[:, 0:1], (2 * TQ, LANES))
        return t + 1, jnp.min(nc_ref[...]) < SB_DEAD_LOG2

    lax.while_loop(more_blocks, earlier_block, (jnp.int32(0), jnp.min(nc_ref[...]) < SB_DEAD_LOG2))


def _chunk_block_diag(x):
    zero = jnp.zeros((CHUNK, LANES), x.dtype)
    out_rows = []
    for r in range(x.shape[0] // CHUNK):
        blocks = [x[r * CHUNK:(r + 1) * CHUNK] if c == r % N_CHUNKS else zero for c in range(N_CHUNKS)]
        out_rows.append(jnp.concatenate(blocks, axis=1))
    return jnp.concatenate(out_rows, axis=0)


def _gla(q, k, v, log_alpha, state_ref, gain):
    row = lax.broadcasted_iota(jnp.int32, (TQ, TQ), 0)
    col = lax.broadcasted_iota(jnp.int32, (TQ, TQ), 1)
    prefix = jnp.logical_and(row // CHUNK == col // CHUNK, col <= row).astype(_BF16)
    la_hi = log_alpha.astype(_BF16)
    la_lo = (log_alpha - la_hi.astype(_F32)).astype(_BF16)
    cum = _dot(prefix, la_hi) + _dot(prefix, la_lo)
    chunk_total = [cum[(c + 1) * CHUNK - 1:(c + 1) * CHUNK] for c in range(N_CHUNKS)]
    cum_last = jnp.concatenate([jnp.broadcast_to(t, (CHUNK, GLA_K_WIDTH)) for t in chunk_total], axis=0)
    q_dec = (q * jnp.exp(cum) * (GLA_KEY_DIM ** -0.5)).astype(_BF16)
    k_inv = (k * jnp.exp(-cum)).astype(_BF16)
    k_end = (k * jnp.exp(cum_last - cum)).astype(_BF16)
    decay = [jnp.exp(t) for t in chunk_total]
    v_bf = v.astype(_BF16)

    row2 = lax.broadcasted_iota(jnp.int32, (2 * TQ, TQ), 0) & (TQ - 1)
    col2 = lax.broadcasted_iota(jnp.int32, (2 * TQ, TQ), 1)
    causal2 = jnp.logical_and(row2 // CHUNK == col2 // CHUNK, col2 <= row2)
    first_head = lax.broadcasted_iota(jnp.int32, (1, LANES), 1) < GLA_KEY_DIM
    outs = []
    for pair in range(GLA_HEADS // 2):
        cols = slice(pair * LANES, (pair + 1) * LANES)
        v_pair = v_bf[:, 2 * pair * GLA_VAL_DIM:(2 * pair + 2) * GLA_VAL_DIM]
        qs = _stack_heads(q_dec[:, cols])
        scores = jnp.where(causal2, _dot_nt(qs, k_inv[:, cols]), 0.0).astype(_BF16)
        intra = (_dot(scores[:TQ], v_pair[:, :GLA_VAL_DIM]), _dot(scores[TQ:], v_pair[:, GLA_VAL_DIM:]))
        kv_all = lax.dot_general(v_pair, _chunk_block_diag(k_end[:, cols]), _TN, preferred_element_type=_F32)
        states = [state_ref[pair]]
        for c in range(N_CHUNKS):
            blk = slice(c * LANES, (c + 1) * LANES)
            kv = jnp.where(first_head, kv_all[:GLA_VAL_DIM, blk], kv_all[GLA_VAL_DIM:, blk])
            states.append(states[-1] * decay[c][:, cols] + kv)
        state_ref[pair] = states[-1]
        state_cat = jnp.concatenate(states[:N_CHUNKS], axis=1).astype(_BF16)
        inter = _dot_nt(_chunk_block_diag(qs), state_cat)
        outs.append(_rms(intra[0] + inter[:TQ], gain))
        outs.append(_rms(intra[1] + inter[TQ:], gain))
    return jnp.concatenate(outs, axis=1)


def _load_weights(w_in_t_hbm, w_o_sb_hbm, w_o_gla_hbm, w_out_hbm, w_in_ref, w_lr_ref, w_gate_ref,
                  w_o_sb_ref, w_o_gla_ref, w_out_ref, stage_ref, sem_ref):
    rows = WEIGHT_CHUNK_ROWS
    jobs = []

    def into_columns(dst, block):
        def consume(chunk):
            dst[:, block * rows:(block + 1) * rows] = chunk.T.astype(_BF16)
        return consume

    def into_rows(dst, block):
        def consume(chunk):
            dst[block * rows:(block + 1) * rows, :] = chunk.astype(_BF16)
        return consume

    def low_rank(chunk):
        lane = lax.broadcasted_iota(jnp.int32, (1, LR_PAD), 1)
        w_lr_ref[...] = jnp.where(lane < GLA_RANK, chunk[:LR_PAD].T, 0.0).astype(_BF16)

    w_in_t = w_in_t_hbm.at[0]
    for block in range(OFF_LR // rows):
        jobs.append((w_in_t, block * rows, into_columns(w_in_ref, block)))
    jobs.append((w_in_t, OFF_LR, low_rank))
    for block in range(2 * D_MODEL // rows):
        jobs.append((w_in_t, OFF_LR + GLA_RANK + block * rows, into_columns(w_gate_ref, block)))
    for src, dst in ((w_o_sb_hbm, w_o_sb_ref), (w_o_gla_hbm, w_o_gla_ref), (w_out_hbm, w_out_ref)):
        for block in range(dst.shape[0] // rows):
            jobs.append((src.at[0], block * rows, into_rows(dst, block)))

    n_buf = stage_ref.shape[0]

    def copy(j):
        src, row0, _ = jobs[j]
        return pltpu.make_async_copy(src.at[pl.ds(row0, rows), :], stage_ref.at[j % n_buf], sem_ref.at[j % n_buf])

    for j in range(min(n_buf - 1, len(jobs))):
        copy(j).start()
    for j in range(len(jobs)):
        if j + n_buf - 1 < len(jobs):
            copy(j + n_buf - 1).start()
        copy(j).wait()
        jobs[j][2](stage_ref[j % n_buf])


def _block_kernel(x_ref, g_pre_ref, w_in_t_hbm, b_gate_ref, w_up_ref, b_up_ref, g_gla_ref,
                  w_o_sb_hbm, w_o_gla_hbm, w_out_hbm, g_final_ref, out_ref,
                  w_in_ref, w_lr_ref, w_gate_ref, w_o_sb_ref, w_o_gla_ref, w_out_ref, stage_ref, sem_ref,
                  k_all, vm_all, state_ref, qs_ref, nc_ref, acc_ref):
    step = pl.program_id(1)

    @pl.when(jnp.logical_and(pl.program_id(0) == 0, step == 0))
    def _():
        _load_weights(w_in_t_hbm, w_o_sb_hbm, w_o_gla_hbm, w_out_hbm, w_in_ref, w_lr_ref, w_gate_ref,
                      w_o_sb_ref, w_o_gla_ref, w_out_ref, stage_ref, sem_ref)

    @pl.when(step == 0)
    def _():
        state_ref[...] = jnp.zeros_like(state_ref)
        k_all[0:TQ, :] = jnp.zeros((TQ, SB_WIDTH), _BF16)
        vm_all[:, 0:TQ, :] = jnp.zeros((2, TQ, SB_WIDTH), _BF16)

    x = x_ref[0]
    h = (x * g_pre_ref[...]).astype(_BF16)
    inv_rms = jnp.broadcast_to(lax.rsqrt(jnp.mean(x * x, axis=-1, keepdims=True) + EPS), (TQ, LANES))

    def scale_rows(p):
        return p * jnp.concatenate([inv_rms] * (p.shape[1] // LANES), axis=1)

    def proj(offset, width):
        return scale_rows(_dot(h, w_in_ref[:, offset:offset + width]))

    rows = pl.ds(pl.multiple_of((step + 1) * TQ, TQ), TQ)
    k_all[rows, :] = proj(OFF_SB_K, SB_WIDTH).astype(_BF16)
    sb_v = proj(OFF_SB_V, SB_WIDTH).astype(_BF16)
    first_head = (lax.broadcasted_iota(jnp.int32, (1, SB_WIDTH), 1) & (LANES - 1)) < SB_HEAD_DIM
    vm_all[0, rows, :] = jnp.where(first_head, sb_v, 0)
    vm_all[1, rows, :] = jnp.where(first_head, 0, sb_v)
    sb_q = (proj(OFF_SB_Q, SB_WIDTH) * (LOG2_E * SB_HEAD_DIM ** -0.5)).astype(_BF16)
    _sb_near_blocks(sb_q, k_all, vm_all, step, qs_ref, nc_ref, acc_ref)

    lr = scale_rows(_dot(h, w_lr_ref[...])).astype(_BF16)
    alpha_logit = _dot(lr, w_up_ref[...]) + b_up_ref[...]
    log_alpha = _log_sigmoid(alpha_logit) / GLA_GATE_TEMP
    o_gla = _gla(proj(OFF_GLA_Q, GLA_K_WIDTH), proj(OFF_GLA_K, GLA_K_WIDTH), proj(OFF_GLA_V, GLA_V_WIDTH),
                 log_alpha, state_ref, g_gla_ref[...])
    y_gla = _dot((o_gla * _silu(proj(OFF_GLA_Z, GLA_V_WIDTH))).astype(_BF16), w_o_gla_ref[...])
    gates = jax.nn.sigmoid(scale_rows(_dot(h, w_gate_ref[...])) + b_gate_ref[...])
    gated_gla = gates[:, D_MODEL:] * y_gla
    gate_sb = gates[:, :D_MODEL]
    sb_zs = _silu(proj(OFF_SB_Z, SB_WIDTH))

    def finish():
        o_sb = jnp.concatenate([acc_ref[pair] for pair in range(SB_PAIRS)], axis=1)
        y_sb = _dot((o_sb * sb_zs).astype(_BF16), w_o_sb_ref[...])
        merged = gate_sb * y_sb + gated_gla
        res = x + _dot(merged.astype(_BF16), w_out_ref[...])
        out_ref[0] = _rms(res, g_final_ref[...])

    finish()

    @pl.when(jnp.logical_and(step >= 2, jnp.min(nc_ref[...]) < SB_DEAD_LOG2))
    def _():
        _sb_far_blocks(k_all, vm_all, step, qs_ref, nc_ref, acc_ref)
        finish()


def _resident(shape):
    return pl.BlockSpec(shape, lambda b, i: (0,) * len(shape), pipeline_mode=pl.Buffered(1))


@jax.jit
def kernel(x, g_pre, w_in, b_gate, w_alpha_up, b_alpha_up, g_gla_norm, w_o_sb, w_o_gla, w_out, g_final):
    batch, seq, d_model = x.shape
    depth = w_in.shape[0]
    assert d_model == D_MODEL and seq % TQ == 0 and depth == 1
    w_up = jnp.pad(w_alpha_up[0], ((0, LR_PAD - GLA_RANK), (0, 0))).astype(_BF16)
    w_in_t = jnp.transpose(w_in, (0, 2, 1))

    row_block = pl.BlockSpec((1, TQ, D_MODEL), lambda b, i: (b, i, 0))
    in_hbm = pl.BlockSpec(memory_space=pl.ANY)
    return pl.pallas_call(
        _block_kernel,
        grid=(batch, seq // TQ),
        in_specs=[
            row_block,
            _resident((1, D_MODEL)),
            in_hbm,
            _resident((1, 2 * D_MODEL)),
            _resident((LR_PAD, GLA_K_WIDTH)),
            _resident((1, GLA_K_WIDTH)),
            _resident((1, GLA_VAL_DIM)),
            in_hbm,
            in_hbm,
            in_hbm,
            _resident((1, D_MODEL)),
        ],
        out_specs=row_block,
        out_shape=jax.ShapeDtypeStruct(x.shape, x.dtype),
        scratch_shapes=[
            pltpu.VMEM((D_MODEL, OFF_LR), _BF16),
            pltpu.VMEM((D_MODEL, LR_PAD), _BF16),
            pltpu.VMEM((D_MODEL, 2 * D_MODEL), _BF16),
            pltpu.VMEM((SB_WIDTH, D_MODEL), _BF16),
            pltpu.VMEM((GLA_V_WIDTH, D_MODEL), _BF16),
            pltpu.VMEM((D_MODEL, D_MODEL), _BF16),
            pltpu.VMEM((WEIGHT_BUFFERS, WEIGHT_CHUNK_ROWS, D_MODEL), _F32),
            pltpu.SemaphoreType.DMA((WEIGHT_BUFFERS,)),
            pltpu.VMEM((seq + TQ, SB_WIDTH), _BF16),
            pltpu.VMEM((2, seq + TQ, SB_WIDTH), _BF16),
            pltpu.VMEM((GLA_HEADS // 2, GLA_VAL_DIM, LANES), _F32),
            pltpu.VMEM((SB_PAIRS, 2 * TQ, LANES), _BF16),
            pltpu.VMEM((SB_PAIRS, 2 * TQ, LANES), _F32),
            pltpu.VMEM((SB_PAIRS, TQ, LANES), _F32),
        ],
        compiler_params=pltpu.CompilerParams(
            dimension_semantics=("arbitrary", "arbitrary"), vmem_limit_bytes=VMEM_LIMIT_BYTES),
        name="hybrid_mixer_block",
    )(x, g_pre, w_in_t, b_gate, w_up, b_alpha_up, g_gla_norm, w_o_sb, w_o_gla, w_out, g_final.reshape(1, D_MODEL))
```

```python
import math

import jax
import jax.numpy as jnp
from jax import lax
from jax.experimental import pallas as pl
from jax.experimental.pallas import tpu as pltpu

D_MODEL = 1024
CHUNK = 64
SB_HEAD_DIM = 64
SB_WIDTH = D_MODEL // 2
GLA_HEADS = 4
GLA_K_WIDTH = D_MODEL // 4
GLA_V_WIDTH = D_MODEL // 2
GLA_KEY_DIM = GLA_K_WIDTH // GLA_HEADS
GLA_VAL_DIM = GLA_V_WIDTH // GLA_HEADS
GLA_RANK = 16
GLA_GATE_TEMP = 16.0
EPS = 1e-6
LOG2_E = math.log2(math.e)
EXP2_CLAMP = 126.0
SB_DEAD_LOG2 = 151.0

LANES = 128
SB_PAIRS = SB_WIDTH // LANES
SB_GROUP = 2
TQ = 256
N_CHUNKS = TQ // CHUNK
LR_PAD = LANES
WEIGHT_CHUNK_ROWS = 512
WEIGHT_BUFFERS = 3

OFF_SB_Q = 0
OFF_SB_K = OFF_SB_Q + SB_WIDTH
OFF_SB_V = OFF_SB_K + SB_WIDTH
OFF_SB_Z = OFF_SB_V + SB_WIDTH
OFF_GLA_Q = OFF_SB_Z + SB_WIDTH
OFF_GLA_K = OFF_GLA_Q + GLA_K_WIDTH
OFF_GLA_V = OFF_GLA_K + GLA_K_WIDTH
OFF_GLA_Z = OFF_GLA_V + GLA_V_WIDTH
OFF_LR = OFF_GLA_Z + GLA_V_WIDTH

VMEM_LIMIT_BYTES = 56 * 1024 * 1024

_F32 = jnp.float32
_BF16 = jnp.bfloat16
_NT = (((1,), (1,)), ((), ()))
_TN = (((0,), (0,)), ((), ()))


def _dot(a, b):
    return jnp.dot(a, b, preferred_element_type=_F32)


def _dot_nt(a, b):
    return lax.dot_general(a, b, _NT, preferred_element_type=_F32)


def _log_sigmoid(x):
    return jnp.minimum(x, 0.0) - jnp.log(1.0 + jnp.exp(-jnp.abs(x)))


def _silu(x):
    return x * jax.nn.sigmoid(x)


def _rms(x, gain):
    return x * lax.rsqrt(jnp.mean(x * x, axis=-1, keepdims=True) + EPS) * gain


def _stack_heads(x2):
    first_head = lax.broadcasted_iota(jnp.int32, (1, LANES), 1) < SB_HEAD_DIM
    return jnp.concatenate([jnp.where(first_head, x2, 0), jnp.where(first_head, 0, x2)], axis=0)


def _sb_neg_log2_miss(zz):
    return jnp.maximum(jnp.log(1.0 + jnp.exp2(jnp.minimum(zz, EXP2_CLAMP))) * LOG2_E, zz)


def _sb_incl():
    row = lax.broadcasted_iota(jnp.int32, (TQ, TQ), 0)
    col = lax.broadcasted_iota(jnp.int32, (TQ, TQ), 1)
    return (row >= col).astype(_BF16)


def _sb_near_blocks(q, k_all, vm_all, step, qs_ref, nc_ref, acc_ref):
    n_heads = 2 * SB_GROUP
    width = SB_GROUP * LANES
    row = lax.broadcasted_iota(jnp.int32, (n_heads * TQ, TQ), 0) & (TQ - 1)
    col = lax.broadcasted_iota(jnp.int32, (n_heads * TQ, TQ), 1)
    causal = col < row
    incl = _sb_incl()
    keys = pl.ds(pl.multiple_of(step * TQ, TQ), 2 * TQ)
    head_of_lane = lax.broadcasted_iota(jnp.int32, (1, width), 1) // SB_HEAD_DIM
    zero_v = jnp.zeros((2 * TQ, LANES), _BF16)
    for group in range(SB_PAIRS // SB_GROUP):
        cols = slice(group * width, (group + 1) * width)
        q_group = q[:, cols]
        qs = jnp.concatenate([jnp.where(head_of_lane == hd, q_group, 0) for hd in range(n_heads)], axis=0)
        zz = _dot_nt(qs, k_all[keys, cols])
        zz_prev, zz_own = zz[:, :TQ], zz[:, TQ:]
        nl_prev = _sb_neg_log2_miss(zz_prev)
        nl_own = jnp.where(causal, _sb_neg_log2_miss(zz_own), 0.0)
        tot_own = _dot(nl_own.astype(_BF16), incl)
        tot_prev = _dot(nl_prev.astype(_BF16), incl) + tot_own[:, 0:1]
        w_prev = jnp.exp2(zz_prev - tot_prev)
        w_own = jnp.where(causal, jnp.exp2(zz_own - tot_own), 0.0)
        carried = jnp.broadcast_to(tot_prev[:, 0:1], (n_heads * TQ, LANES))
        w_cat = jnp.concatenate(
            [blk[hd * TQ:(hd + 1) * TQ] for hd in range(n_heads) for blk in (w_prev, w_own)], axis=1).astype(_BF16)
        v_cat = jnp.concatenate([
            jnp.concatenate([vm_all[hd % 2, keys, (group * SB_GROUP + p) * LANES:(group * SB_GROUP + p + 1) * LANES]
                             if p == hd // 2 else zero_v for p in range(SB_GROUP)], axis=1)
            for hd in range(n_heads)], axis=0)
        out = _dot(w_cat, v_cat)
        for p in range(SB_GROUP):
            pair = group * SB_GROUP + p
            heads = slice(2 * p * TQ, (2 * p + 2) * TQ)
            qs_ref[pair] = qs[heads, p * LANES:(p + 1) * LANES]
            nc_ref[pair] = carried[heads]
            acc_ref[pair] = out[:, p * LANES:(p + 1) * LANES]


def _sb_far_blocks(k_all, vm_all, step, qs_ref, nc_ref, acc_ref):
    def more_blocks(carry):
        t, live = carry
        return jnp.logical_and(t < step - 1, live)

    def earlier_block(carry):
        t, _ = carry
        keys = pl.ds(pl.multiple_of((step - 1 - t) * TQ, TQ), TQ)
        incl = _sb_incl()
        for pair in range(SB_PAIRS):
            cols = slice(pair * LANES, (pair + 1) * LANES)
            zz = _dot_nt(qs_ref[pair], k_all[keys, cols])
            nl = _sb_neg_log2_miss(zz)
            tot = _dot(nl.astype(_BF16), incl) + jnp.concatenate([nc_ref[pair]] * (TQ // LANES), axis=1)
            w = jnp.exp2(zz - tot)
            w_cat = jnp.concatenate([w[:TQ], w[TQ:]], axis=1).astype(_BF16)
            v_cat = jnp.concatenate([vm_all[0, keys, cols], vm_all[1, keys, cols]], axis=0)
            acc_ref[pair] += _dot(w_cat, v_cat)
            nc_ref[pair] = jnp.broadcast_to(tot[:, 0:1], (2 * TQ, LANES))
        return t + 1, jnp.min(nc_ref[...]) < SB_DEAD_LOG2

    lax.while_loop(more_blocks, earlier_block, (jnp.int32(0), jnp.min(nc_ref[...]) < SB_DEAD_LOG2))


def _chunk_block_diag(x):
    zero = jnp.zeros((CHUNK, LANES), x.dtype)
    out_rows = []
    for r in range(x.shape[0] // CHUNK):
        blocks = [x[r * CHUNK:(r + 1) * CHUNK] if c == r % N_CHUNKS else zero for c in range(N_CHUNKS)]
        out_rows.append(jnp.concatenate(blocks, axis=1))
    return jnp.concatenate(out_rows, axis=0)


def _gla(q, k, v, log_alpha, state_ref, gain):
    row = lax.broadcasted_iota(jnp.int32, (TQ, TQ), 0)
    col = lax.broadcasted_iota(jnp.int32, (TQ, TQ), 1)
    prefix = jnp.logical_and(row // CHUNK == col // CHUNK, col <= row).astype(_BF16)
    la_hi = log_alpha.astype(_BF16)
    la_lo = (log_alpha - la_hi.astype(_F32)).astype(_BF16)
    cum = _dot(prefix, la_hi) + _dot(prefix, la_lo)
    chunk_total = [cum[(c + 1) * CHUNK - 1:(c + 1) * CHUNK] for c in range(N_CHUNKS)]
    cum_last = jnp.concatenate([jnp.broadcast_to(t, (CHUNK, GLA_K_WIDTH)) for t in chunk_total], axis=0)
    q_dec = (q * jnp.exp(cum) * (GLA_KEY_DIM ** -0.5)).astype(_BF16)
    k_inv = (k * jnp.exp(-cum)).astype(_BF16)
    k_end = (k * jnp.exp(cum_last - cum)).astype(_BF16)
    decay = [jnp.exp(t) for t in chunk_total]
    v_bf = v.astype(_BF16)

    row2 = lax.broadcasted_iota(jnp.int32, (2 * TQ, TQ), 0) & (TQ - 1)
    col2 = lax.broadcasted_iota(jnp.int32, (2 * TQ, TQ), 1)
    causal2 = jnp.logical_and(row2 // CHUNK == col2 // CHUNK, col2 <= row2)
    first_head = lax.broadcasted_iota(jnp.int32, (1, LANES), 1) < GLA_KEY_DIM
    outs = []
    for pair in range(GLA_HEADS // 2):
        cols = slice(pair * LANES, (pair + 1) * LANES)
        v_pair = v_bf[:, 2 * pair * GLA_VAL_DIM:(2 * pair + 2) * GLA_VAL_DIM]
        qs = _stack_heads(q_dec[:, cols])
        scores = jnp.where(causal2, _dot_nt(qs, k_inv[:, cols]), 0.0).astype(_BF16)
        intra = (_dot(scores[:TQ], v_pair[:, :GLA_VAL_DIM]), _dot(scores[TQ:], v_pair[:, GLA_VAL_DIM:]))
        kv_all = lax.dot_general(v_pair, _chunk_block_diag(k_end[:, cols]), _TN, preferred_element_type=_F32)
        states = [state_ref[pair]]
        for c in range(N_CHUNKS):
            blk = slice(c * LANES, (c + 1) * LANES)
            kv = jnp.where(first_head, kv_all[:GLA_VAL_DIM, blk], kv_all[GLA_VAL_DIM:, blk])
            states.append(states[-1] * decay[c][:, cols] + kv)
        state_ref[pair] = states[-1]
        state_cat = jnp.concatenate(states[:N_CHUNKS], axis=1).astype(_BF16)
        inter = _dot_nt(_chunk_block_diag(qs), state_cat)
        outs.append(_rms(intra[0] + inter[:TQ], gain))
        outs.append(_rms(intra[1] + inter[TQ:], gain))
    return jnp.concatenate(outs, axis=1)


def _load_weights(w_in_t_hbm, w_up_ref, w_o_sb_hbm, w_o_gla_hbm, w_out_hbm, w_in_ref, w_alpha_ref, w_gate_ref,
                  w_o_sb_ref, w_o_gla_ref, w_out_ref, stage_ref, sem_ref):
    rows = WEIGHT_CHUNK_ROWS
    jobs = []

    def into_columns(dst, block):
        def consume(chunk):
            dst[:, block * rows:(block + 1) * rows] = chunk.T.astype(_BF16)
        return consume

    def into_rows(dst, block):
        def consume(chunk):
            dst[block * rows:(block + 1) * rows, :] = chunk.astype(_BF16)
        return consume

    def low_rank(chunk):
        lane = lax.broadcasted_iota(jnp.int32, (1, LR_PAD), 1)
        w_lr = jnp.where(lane < GLA_RANK, chunk[:LR_PAD].T, 0.0)
        w_alpha_ref[...] = jnp.dot(w_lr, w_up_ref[...], precision=lax.Precision.HIGHEST,
                                   preferred_element_type=_F32).astype(_BF16)

    w_in_t = w_in_t_hbm.at[0]
    for block in range(OFF_LR // rows):
        jobs.append((w_in_t, block * rows, into_columns(w_in_ref, block)))
    jobs.append((w_in_t, OFF_LR, low_rank))
    for block in range(2 * D_MODEL // rows):
        jobs.append((w_in_t, OFF_LR + GLA_RANK + block * rows, into_columns(w_gate_ref, block)))
    for src, dst in ((w_o_sb_hbm, w_o_sb_ref), (w_o_gla_hbm, w_o_gla_ref), (w_out_hbm, w_out_ref)):
        for block in range(dst.shape[0] // rows):
            jobs.append((src.at[0], block * rows, into_rows(dst, block)))

    n_buf = stage_ref.shape[0]

    def copy(j):
        src, row0, _ = jobs[j]
        return pltpu.make_async_copy(src.at[pl.ds(row0, rows), :], stage_ref.at[j % n_buf], sem_ref.at[j % n_buf])

    for j in range(min(n_buf - 1, len(jobs))):
        copy(j).start()
    for j in range(len(jobs)):
        if j + n_buf - 1 < len(jobs):
            copy(j + n_buf - 1).start()
        copy(j).wait()
        jobs[j][2](stage_ref[j % n_buf])


def _block_kernel(x_ref, g_pre_ref, w_in_t_hbm, b_gate_ref, w_up_ref, b_up_ref, g_gla_ref,
                  w_o_sb_hbm, w_o_gla_hbm, w_out_hbm, g_final_ref, out_ref,
                  w_in_ref, w_alpha_ref, w_gate_ref, w_o_sb_ref, w_o_gla_ref, w_out_ref, stage_ref, sem_ref,
                  k_all, vm_all, state_ref, qs_ref, nc_ref, acc_ref):
    step = pl.program_id(1)

    @pl.when(jnp.logical_and(pl.program_id(0) == 0, step == 0))
    def _():
        _load_weights(w_in_t_hbm, w_up_ref, w_o_sb_hbm, w_o_gla_hbm, w_out_hbm, w_in_ref, w_alpha_ref, w_gate_ref,
                      w_o_sb_ref, w_o_gla_ref, w_out_ref, stage_ref, sem_ref)

    @pl.when(step == 0)
    def _():
        state_ref[...] = jnp.zeros_like(state_ref)
        k_all[0:TQ, :] = jnp.zeros((TQ, SB_WIDTH), _BF16)
        vm_all[:, 0:TQ, :] = jnp.zeros((2, TQ, SB_WIDTH), _BF16)

    x = x_ref[0]
    h = (x * g_pre_ref[...]).astype(_BF16)
    inv_rms = jnp.broadcast_to(lax.rsqrt(jnp.mean(x * x, axis=-1, keepdims=True) + EPS), (TQ, LANES))

    def scale_rows(p):
        return p * jnp.concatenate([inv_rms] * (p.shape[1] // LANES), axis=1)

    def proj(offset, width):
        return scale_rows(_dot(h, w_in_ref[:, offset:offset + width]))

    rows = pl.ds(pl.multiple_of((step + 1) * TQ, TQ), TQ)
    k_all[rows, :] = proj(OFF_SB_K, SB_WIDTH).astype(_BF16)
    sb_v = proj(OFF_SB_V, SB_WIDTH).astype(_BF16)
    first_head = (lax.broadcasted_iota(jnp.int32, (1, SB_WIDTH), 1) & (LANES - 1)) < SB_HEAD_DIM
    vm_all[0, rows, :] = jnp.where(first_head, sb_v, 0)
    vm_all[1, rows, :] = jnp.where(first_head, 0, sb_v)
    sb_q = (proj(OFF_SB_Q, SB_WIDTH) * (LOG2_E * SB_HEAD_DIM ** -0.5)).astype(_BF16)
    _sb_near_blocks(sb_q, k_all, vm_all, step, qs_ref, nc_ref, acc_ref)

    alpha_logit = scale_rows(_dot(h, w_alpha_ref[...])) + b_up_ref[...]
    log_alpha = _log_sigmoid(alpha_logit) / GLA_GATE_TEMP
    o_gla = _gla(proj(OFF_GLA_Q, GLA_K_WIDTH), proj(OFF_GLA_K, GLA_K_WIDTH), proj(OFF_GLA_V, GLA_V_WIDTH),
                 log_alpha, state_ref, g_gla_ref[...])
    y_gla = _dot((o_gla * _silu(proj(OFF_GLA_Z, GLA_V_WIDTH))).astype(_BF16), w_o_gla_ref[...])
    gates = jax.nn.sigmoid(scale_rows(_dot(h, w_gate_ref[...])) + b_gate_ref[...])
    gated_gla = gates[:, D_MODEL:] * y_gla
    gate_sb = gates[:, :D_MODEL]
    sb_zs = _silu(proj(OFF_SB_Z, SB_WIDTH))

    def finish():
        o_sb = jnp.concatenate([acc_ref[pair] for pair in range(SB_PAIRS)], axis=1)
        y_sb = _dot((o_sb * sb_zs).astype(_BF16), w_o_sb_ref[...])
        merged = gate_sb * y_sb + gated_gla
        res = x + _dot(merged.astype(_BF16), w_out_ref[...])
        out_ref[0] = _rms(res, g_final_ref[...])

    finish()

    @pl.when(jnp.logical_and(step >= 2, jnp.min(nc_ref[...]) < SB_DEAD_LOG2))
    def _():
        _sb_far_blocks(k_all, vm_all, step, qs_ref, nc_ref, acc_ref)
        finish()


def _resident(shape):
    return pl.BlockSpec(shape, lambda b, i: (0,) * len(shape), pipeline_mode=pl.Buffered(1))


@jax.jit
def kernel(x, g_pre, w_in, b_gate, w_alpha_up, b_alpha_up, g_gla_norm, w_o_sb, w_o_gla, w_out, g_final):
    batch, seq, d_model = x.shape
    depth = w_in.shape[0]
    assert d_model == D_MODEL and seq % TQ == 0 and depth == 1
    w_up = jnp.pad(w_alpha_up[0], ((0, LR_PAD - GLA_RANK), (0, 0)))
    w_in_t = jnp.transpose(w_in, (0, 2, 1))

    row_block = pl.BlockSpec((1, TQ, D_MODEL), lambda b, i: (b, i, 0))
    in_hbm = pl.BlockSpec(memory_space=pl.ANY)
    return pl.pallas_call(
        _block_kernel,
        grid=(batch, seq // TQ),
        in_specs=[
            row_block,
            _resident((1, D_MODEL)),
            in_hbm,
            _resident((1, 2 * D_MODEL)),
            _resident((LR_PAD, GLA_K_WIDTH)),
            _resident((1, GLA_K_WIDTH)),
            _resident((1, GLA_VAL_DIM)),
            in_hbm,
            in_hbm,
            in_hbm,
            _resident((1, D_MODEL)),
        ],
        out_specs=row_block,
        out_shape=jax.ShapeDtypeStruct(x.shape, x.dtype),
        scratch_shapes=[
            pltpu.VMEM((D_MODEL, OFF_LR), _BF16),
            pltpu.VMEM((D_MODEL, GLA_K_WIDTH), _BF16),
            pltpu.VMEM((D_MODEL, 2 * D_MODEL), _BF16),
            pltpu.VMEM((SB_WIDTH, D_MODEL), _BF16),
            pltpu.VMEM((GLA_V_WIDTH, D_MODEL), _BF16),
            pltpu.VMEM((D_MODEL, D_MODEL), _BF16),
            pltpu.VMEM((WEIGHT_BUFFERS, WEIGHT_CHUNK_ROWS, D_MODEL), _F32),
            pltpu.SemaphoreType.DMA((WEIGHT_BUFFERS,)),
            pltpu.VMEM((seq + TQ, SB_WIDTH), _BF16),
            pltpu.VMEM((2, seq + TQ, SB_WIDTH), _BF16),
            pltpu.VMEM((GLA_HEADS // 2, GLA_VAL_DIM, LANES), _F32),
            pltpu.VMEM((SB_PAIRS, 2 * TQ, LANES), _BF16),
            pltpu.VMEM((SB_PAIRS, 2 * TQ, LANES), _F32),
            pltpu.VMEM((SB_PAIRS, TQ, LANES), _F32),
        ],
        compiler_params=pltpu.CompilerParams(
            dimension_semantics=("arbitrary", "arbitrary"), vmem_limit_bytes=VMEM_LIMIT_BYTES),
        name="hybrid_mixer_block",
    )(x, g_pre, w_in_t, b_gate, w_up, b_alpha_up, g_gla_norm, w_o_sb, w_o_gla, w_out, g_final.reshape(1, D_MODEL))
```

```python
import math

import jax
import jax.numpy as jnp
from jax import lax
from jax.experimental import pallas as pl
from jax.experimental.pallas import tpu as pltpu

D_MODEL = 1024
CHUNK = 64
SB_HEAD_DIM = 64
SB_WIDTH = D_MODEL // 2
GLA_HEADS = 4
GLA_K_WIDTH = D_MODEL // 4
GLA_V_WIDTH = D_MODEL // 2
GLA_KEY_DIM = GLA_K_WIDTH // GLA_HEADS
GLA_VAL_DIM = GLA_V_WIDTH // GLA_HEADS
GLA_RANK = 16
GLA_GATE_TEMP = 16.0
EPS = 1e-6
LOG2_E = math.log2(math.e)
EXP2_CLAMP = 126.0
SB_DEAD_LOG2 = 151.0

LANES = 128
SUBLANES = 8
SB_PAIRS = SB_WIDTH // LANES
SB_GROUP = 2
TQ = 256
N_CHUNKS = TQ // CHUNK
LR_PAD = LANES
WEIGHT_CHUNK_ROWS = 512
WEIGHT_BUFFERS = 3

OFF_SB_Q = 0
OFF_SB_K = OFF_SB_Q + SB_WIDTH
OFF_SB_V = OFF_SB_K + SB_WIDTH
OFF_SB_Z = OFF_SB_V + SB_WIDTH
OFF_GLA_Q = OFF_SB_Z + SB_WIDTH
OFF_GLA_K = OFF_GLA_Q + GLA_K_WIDTH
OFF_GLA_V = OFF_GLA_K + GLA_K_WIDTH
OFF_GLA_Z = OFF_GLA_V + GLA_V_WIDTH
OFF_LR = OFF_GLA_Z + GLA_V_WIDTH

VMEM_LIMIT_BYTES = 56 * 1024 * 1024

_F32 = jnp.float32
_BF16 = jnp.bfloat16
_NT = (((1,), (1,)), ((), ()))
_TN = (((0,), (0,)), ((), ()))


def _dot(a, b):
    return jnp.dot(a, b, preferred_element_type=_F32)


def _dot_nt(a, b):
    return lax.dot_general(a, b, _NT, preferred_element_type=_F32)


def _log_sigmoid(x):
    return jnp.minimum(x, 0.0) - jnp.log(1.0 + jnp.exp(-jnp.abs(x)))


def _silu(x):
    return x * jax.nn.sigmoid(x)


def _rms(x, gain):
    return x * lax.rsqrt(jnp.mean(x * x, axis=-1, keepdims=True) + EPS) * gain


def _sb_neg_log2_miss(zz):
    return jnp.maximum(jnp.log(1.0 + jnp.exp2(jnp.minimum(zz, EXP2_CLAMP))) * LOG2_E, zz)


def _sb_incl():
    row = lax.broadcasted_iota(jnp.int32, (TQ, TQ), 0)
    col = lax.broadcasted_iota(jnp.int32, (TQ, TQ), 1)
    return (row >= col).astype(_BF16)


def _sb_near_blocks(q, k_all, vm_all, step, qs_ref, nc_ref, acc_ref):
    n_heads = 2 * SB_GROUP
    width = SB_GROUP * LANES
    row = lax.broadcasted_iota(jnp.int32, (n_heads * TQ, TQ), 0) & (TQ - 1)
    col = lax.broadcasted_iota(jnp.int32, (n_heads * TQ, TQ), 1)
    causal = col < row
    incl = _sb_incl()
    keys = pl.ds(pl.multiple_of(step * TQ, TQ), 2 * TQ)
    head_of_lane = lax.broadcasted_iota(jnp.int32, (1, width), 1) // SB_HEAD_DIM
    zero_v = jnp.zeros((2 * TQ, LANES), _BF16)
    for group in range(SB_PAIRS // SB_GROUP):
        cols = slice(group * width, (group + 1) * width)
        q_group = q[:, cols]
        qs = jnp.concatenate([jnp.where(head_of_lane == hd, q_group, 0) for hd in range(n_heads)], axis=0)
        zz = _dot_nt(qs, k_all[keys, cols])
        zz_prev, zz_own = zz[:, :TQ], zz[:, TQ:]
        nl_prev = _sb_neg_log2_miss(zz_prev)
        nl_own = jnp.where(causal, _sb_neg_log2_miss(zz_own), 0.0)
        tot_own = _dot(nl_own.astype(_BF16), incl)
        tot_prev = _dot(nl_prev.astype(_BF16), incl) + tot_own[:, 0:1]
        w_prev = jnp.exp2(zz_prev - tot_prev)
        w_own = jnp.where(causal, jnp.exp2(zz_own - tot_own), 0.0)
        carried = jnp.broadcast_to(tot_prev[:, 0:1], (n_heads * TQ, LANES))
        w_cat = jnp.concatenate(
            [blk[hd * TQ:(hd + 1) * TQ] for hd in range(n_heads) for blk in (w_prev, w_own)], axis=1).astype(_BF16)
        v_cat = jnp.concatenate([
            jnp.concatenate([vm_all[hd % 2, keys, (group * SB_GROUP + p) * LANES:(group * SB_GROUP + p + 1) * LANES]
                             if p == hd // 2 else zero_v for p in range(SB_GROUP)], axis=1)
            for hd in range(n_heads)], axis=0)
        out = _dot(w_cat, v_cat)
        for p in range(SB_GROUP):
            pair = group * SB_GROUP + p
            heads = slice(2 * p * TQ, (2 * p + 2) * TQ)
            qs_ref[pair] = qs[heads, p * LANES:(p + 1) * LANES]
            nc_ref[pair] = carried[heads]
            acc_ref[pair] = out[:, p * LANES:(p + 1) * LANES]


def _sb_far_blocks(k_all, vm_all, step, qs_ref, nc_ref, acc_ref):
    def more_blocks(carry):
        t, live = carry
        return jnp.logical_and(t < step - 1, live)

    def earlier_block(carry):
        t, _ = carry
        keys = pl.ds(pl.multiple_of((step - 1 - t) * TQ, TQ), TQ)
        incl = _sb_incl()
        for pair in range(SB_PAIRS):
            cols = slice(pair * LANES, (pair + 1) * LANES)
            zz = _dot_nt(qs_ref[pair], k_all[keys, cols])
            nl = _sb_neg_log2_miss(zz)
            tot = _dot(nl.astype(_BF16), incl) + jnp.concatenate([nc_ref[pair]] * (TQ // LANES), axis=1)
            w = jnp.exp2(zz - tot)
            w_cat = jnp.concatenate([w[:TQ], w[TQ:]], axis=1).astype(_BF16)
            v_cat = jnp.concatenate([vm_all[0, keys, cols], vm_all[1, keys, cols]], axis=0)
            acc_ref[pair] += _dot(w_cat, v_cat)
            nc_ref[pair] = jnp.broadcast_to(tot[:, 0:1], (2 * TQ, LANES))
        return t + 1, jnp.min(nc_ref[...]) < SB_DEAD_LOG2

    lax.while_loop(more_blocks, earlier_block, (jnp.int32(0), jnp.min(nc_ref[...]) < SB_DEAD_LOG2))


def _chunk_block_diag(x):
    zero = jnp.zeros((CHUNK, LANES), x.dtype)
    out_rows = []
    for r in range(x.shape[0] // CHUNK):
        blocks = [x[r * CHUNK:(r + 1) * CHUNK] if c == r % N_CHUNKS else zero for c in range(N_CHUNKS)]
        out_rows.append(jnp.concatenate(blocks, axis=1))
    return jnp.concatenate(out_rows, axis=0)


def _chunk_prefix_sum(x):
    sub = lax.broadcasted_iota(jnp.int32, (TQ, 1), 0) & (SUBLANES - 1)
    for shift in (1, 2, 4):
        x = x + jnp.where(sub >= shift, pltpu.roll(x, shift, axis=0), 0.0)
    groups = []
    for g in range(TQ // SUBLANES):
        blk = x[g * SUBLANES:(g + 1) * SUBLANES]
        if g % (CHUNK // SUBLANES):
            blk = blk + jnp.broadcast_to(groups[-1][SUBLANES - 1:], blk.shape)
        groups.append(blk)
    return jnp.concatenate(groups, axis=0)


def _gla(q, k, v, log_alpha, state_ref, gain):
    cum = _chunk_prefix_sum(log_alpha)
    chunk_total = [cum[(c + 1) * CHUNK - 1:(c + 1) * CHUNK] for c in range(N_CHUNKS)]
    cum_last = jnp.concatenate([jnp.broadcast_to(t, (CHUNK, GLA_K_WIDTH)) for t in chunk_total], axis=0)
    q_dec = (q * jnp.exp(cum) * (GLA_KEY_DIM ** -0.5)).astype(_BF16)
    k_inv = (k * jnp.exp(-cum)).astype(_BF16)
    k_end = (k * jnp.exp(cum_last - cum)).astype(_BF16)
    decay = [jnp.exp(t) for t in chunk_total]
    v_bf = v.astype(_BF16)

    row_all = lax.broadcasted_iota(jnp.int32, (GLA_HEADS * TQ, TQ), 0) & (TQ - 1)
    col_all = lax.broadcasted_iota(jnp.int32, (GLA_HEADS * TQ, TQ), 1)
    causal_all = jnp.logical_and(row_all // CHUNK == col_all // CHUNK, col_all <= row_all)
    head_of_lane = lax.broadcasted_iota(jnp.int32, (1, GLA_K_WIDTH), 1) // GLA_KEY_DIM
    qs_all = jnp.concatenate([jnp.where(head_of_lane == hd, q_dec, 0) for hd in range(GLA_HEADS)], axis=0)
    scores_all = jnp.where(causal_all, _dot_nt(qs_all, k_inv), 0.0).astype(_BF16)
    first_head = lax.broadcasted_iota(jnp.int32, (1, LANES), 1) < GLA_KEY_DIM
    zero_v = jnp.zeros((TQ, GLA_VAL_DIM), _BF16)
    outs = []
    for pair in range(GLA_HEADS // 2):
        cols = slice(pair * LANES, (pair + 1) * LANES)
        v_pair = v_bf[:, 2 * pair * GLA_VAL_DIM:(2 * pair + 2) * GLA_VAL_DIM]
        qs = qs_all[2 * pair * TQ:(2 * pair + 2) * TQ, cols]
        scores = jnp.concatenate([scores_all[2 * pair * TQ:(2 * pair + 1) * TQ],
                                  scores_all[(2 * pair + 1) * TQ:(2 * pair + 2) * TQ]], axis=1)
        v_diag = jnp.concatenate([jnp.concatenate([v_pair[:, :GLA_VAL_DIM], zero_v], axis=1),
                                  jnp.concatenate([zero_v, v_pair[:, GLA_VAL_DIM:]], axis=1)], axis=0)
        intra_pair = _dot(scores, v_diag)
        intra = (intra_pair[:, :GLA_VAL_DIM], intra_pair[:, GLA_VAL_DIM:])
        kv_all = lax.dot_general(v_pair, _chunk_block_diag(k_end[:, cols]), _TN, preferred_element_type=_F32)
        states = [state_ref[pair]]
        for c in range(N_CHUNKS):
            blk = slice(c * LANES, (c + 1) * LANES)
            kv = jnp.where(first_head, kv_all[:GLA_VAL_DIM, blk], kv_all[GLA_VAL_DIM:, blk])
            states.append(states[-1] * decay[c][:, cols] + kv)
        state_ref[pair] = states[-1]
        state_cat = jnp.concatenate(states[:N_CHUNKS], axis=1).astype(_BF16)
        inter = _dot_nt(_chunk_block_diag(qs), state_cat)
        outs.append(_rms(intra[0] + inter[:TQ], gain))
        outs.append(_rms(intra[1] + inter[TQ:], gain))
    return jnp.concatenate(outs, axis=1)


def _load_weights(w_in_t_hbm, w_up_ref, w_o_sb_hbm, w_o_gla_hbm, w_out_hbm, w_in_ref, w_alpha_ref, w_gate_ref,
                  w_o_sb_ref, w_o_gla_ref, w_out_ref, stage_ref, sem_ref):
    rows = WEIGHT_CHUNK_ROWS
    jobs = []

    def into_columns(dst, block):
        def consume(chunk):
            dst[:, block * rows:(block + 1) * rows] = chunk.T.astype(_BF16)
        return consume

    def into_rows(dst, block):
        def consume(chunk):
            dst[block * rows:(block + 1) * rows, :] = chunk.astype(_BF16)
        return consume

    def low_rank(chunk):
        lane = lax.broadcasted_iota(jnp.int32, (1, LR_PAD), 1)
        w_lr = jnp.where(lane < GLA_RANK, chunk[:LR_PAD].T, 0.0)
        w_alpha_ref[...] = jnp.dot(w_lr, w_up_ref[...], precision=lax.Precision.HIGHEST,
                                   preferred_element_type=_F32).astype(_BF16)

    w_in_t = w_in_t_hbm.at[0]
    for block in range(OFF_LR // rows):
        jobs.append((w_in_t, block * rows, into_columns(w_in_ref, block)))
    jobs.append((w_in_t, OFF_LR, low_rank))
    for block in range(2 * D_MODEL // rows):
        jobs.append((w_in_t, OFF_LR + GLA_RANK + block * rows, into_columns(w_gate_ref, block)))
    for src, dst in ((w_o_sb_hbm, w_o_sb_ref), (w_o_gla_hbm, w_o_gla_ref), (w_out_hbm, w_out_ref)):
        for block in range(dst.shape[0] // rows):
            jobs.append((src.at[0], block * rows, into_rows(dst, block)))

    n_buf = stage_ref.shape[0]

    def copy(j):
        src, row0, _ = jobs[j]
        return pltpu.make_async_copy(src.at[pl.ds(row0, rows), :], stage_ref.at[j % n_buf], sem_ref.at[j % n_buf])

    for j in range(min(n_buf - 1, len(jobs))):
        copy(j).start()
    for j in range(len(jobs)):
        if j + n_buf - 1 < len(jobs):
            copy(j + n_buf - 1).start()
        copy(j).wait()
        jobs[j][2](stage_ref[j % n_buf])


def _block_kernel(x_ref, g_pre_ref, w_in_t_hbm, b_gate_ref, w_up_ref, b_up_ref, g_gla_ref,
                  w_o_sb_hbm, w_o_gla_hbm, w_out_hbm, g_final_ref, out_ref,
                  w_in_ref, w_alpha_ref, w_gate_ref, w_o_sb_ref, w_o_gla_ref, w_out_ref, stage_ref, sem_ref,
                  k_all, vm_all, state_ref, qs_ref, nc_ref, acc_ref):
    step = pl.program_id(1)

    @pl.when(jnp.logical_and(pl.program_id(0) == 0, step == 0))
    def _():
        _load_weights(w_in_t_hbm, w_up_ref, w_o_sb_hbm, w_o_gla_hbm, w_out_hbm, w_in_ref, w_alpha_ref, w_gate_ref,
                      w_o_sb_ref, w_o_gla_ref, w_out_ref, stage_ref, sem_ref)

    @pl.when(step == 0)
    def _():
        state_ref[...] = jnp.zeros_like(state_ref)
        k_all[0:TQ, :] = jnp.zeros((TQ, SB_WIDTH), _BF16)
        vm_all[:, 0:TQ, :] = jnp.zeros((2, TQ, SB_WIDTH), _BF16)

    x = x_ref[0]
    h = (x * g_pre_ref[...]).astype(_BF16)
    inv_rms = jnp.broadcast_to(lax.rsqrt(jnp.mean(x * x, axis=-1, keepdims=True) + EPS), (TQ, LANES))

    def scale_rows(p):
        return p * jnp.concatenate([inv_rms] * (p.shape[1] // LANES), axis=1)

    def proj(offset, width):
        return scale_rows(_dot(h, w_in_ref[:, offset:offset + width]))

    rows = pl.ds(pl.multiple_of((step + 1) * TQ, TQ), TQ)
    k_all[rows, :] = proj(OFF_SB_K, SB_WIDTH).astype(_BF16)
    sb_v = proj(OFF_SB_V, SB_WIDTH).astype(_BF16)
    first_head = (lax.broadcasted_iota(jnp.int32, (1, SB_WIDTH), 1) & (LANES - 1)) < SB_HEAD_DIM
    vm_all[0, rows, :] = jnp.where(first_head, sb_v, 0)
    vm_all[1, rows, :] = jnp.where(first_head, 0, sb_v)
    sb_q = (proj(OFF_SB_Q, SB_WIDTH) * (LOG2_E * SB_HEAD_DIM ** -0.5)).astype(_BF16)
    _sb_near_blocks(sb_q, k_all, vm_all, step, qs_ref, nc_ref, acc_ref)

    alpha_logit = scale_rows(_dot(h, w_alpha_ref[...])) + b_up_ref[...]
    log_alpha = _log_sigmoid(alpha_logit) / GLA_GATE_TEMP
    o_gla = _gla(proj(OFF_GLA_Q, GLA_K_WIDTH), proj(OFF_GLA_K, GLA_K_WIDTH), proj(OFF_GLA_V, GLA_V_WIDTH),
                 log_alpha, state_ref, g_gla_ref[...])
    y_gla = _dot((o_gla * _silu(proj(OFF_GLA_Z, GLA_V_WIDTH))).astype(_BF16), w_o_gla_ref[...])
    gates = jax.nn.sigmoid(scale_rows(_dot(h, w_gate_ref[...])) + b_gate_ref[...])
    gated_gla = gates[:, D_MODEL:] * y_gla
    gate_sb = gates[:, :D_MODEL]
    sb_zs = _silu(proj(OFF_SB_Z, SB_WIDTH))

    def finish():
        o_sb = jnp.concatenate([acc_ref[pair] for pair in range(SB_PAIRS)], axis=1)
        y_sb = _dot((o_sb * sb_zs).astype(_BF16), w_o_sb_ref[...])
        merged = gate_sb * y_sb + gated_gla
        res = x + _dot(merged.astype(_BF16), w_out_ref[...])
        out_ref[0] = _rms(res, g_final_ref[...])

    finish()

    @pl.when(jnp.logical_and(step >= 2, jnp.min(nc_ref[...]) < SB_DEAD_LOG2))
    def _():
        _sb_far_blocks(k_all, vm_all, step, qs_ref, nc_ref, acc_ref)
        finish()


def _resident(shape):
    return pl.BlockSpec(shape, lambda b, i: (0,) * len(shape), pipeline_mode=pl.Buffered(1))


@jax.jit
def kernel(x, g_pre, w_in, b_gate, w_alpha_up, b_alpha_up, g_gla_norm, w_o_sb, w_o_gla, w_out, g_final):
    batch, seq, d_model = x.shape
    depth = w_in.shape[0]
    assert d_model == D_MODEL and seq % TQ == 0 and depth == 1
    w_up = jnp.pad(w_alpha_up[0], ((0, LR_PAD - GLA_RANK), (0, 0)))
    w_in_t = jnp.transpose(w_in, (0, 2, 1))

    row_block = pl.BlockSpec((1, TQ, D_MODEL), lambda b, i: (b, i, 0))
    in_hbm = pl.BlockSpec(memory_space=pl.ANY)
    return pl.pallas_call(
        _block_kernel,
        grid=(batch, seq // TQ),
        in_specs=[
            row_block,
            _resident((1, D_MODEL)),
            in_hbm,
            _resident((1, 2 * D_MODEL)),
            _resident((LR_PAD, GLA_K_WIDTH)),
            _resident((1, GLA_K_WIDTH)),
            _resident((1, GLA_VAL_DIM)),
            in_hbm,
            in_hbm,
            in_hbm,
            _resident((1, D_MODEL)),
        ],
        out_specs=row_block,
        out_shape=jax.ShapeDtypeStruct(x.shape, x.dtype),
        scratch_shapes=[
            pltpu.VMEM((D_MODEL, OFF_LR), _BF16),
            pltpu.VMEM((D_MODEL, GLA_K_WIDTH), _BF16),
            pltpu.VMEM((D_MODEL, 2 * D_MODEL), _BF16),
            pltpu.VMEM((SB_WIDTH, D_MODEL), _BF16),
            pltpu.VMEM((GLA_V_WIDTH, D_MODEL), _BF16),
            pltpu.VMEM((D_MODEL, D_MODEL), _BF16),
            pltpu.VMEM((WEIGHT_BUFFERS, WEIGHT_CHUNK_ROWS, D_MODEL), _F32),
            pltpu.SemaphoreType.DMA((WEIGHT_BUFFERS,)),
            pltpu.VMEM((seq + TQ, SB_WIDTH), _BF16),
            pltpu.VMEM((2, seq + TQ, SB_WIDTH), _BF16),
            pltpu.VMEM((GLA_HEADS // 2, GLA_VAL_DIM, LANES), _F32),
            pltpu.VMEM((SB_PAIRS, 2 * TQ, LANES), _BF16),
            pltpu.VMEM((SB_PAIRS, 2 * TQ, LANES), _F32),
            pltpu.VMEM((SB_PAIRS, TQ, LANES), _F32),
        ],
        compiler_params=pltpu.CompilerParams(
            dimension_semantics=("arbitrary", "arbitrary"), vmem_limit_bytes=VMEM_LIMIT_BYTES),
        name="hybrid_mixer_block",
    )(x, g_pre, w_in_t, b_gate, w_up, b_alpha_up, g_gla_norm, w_o_sb, w_o_gla, w_out, g_final.reshape(1, D_MODEL))
```

```python
import math

import jax
import jax.numpy as jnp
from jax import lax
from jax.experimental import pallas as pl
from jax.experimental.pallas import tpu as pltpu

D_MODEL = 1024
CHUNK = 64
SB_HEAD_DIM = 64
SB_WIDTH = D_MODEL // 2
GLA_HEADS = 4
GLA_K_WIDTH = D_MODEL // 4
GLA_V_WIDTH = D_MODEL // 2
GLA_KEY_DIM = GLA_K_WIDTH // GLA_HEADS
GLA_VAL_DIM = GLA_V_WIDTH // GLA_HEADS
GLA_RANK = 16
GLA_GATE_TEMP = 16.0
EPS = 1e-6
LOG2_E = math.log2(math.e)
EXP2_CLAMP = 126.0
SB_DEAD_LOG2 = 151.0

LANES = 128
SUBLANES = 8
SB_PAIRS = SB_WIDTH // LANES
SB_GROUP = 2
TQ = 256
N_CHUNKS = TQ // CHUNK
LR_PAD = LANES
WEIGHT_CHUNK_ROWS = 512
WEIGHT_BUFFERS = 3

OFF_SB_Q = 0
OFF_SB_K = OFF_SB_Q + SB_WIDTH
OFF_SB_V = OFF_SB_K + SB_WIDTH
OFF_SB_Z = OFF_SB_V + SB_WIDTH
OFF_GLA_Q = OFF_SB_Z + SB_WIDTH
OFF_GLA_K = OFF_GLA_Q + GLA_K_WIDTH
OFF_GLA_V = OFF_GLA_K + GLA_K_WIDTH
OFF_GLA_Z = OFF_GLA_V + GLA_V_WIDTH
OFF_LR = OFF_GLA_Z + GLA_V_WIDTH

VMEM_LIMIT_BYTES = 56 * 1024 * 1024

_F32 = jnp.float32
_BF16 = jnp.bfloat16
_NT = (((1,), (1,)), ((), ()))
_TN = (((0,), (0,)), ((), ()))


def _dot(a, b):
    return jnp.dot(a, b, preferred_element_type=_F32)


def _dot_nt(a, b):
    return lax.dot_general(a, b, _NT, preferred_element_type=_F32)


def _log_sigmoid(x):
    return jnp.minimum(x, 0.0) - jnp.log(1.0 + jnp.exp(-jnp.abs(x)))


def _silu(x):
    return x * jax.nn.sigmoid(x)


def _rms(x, gain):
    return x * lax.rsqrt(jnp.mean(x * x, axis=-1, keepdims=True) + EPS) * gain


def _sb_neg_log2_miss(zz):
    return jnp.maximum(jnp.log(1.0 + jnp.exp2(jnp.minimum(zz, EXP2_CLAMP))) * LOG2_E, zz)


def _sb_incl():
    row = lax.broadcasted_iota(jnp.int32, (TQ, TQ), 0)
    col = lax.broadcasted_iota(jnp.int32, (TQ, TQ), 1)
    return (row >= col).astype(_BF16)


def _sb_near_blocks(q, k_all, vm_all, step, qs_ref, nc_ref, acc_ref):
    n_heads = 2 * SB_GROUP
    width = SB_GROUP * LANES
    row = lax.broadcasted_iota(jnp.int32, (n_heads * TQ, TQ), 0) & (TQ - 1)
    col = lax.broadcasted_iota(jnp.int32, (n_heads * TQ, TQ), 1)
    causal = col < row
    incl = _sb_incl()
    keys = pl.ds(pl.multiple_of(step * TQ, TQ), 2 * TQ)
    head_of_lane = lax.broadcasted_iota(jnp.int32, (1, width), 1) // SB_HEAD_DIM
    zero_v = jnp.zeros((2 * TQ, LANES), _BF16)
    for group in range(SB_PAIRS // SB_GROUP):
        cols = slice(group * width, (group + 1) * width)
        q_group = q[:, cols]
        qs = jnp.concatenate([jnp.where(head_of_lane == hd, q_group, 0) for hd in range(n_heads)], axis=0)
        zz = _dot_nt(qs, k_all[keys, cols])
        zz_prev, zz_own = zz[:, :TQ], zz[:, TQ:]
        nl_prev = _sb_neg_log2_miss(zz_prev)
        nl_own = jnp.where(causal, _sb_neg_log2_miss(zz_own), 0.0)
        tot_own = _dot(nl_own.astype(_BF16), incl)
        tot_prev = _dot(nl_prev.astype(_BF16), incl) + tot_own[:, 0:1]
        w_prev = jnp.exp2(zz_prev - tot_prev)
        w_own = jnp.where(causal, jnp.exp2(zz_own - tot_own), 0.0)
        carried = jnp.broadcast_to(tot_prev[:, 0:1], (n_heads * TQ, LANES))
        w_cat = jnp.concatenate(
            [blk[hd * TQ:(hd + 1) * TQ] for hd in range(n_heads) for blk in (w_prev, w_own)], axis=1).astype(_BF16)
        v_cat = jnp.concatenate([
            jnp.concatenate([vm_all[hd % 2, keys, (group * SB_GROUP + p) * LANES:(group * SB_GROUP + p + 1) * LANES]
                             if p == hd // 2 else zero_v for p in range(SB_GROUP)], axis=1)
            for hd in range(n_heads)], axis=0)
        out = _dot(w_cat, v_cat)
        for p in range(SB_GROUP):
            pair = group * SB_GROUP + p
            heads = slice(2 * p * TQ, (2 * p + 2) * TQ)
            qs_ref[pair] = qs[heads, p * LANES:(p + 1) * LANES]
            nc_ref[pair] = carried[heads]
            acc_ref[pair] = out[:, p * LANES:(p + 1) * LANES]


def _sb_far_blocks(k_all, vm_all, step, qs_ref, nc_ref, acc_ref):
    def more_blocks(carry):
        t, live = carry
        return jnp.logical_and(t < step - 1, live)

    def earlier_block(carry):
        t, _ = carry
        keys = pl.ds(pl.multiple_of((step - 1 - t) * TQ, TQ), TQ)
        incl = _sb_incl()
        for pair in range(SB_PAIRS):
            cols = slice(pair * LANES, (pair + 1) * LANES)
            zz = _dot_nt(qs_ref[pair], k_all[keys, cols])
            nl = _sb_neg_log2_miss(zz)
            tot = _dot(nl.astype(_BF16), incl) + jnp.concatenate([nc_ref[pair]] * (TQ // LANES), axis=1)
            w = jnp.exp2(zz - tot)
            w_cat = jnp.concatenate([w[:TQ], w[TQ:]], axis=1).astype(_BF16)
            v_cat = jnp.concatenate([vm_all[0, keys, cols], vm_all[1, keys, cols]], axis=0)
            acc_ref[pair] += _dot(w_cat, v_cat)
            nc_ref[pair] = jnp.broadcast_to(tot[:, 0:1], (2 * TQ, LANES))
        return t + 1, jnp.min(nc_ref[...]) < SB_DEAD_LOG2

    lax.while_loop(more_blocks, earlier_block, (jnp.int32(0), jnp.min(nc_ref[...]) < SB_DEAD_LOG2))


def _chunk_block_diag(x):
    zero = jnp.zeros((CHUNK, LANES), x.dtype)
    out_rows = []
    for r in range(x.shape[0] // CHUNK):
        blocks = [x[r * CHUNK:(r + 1) * CHUNK] if c == r % N_CHUNKS else zero for c in range(N_CHUNKS)]
        out_rows.append(jnp.concatenate(blocks, axis=1))
    return jnp.concatenate(out_rows, axis=0)


def _chunk_prefix_sum(x):
    sub = lax.broadcasted_iota(jnp.int32, (TQ, 1), 0) & (SUBLANES - 1)
    for shift in (1, 2, 4):
        x = x + jnp.where(sub >= shift, pltpu.roll(x, shift, axis=0), 0.0)
    groups = []
    for g in range(TQ // SUBLANES):
        blk = x[g * SUBLANES:(g + 1) * SUBLANES]
        if g % (CHUNK // SUBLANES):
            blk = blk + jnp.broadcast_to(groups[-1][SUBLANES - 1:], blk.shape)
        groups.append(blk)
    return jnp.concatenate(groups, axis=0)


def _gla(q, k, v, log_alpha, state_ref, gain):
    cum = _chunk_prefix_sum(log_alpha)
    chunk_total = [cum[(c + 1) * CHUNK - 1:(c + 1) * CHUNK] for c in range(N_CHUNKS)]
    cum_last = jnp.concatenate([jnp.broadcast_to(t, (CHUNK, GLA_K_WIDTH)) for t in chunk_total], axis=0)
    q_dec = (q * jnp.exp(cum) * (GLA_KEY_DIM ** -0.5)).astype(_BF16)
    k_inv = (k * jnp.exp(-cum)).astype(_BF16)
    k_end = (k * jnp.exp(cum_last - cum)).astype(_BF16)
    decay = [jnp.exp(t) for t in chunk_total]
    v_bf = v.astype(_BF16)

    row_all = lax.broadcasted_iota(jnp.int32, (GLA_HEADS * TQ, TQ), 0) & (TQ - 1)
    col_all = lax.broadcasted_iota(jnp.int32, (GLA_HEADS * TQ, TQ), 1)
    causal_all = jnp.logical_and(row_all // CHUNK == col_all // CHUNK, col_all <= row_all)
    head_of_lane = lax.broadcasted_iota(jnp.int32, (1, GLA_K_WIDTH), 1) // GLA_KEY_DIM
    qs_all = jnp.concatenate([jnp.where(head_of_lane == hd, q_dec, 0) for hd in range(GLA_HEADS)], axis=0)
    scores_all = jnp.where(causal_all, _dot_nt(qs_all, k_inv), 0.0).astype(_BF16)
    first_head = lax.broadcasted_iota(jnp.int32, (1, LANES), 1) < GLA_KEY_DIM
    first_head_rows = lax.broadcasted_iota(jnp.int32, (LANES, 1), 0) < GLA_KEY_DIM
    zero_v = jnp.zeros((TQ, GLA_VAL_DIM), _BF16)
    outs = []
    for pair in range(GLA_HEADS // 2):
        cols = slice(pair * LANES, (pair + 1) * LANES)
        v_pair = v_bf[:, 2 * pair * GLA_VAL_DIM:(2 * pair + 2) * GLA_VAL_DIM]
        scores = jnp.concatenate([scores_all[2 * pair * TQ:(2 * pair + 1) * TQ],
                                  scores_all[(2 * pair + 1) * TQ:(2 * pair + 2) * TQ]], axis=1)
        v_diag = jnp.concatenate([jnp.concatenate([v_pair[:, :GLA_VAL_DIM], zero_v], axis=1),
                                  jnp.concatenate([zero_v, v_pair[:, GLA_VAL_DIM:]], axis=1)], axis=0)
        kv_all = lax.dot_general(v_pair, _chunk_block_diag(k_end[:, cols]), _TN, preferred_element_type=_F32)
        states = [state_ref[pair]]
        for c in range(N_CHUNKS):
            blk = slice(c * LANES, (c + 1) * LANES)
            kv = jnp.where(first_head, kv_all[:GLA_VAL_DIM, blk], kv_all[GLA_VAL_DIM:, blk])
            states.append(states[-1] * decay[c][:, cols] + kv)
        state_ref[pair] = states[-1]
        state_rows = []
        for c in range(N_CHUNKS):
            turned = states[c].T
            state_rows.append(jnp.concatenate([jnp.where(first_head_rows, turned, 0.0),
                                               jnp.where(first_head_rows, 0.0, turned)], axis=1))
        rhs = jnp.concatenate([v_diag, jnp.concatenate(state_rows, axis=0).astype(_BF16)], axis=0)
        lhs = jnp.concatenate([scores, _chunk_block_diag(q_dec[:, cols])], axis=1)
        out_pair = _dot(lhs, rhs)
        outs.append(_rms(out_pair[:, :GLA_VAL_DIM], gain))
        outs.append(_rms(out_pair[:, GLA_VAL_DIM:], gain))
    return jnp.concatenate(outs, axis=1)


def _load_weights(w_in_t_hbm, w_up_ref, w_o_sb_hbm, w_o_gla_hbm, w_out_hbm, w_in_ref, w_alpha_ref, w_gate_ref,
                  w_o_sb_ref, w_o_gla_ref, w_out_ref, stage_ref, sem_ref):
    rows = WEIGHT_CHUNK_ROWS
    jobs = []

    def into_columns(dst, block):
        def consume(chunk):
            dst[:, block * rows:(block + 1) * rows] = chunk.T.astype(_BF16)
        return consume

    def into_rows(dst, block):
        def consume(chunk):
            dst[block * rows:(block + 1) * rows, :] = chunk.astype(_BF16)
        return consume

    def low_rank(chunk):
        lane = lax.broadcasted_iota(jnp.int32, (1, LR_PAD), 1)
        w_lr = jnp.where(lane < GLA_RANK, chunk[:LR_PAD].T, 0.0)
        w_alpha_ref[...] = jnp.dot(w_lr, w_up_ref[...], precision=lax.Precision.HIGHEST,
                                   preferred_element_type=_F32).astype(_BF16)

    w_in_t = w_in_t_hbm.at[0]
    for block in range(OFF_LR // rows):
        jobs.append((w_in_t, block * rows, into_columns(w_in_ref, block)))
    jobs.append((w_in_t, OFF_LR, low_rank))
    for block in range(2 * D_MODEL // rows):
        jobs.append((w_in_t, OFF_LR + GLA_RANK + block * rows, into_columns(w_gate_ref, block)))
    for src, dst in ((w_o_sb_hbm, w_o_sb_ref), (w_o_gla_hbm, w_o_gla_ref), (w_out_hbm, w_out_ref)):
        for block in range(dst.shape[0] // rows):
            jobs.append((src.at[0], block * rows, into_rows(dst, block)))

    n_buf = stage_ref.shape[0]

    def copy(j):
        src, row0, _ = jobs[j]
        return pltpu.make_async_copy(src.at[pl.ds(row0, rows), :], stage_ref.at[j % n_buf], sem_ref.at[j % n_buf])

    for j in range(min(n_buf - 1, len(jobs))):
        copy(j).start()
    for j in range(len(jobs)):
        if j + n_buf - 1 < len(jobs):
            copy(j + n_buf - 1).start()
        copy(j).wait()
        jobs[j][2](stage_ref[j % n_buf])


def _block_kernel(x_ref, g_pre_ref, w_in_t_hbm, b_gate_ref, w_up_ref, b_up_ref, g_gla_ref,
                  w_o_sb_hbm, w_o_gla_hbm, w_out_hbm, g_final_ref, out_ref,
                  w_in_ref, w_alpha_ref, w_gate_ref, w_o_sb_ref, w_o_gla_ref, w_out_ref, stage_ref, sem_ref,
                  k_all, vm_all, state_ref, qs_ref, nc_ref, acc_ref):
    step = pl.program_id(1)

    @pl.when(jnp.logical_and(pl.program_id(0) == 0, step == 0))
    def _():
        _load_weights(w_in_t_hbm, w_up_ref, w_o_sb_hbm, w_o_gla_hbm, w_out_hbm, w_in_ref, w_alpha_ref, w_gate_ref,
                      w_o_sb_ref, w_o_gla_ref, w_out_ref, stage_ref, sem_ref)

    @pl.when(step == 0)
    def _():
        state_ref[...] = jnp.zeros_like(state_ref)
        k_all[0:TQ, :] = jnp.zeros((TQ, SB_WIDTH), _BF16)
        vm_all[:, 0:TQ, :] = jnp.zeros((2, TQ, SB_WIDTH), _BF16)

    x = x_ref[0]
    h = (x * g_pre_ref[...]).astype(_BF16)
    inv_rms = jnp.broadcast_to(lax.rsqrt(jnp.mean(x * x, axis=-1, keepdims=True) + EPS), (TQ, LANES))

    def scale_rows(p):
        return p * jnp.concatenate([inv_rms] * (p.shape[1] // LANES), axis=1)

    def proj(offset, width):
        return scale_rows(_dot(h, w_in_ref[:, offset:offset + width]))

    rows = pl.ds(pl.multiple_of((step + 1) * TQ, TQ), TQ)
    k_all[rows, :] = proj(OFF_SB_K, SB_WIDTH).astype(_BF16)
    sb_v = proj(OFF_SB_V, SB_WIDTH).astype(_BF16)
    first_head = (lax.broadcasted_iota(jnp.int32, (1, SB_WIDTH), 1) & (LANES - 1)) < SB_HEAD_DIM
    vm_all[0, rows, :] = jnp.where(first_head, sb_v, 0)
    vm_all[1, rows, :] = jnp.where(first_head, 0, sb_v)
    sb_q = (proj(OFF_SB_Q, SB_WIDTH) * (LOG2_E * SB_HEAD_DIM ** -0.5)).astype(_BF16)
    _sb_near_blocks(sb_q, k_all, vm_all, step, qs_ref, nc_ref, acc_ref)

    alpha_logit = scale_rows(_dot(h, w_alpha_ref[...])) + b_up_ref[...]
    log_alpha = _log_sigmoid(alpha_logit) / GLA_GATE_TEMP
    o_gla = _gla(proj(OFF_GLA_Q, GLA_K_WIDTH), proj(OFF_GLA_K, GLA_K_WIDTH), proj(OFF_GLA_V, GLA_V_WIDTH),
                 log_alpha, state_ref, g_gla_ref[...])
    y_gla = _dot((o_gla * _silu(proj(OFF_GLA_Z, GLA_V_WIDTH))).astype(_BF16), w_o_gla_ref[...])
    gates = jax.nn.sigmoid(scale_rows(_dot(h, w_gate_ref[...])) + b_gate_ref[...])
    gated_gla = gates[:, D_MODEL:] * y_gla
    gate_sb = gates[:, :D_MODEL]
    sb_zs = _silu(proj(OFF_SB_Z, SB_WIDTH))

    def finish():
        o_sb = jnp.concatenate([acc_ref[pair] for pair in range(SB_PAIRS)], axis=1)
        y_sb = _dot((o_sb * sb_zs).astype(_BF16), w_o_sb_ref[...])
        merged = gate_sb * y_sb + gated_gla
        res = x + _dot(merged.astype(_BF16), w_out_ref[...])
        out_ref[0] = _rms(res, g_final_ref[...])

    finish()

    @pl.when(jnp.logical_and(step >= 2, jnp.min(nc_ref[...]) < SB_DEAD_LOG2))
    def _():
        _sb_far_blocks(k_all, vm_all, step, qs_ref, nc_ref, acc_ref)
        finish()


def _resident(shape):
    return pl.BlockSpec(shape, lambda b, i: (0,) * len(shape), pipeline_mode=pl.Buffered(1))


@jax.jit
def kernel(x, g_pre, w_in, b_gate, w_alpha_up, b_alpha_up, g_gla_norm, w_o_sb, w_o_gla, w_out, g_final):
    batch, seq, d_model = x.shape
    depth = w_in.shape[0]
    assert d_model == D_MODEL and seq % TQ == 0 and depth == 1
    w_up = jnp.pad(w_alpha_up[0], ((0, LR_PAD - GLA_RANK), (0, 0)))
    w_in_t = jnp.transpose(w_in, (0, 2, 1))

    row_block = pl.BlockSpec((1, TQ, D_MODEL), lambda b, i: (b, i, 0))
    in_hbm = pl.BlockSpec(memory_space=pl.ANY)
    return pl.pallas_call(
        _block_kernel,
        grid=(batch, seq // TQ),
        in_specs=[
            row_block,
            _resident((1, D_MODEL)),
            in_hbm,
            _resident((1, 2 * D_MODEL)),
            _resident((LR_PAD, GLA_K_WIDTH)),
            _resident((1, GLA_K_WIDTH)),
            _resident((1, GLA_VAL_DIM)),
            in_hbm,
            in_hbm,
            in_hbm,
            _resident((1, D_MODEL)),
        ],
        out_specs=row_block,
        out_shape=jax.ShapeDtypeStruct(x.shape, x.dtype),
        scratch_shapes=[
            pltpu.VMEM((D_MODEL, OFF_LR), _BF16),
            pltpu.VMEM((D_MODEL, GLA_K_WIDTH), _BF16),
            pltpu.VMEM((D_MODEL, 2 * D_MODEL), _BF16),
            pltpu.VMEM((SB_WIDTH, D_MODEL), _BF16),
            pltpu.VMEM((GLA_V_WIDTH, D_MODEL), _BF16),
            pltpu.VMEM((D_MODEL, D_MODEL), _BF16),
            pltpu.VMEM((WEIGHT_BUFFERS, WEIGHT_CHUNK_ROWS, D_MODEL), _F32),
            pltpu.SemaphoreType.DMA((WEIGHT_BUFFERS,)),
            pltpu.VMEM((seq + TQ, SB_WIDTH), _BF16),
            pltpu.VMEM((2, seq + TQ, SB_WIDTH), _BF16),
            pltpu.VMEM((GLA_HEADS // 2, GLA_VAL_DIM, LANES), _F32),
            pltpu.VMEM((SB_PAIRS, 2 * TQ, LANES), _BF16),
            pltpu.VMEM((SB_PAIRS, 2 * TQ, LANES), _F32),
            pltpu.VMEM((SB_PAIRS, TQ, LANES), _F32),
        ],
        compiler_params=pltpu.CompilerParams(
            dimension_semantics=("arbitrary", "arbitrary"), vmem_limit_bytes=VMEM_LIMIT_BYTES),
        name="hybrid_mixer_block",
    )(x, g_pre, w_in_t, b_gate, w_up, b_alpha_up, g_gla_norm, w_o_sb, w_o_gla, w_out, g_final.reshape(1, D_MODEL))
```

```python
import math

import jax
import jax.numpy as jnp
from jax import lax
from jax.experimental import pallas as pl
from jax.experimental.pallas import tpu as pltpu

D_MODEL = 1024
CHUNK = 64
SB_HEAD_DIM = 64
SB_WIDTH = D_MODEL // 2
GLA_HEADS = 4
GLA_K_WIDTH = D_MODEL // 4
GLA_V_WIDTH = D_MODEL // 2
GLA_KEY_DIM = GLA_K_WIDTH // GLA_HEADS
GLA_VAL_DIM = GLA_V_WIDTH // GLA_HEADS
GLA_RANK = 16
GLA_GATE_TEMP = 16.0
EPS = 1e-6
LOG2_E = math.log2(math.e)
EXP2_CLAMP = 126.0
SB_DEAD_LOG2 = 151.0

LANES = 128
SUBLANES = 8
SB_PAIRS = SB_WIDTH // LANES
SB_GROUP = 2
TQ = 256
STEP_BLOCKS = 2
STEP_ROWS = STEP_BLOCKS * TQ
N_CHUNKS = TQ // CHUNK
LR_PAD = LANES
WEIGHT_CHUNK_ROWS = 512
WEIGHT_BUFFERS = 3

OFF_SB_Q = 0
OFF_SB_K = OFF_SB_Q + SB_WIDTH
OFF_SB_V = OFF_SB_K + SB_WIDTH
OFF_SB_Z = OFF_SB_V + SB_WIDTH
OFF_GLA_Q = OFF_SB_Z + SB_WIDTH
OFF_GLA_K = OFF_GLA_Q + GLA_K_WIDTH
OFF_GLA_V = OFF_GLA_K + GLA_K_WIDTH
OFF_GLA_Z = OFF_GLA_V + GLA_V_WIDTH
OFF_LR = OFF_GLA_Z + GLA_V_WIDTH

VMEM_LIMIT_BYTES = 56 * 1024 * 1024

_F32 = jnp.float32
_BF16 = jnp.bfloat16
_NT = (((1,), (1,)), ((), ()))
_TN = (((0,), (0,)), ((), ()))


def _dot(a, b):
    return jnp.dot(a, b, preferred_element_type=_F32)


def _dot_nt(a, b):
    return lax.dot_general(a, b, _NT, preferred_element_type=_F32)


def _log_sigmoid(x):
    return jnp.minimum(x, 0.0) - jnp.log(1.0 + jnp.exp(-jnp.abs(x)))


def _silu(x):
    return x * jax.nn.sigmoid(x)


def _rms(x, gain):
    return x * lax.rsqrt(jnp.mean(x * x, axis=-1, keepdims=True) + EPS) * gain


def _sb_neg_log2_miss(zz):
    return jnp.maximum(jnp.log(1.0 + jnp.exp2(jnp.minimum(zz, EXP2_CLAMP))) * LOG2_E, zz)


def _sb_incl():
    row = lax.broadcasted_iota(jnp.int32, (TQ, TQ), 0)
    col = lax.broadcasted_iota(jnp.int32, (TQ, TQ), 1)
    return (row >= col).astype(_BF16)


def _sb_near_blocks(q, k_all, vm_all, step, qs_ref, nc_ref, acc_ref):
    n_heads = 2 * SB_GROUP
    width = SB_GROUP * LANES
    row = lax.broadcasted_iota(jnp.int32, (n_heads * TQ, TQ), 0) & (TQ - 1)
    col = lax.broadcasted_iota(jnp.int32, (n_heads * TQ, TQ), 1)
    causal = col < row
    incl = _sb_incl()
    keys = pl.ds(pl.multiple_of(step * TQ, TQ), 2 * TQ)
    head_of_lane = lax.broadcasted_iota(jnp.int32, (1, width), 1) // SB_HEAD_DIM
    zero_v = jnp.zeros((2 * TQ, LANES), _BF16)
    for group in range(SB_PAIRS // SB_GROUP):
        cols = slice(group * width, (group + 1) * width)
        q_group = q[:, cols]
        qs = jnp.concatenate([jnp.where(head_of_lane == hd, q_group, 0) for hd in range(n_heads)], axis=0)
        zz = _dot_nt(qs, k_all[keys, cols])
        zz_prev, zz_own = zz[:, :TQ], zz[:, TQ:]
        nl_prev = _sb_neg_log2_miss(zz_prev)
        nl_own = jnp.where(causal, _sb_neg_log2_miss(zz_own), 0.0)
        tot_own = _dot(nl_own.astype(_BF16), incl)
        tot_prev = _dot(nl_prev.astype(_BF16), incl) + tot_own[:, 0:1]
        w_prev = jnp.exp2(zz_prev - tot_prev)
        w_own = jnp.where(causal, jnp.exp2(zz_own - tot_own), 0.0)
        carried = jnp.broadcast_to(tot_prev[:, 0:1], (n_heads * TQ, LANES))
        w_cat = jnp.concatenate(
            [blk[hd * TQ:(hd + 1) * TQ] for hd in range(n_heads) for blk in (w_prev, w_own)], axis=1).astype(_BF16)
        v_cat = jnp.concatenate([
            jnp.concatenate([vm_all[hd % 2, keys, (group * SB_GROUP + p) * LANES:(group * SB_GROUP + p + 1) * LANES]
                             if p == hd // 2 else zero_v for p in range(SB_GROUP)], axis=1)
            for hd in range(n_heads)], axis=0)
        out = _dot(w_cat, v_cat)
        for p in range(SB_GROUP):
            pair = group * SB_GROUP + p
            heads = slice(2 * p * TQ, (2 * p + 2) * TQ)
            qs_ref[pair] = qs[heads, p * LANES:(p + 1) * LANES]
            nc_ref[pair] = carried[heads]
            acc_ref[pair] = out[:, p * LANES:(p + 1) * LANES]


def _sb_far_blocks(k_all, vm_all, step, qs_ref, nc_ref, acc_ref):
    def more_blocks(carry):
        t, live = carry
        return jnp.logical_and(t < step - 1, live)

    def earlier_block(carry):
        t, _ = carry
        keys = pl.ds(pl.multiple_of((step - 1 - t) * TQ, TQ), TQ)
        incl = _sb_incl()
        for pair in range(SB_PAIRS):
            cols = slice(pair * LANES, (pair + 1) * LANES)
            zz = _dot_nt(qs_ref[pair], k_all[keys, cols])
            nl = _sb_neg_log2_miss(zz)
            tot = _dot(nl.astype(_BF16), incl) + jnp.concatenate([nc_ref[pair]] * (TQ // LANES), axis=1)
            w = jnp.exp2(zz - tot)
            w_cat = jnp.concatenate([w[:TQ], w[TQ:]], axis=1).astype(_BF16)
            v_cat = jnp.concatenate([vm_all[0, keys, cols], vm_all[1, keys, cols]], axis=0)
            acc_ref[pair] += _dot(w_cat, v_cat)
            nc_ref[pair] = jnp.broadcast_to(tot[:, 0:1], (2 * TQ, LANES))
        return t + 1, jnp.min(nc_ref[...]) < SB_DEAD_LOG2

    lax.while_loop(more_blocks, earlier_block, (jnp.int32(0), jnp.min(nc_ref[...]) < SB_DEAD_LOG2))


def _chunk_block_diag(x):
    zero = jnp.zeros((CHUNK, LANES), x.dtype)
    out_rows = []
    for r in range(x.shape[0] // CHUNK):
        blocks = [x[r * CHUNK:(r + 1) * CHUNK] if c == r % N_CHUNKS else zero for c in range(N_CHUNKS)]
        out_rows.append(jnp.concatenate(blocks, axis=1))
    return jnp.concatenate(out_rows, axis=0)


def _chunk_prefix_sum(x):
    sub = lax.broadcasted_iota(jnp.int32, (TQ, 1), 0) & (SUBLANES - 1)
    for shift in (1, 2, 4):
        x = x + jnp.where(sub >= shift, pltpu.roll(x, shift, axis=0), 0.0)
    groups = []
    for g in range(TQ // SUBLANES):
        blk = x[g * SUBLANES:(g + 1) * SUBLANES]
        if g % (CHUNK // SUBLANES):
            blk = blk + jnp.broadcast_to(groups[-1][SUBLANES - 1:], blk.shape)
        groups.append(blk)
    return jnp.concatenate(groups, axis=0)


def _gla(q, k, v, log_alpha, state_ref, gain):
    cum = _chunk_prefix_sum(log_alpha)
    chunk_total = [cum[(c + 1) * CHUNK - 1:(c + 1) * CHUNK] for c in range(N_CHUNKS)]
    cum_last = jnp.concatenate([jnp.broadcast_to(t, (CHUNK, GLA_K_WIDTH)) for t in chunk_total], axis=0)
    q_dec = (q * jnp.exp(cum) * (GLA_KEY_DIM ** -0.5)).astype(_BF16)
    k_inv = (k * jnp.exp(-cum)).astype(_BF16)
    k_end = (k * jnp.exp(cum_last - cum)).astype(_BF16)
    decay = [jnp.exp(t) for t in chunk_total]
    v_bf = v.astype(_BF16)

    row_all = lax.broadcasted_iota(jnp.int32, (GLA_HEADS * TQ, TQ), 0) & (TQ - 1)
    col_all = lax.broadcasted_iota(jnp.int32, (GLA_HEADS * TQ, TQ), 1)
    causal_all = jnp.logical_and(row_all // CHUNK == col_all // CHUNK, col_all <= row_all)
    head_of_lane = lax.broadcasted_iota(jnp.int32, (1, GLA_K_WIDTH), 1) // GLA_KEY_DIM
    qs_all = jnp.concatenate([jnp.where(head_of_lane == hd, q_dec, 0) for hd in range(GLA_HEADS)], axis=0)
    scores_all = jnp.where(causal_all, _dot_nt(qs_all, k_inv), 0.0).astype(_BF16)
    first_head = lax.broadcasted_iota(jnp.int32, (1, LANES), 1) < GLA_KEY_DIM
    first_head_rows = lax.broadcasted_iota(jnp.int32, (LANES, 1), 0) < GLA_KEY_DIM
    zero_v = jnp.zeros((TQ, GLA_VAL_DIM), _BF16)
    outs = []
    for pair in range(GLA_HEADS // 2):
        cols = slice(pair * LANES, (pair + 1) * LANES)
        v_pair = v_bf[:, 2 * pair * GLA_VAL_DIM:(2 * pair + 2) * GLA_VAL_DIM]
        scores = jnp.concatenate([scores_all[2 * pair * TQ:(2 * pair + 1) * TQ],
                                  scores_all[(2 * pair + 1) * TQ:(2 * pair + 2) * TQ]], axis=1)
        v_diag = jnp.concatenate([jnp.concatenate([v_pair[:, :GLA_VAL_DIM], zero_v], axis=1),
                                  jnp.concatenate([zero_v, v_pair[:, GLA_VAL_DIM:]], axis=1)], axis=0)
        kv_all = lax.dot_general(v_pair, _chunk_block_diag(k_end[:, cols]), _TN, preferred_element_type=_F32)
        states = [state_ref[pair]]
        for c in range(N_CHUNKS):
            blk = slice(c * LANES, (c + 1) * LANES)
            kv = jnp.where(first_head, kv_all[:GLA_VAL_DIM, blk], kv_all[GLA_VAL_DIM:, blk])
            states.append(states[-1] * decay[c][:, cols] + kv)
        state_ref[pair] = states[-1]
        state_rows = []
        for c in range(N_CHUNKS):
            turned = states[c].T
            state_rows.append(jnp.concatenate([jnp.where(first_head_rows, turned, 0.0),
                                               jnp.where(first_head_rows, 0.0, turned)], axis=1))
        rhs = jnp.concatenate([v_diag, jnp.concatenate(state_rows, axis=0).astype(_BF16)], axis=0)
        lhs = jnp.concatenate([scores, _chunk_block_diag(q_dec[:, cols])], axis=1)
        out_pair = _dot(lhs, rhs)
        outs.append(_rms(out_pair[:, :GLA_VAL_DIM], gain))
        outs.append(_rms(out_pair[:, GLA_VAL_DIM:], gain))
    return jnp.concatenate(outs, axis=1)


def _load_weights(w_in_t_hbm, w_up_ref, w_o_sb_hbm, w_o_gla_hbm, w_out_hbm, w_in_ref, w_alpha_ref, w_gate_ref,
                  w_o_sb_ref, w_o_gla_ref, w_out_ref, stage_ref, sem_ref):
    rows = WEIGHT_CHUNK_ROWS
    jobs = []

    def into_columns(dst, block):
        def consume(chunk):
            dst[:, block * rows:(block + 1) * rows] = chunk.T.astype(_BF16)
        return consume

    def into_rows(dst, block):
        def consume(chunk):
            dst[block * rows:(block + 1) * rows, :] = chunk.astype(_BF16)
        return consume

    def low_rank(chunk):
        lane = lax.broadcasted_iota(jnp.int32, (1, LR_PAD), 1)
        w_lr = jnp.where(lane < GLA_RANK, chunk[:LR_PAD].T, 0.0)
        w_alpha_ref[...] = jnp.dot(w_lr, w_up_ref[...], precision=lax.Precision.HIGHEST,
                                   preferred_element_type=_F32).astype(_BF16)

    w_in_t = w_in_t_hbm.at[0]
    for block in range(OFF_LR // rows):
        jobs.append((w_in_t, block * rows, into_columns(w_in_ref, block)))
    jobs.append((w_in_t, OFF_LR, low_rank))
    for block in range(2 * D_MODEL // rows):
        jobs.append((w_in_t, OFF_LR + GLA_RANK + block * rows, into_columns(w_gate_ref, block)))
    for src, dst in ((w_o_sb_hbm, w_o_sb_ref), (w_o_gla_hbm, w_o_gla_ref), (w_out_hbm, w_out_ref)):
        for block in range(dst.shape[0] // rows):
            jobs.append((src.at[0], block * rows, into_rows(dst, block)))

    n_buf = stage_ref.shape[0]

    def copy(j):
        src, row0, _ = jobs[j]
        return pltpu.make_async_copy(src.at[pl.ds(row0, rows), :], stage_ref.at[j % n_buf], sem_ref.at[j % n_buf])

    for j in range(min(n_buf - 1, len(jobs))):
        copy(j).start()
    for j in range(len(jobs)):
        if j + n_buf - 1 < len(jobs):
            copy(j + n_buf - 1).start()
        copy(j).wait()
        jobs[j][2](stage_ref[j % n_buf])


def _block_kernel(x_ref, g_pre_ref, w_in_t_hbm, b_gate_ref, w_up_ref, b_up_ref, g_gla_ref,
                  w_o_sb_hbm, w_o_gla_hbm, w_out_hbm, g_final_ref, out_ref,
                  w_in_ref, w_alpha_ref, w_gate_ref, w_o_sb_ref, w_o_gla_ref, w_out_ref, stage_ref, sem_ref,
                  k_all, vm_all, state_ref, qs_ref, nc_ref, acc_ref):
    step = pl.program_id(1)

    @pl.when(jnp.logical_and(pl.program_id(0) == 0, step == 0))
    def _():
        _load_weights(w_in_t_hbm, w_up_ref, w_o_sb_hbm, w_o_gla_hbm, w_out_hbm, w_in_ref, w_alpha_ref, w_gate_ref,
                      w_o_sb_ref, w_o_gla_ref, w_out_ref, stage_ref, sem_ref)

    @pl.when(step == 0)
    def _():
        state_ref[...] = jnp.zeros_like(state_ref)
        k_all[0:TQ, :] = jnp.zeros((TQ, SB_WIDTH), _BF16)
        vm_all[:, 0:TQ, :] = jnp.zeros((2, TQ, SB_WIDTH), _BF16)

    x = x_ref[0]
    h = (x * g_pre_ref[...]).astype(_BF16)
    inv_rms = jnp.broadcast_to(lax.rsqrt(jnp.mean(x * x, axis=-1, keepdims=True) + EPS), (STEP_ROWS, LANES))

    def scale_rows(p):
        return p * jnp.concatenate([inv_rms] * (p.shape[1] // LANES), axis=1)

    def proj(offset, width):
        return scale_rows(_dot(h, w_in_ref[:, offset:offset + width]))

    first_block = step * STEP_BLOCKS
    block_rows = [slice(s * TQ, (s + 1) * TQ) for s in range(STEP_BLOCKS)]
    rows = pl.ds(pl.multiple_of((first_block + 1) * TQ, TQ), STEP_ROWS)
    k_all[rows, :] = proj(OFF_SB_K, SB_WIDTH).astype(_BF16)
    sb_v = proj(OFF_SB_V, SB_WIDTH).astype(_BF16)
    first_head = (lax.broadcasted_iota(jnp.int32, (1, SB_WIDTH), 1) & (LANES - 1)) < SB_HEAD_DIM
    vm_all[0, rows, :] = jnp.where(first_head, sb_v, 0)
    vm_all[1, rows, :] = jnp.where(first_head, 0, sb_v)
    sb_q = (proj(OFF_SB_Q, SB_WIDTH) * (LOG2_E * SB_HEAD_DIM ** -0.5)).astype(_BF16)
    for s in range(STEP_BLOCKS):
        _sb_near_blocks(sb_q[block_rows[s]], k_all, vm_all, first_block + s, qs_ref.at[s], nc_ref.at[s], acc_ref.at[s])

    alpha_logit = scale_rows(_dot(h, w_alpha_ref[...])) + b_up_ref[...]
    log_alpha = _log_sigmoid(alpha_logit) / GLA_GATE_TEMP
    gla_q, gla_k, gla_v = proj(OFF_GLA_Q, GLA_K_WIDTH), proj(OFF_GLA_K, GLA_K_WIDTH), proj(OFF_GLA_V, GLA_V_WIDTH)
    o_gla = jnp.concatenate(
        [_gla(gla_q[r], gla_k[r], gla_v[r], log_alpha[r], state_ref, g_gla_ref[...]) for r in block_rows], axis=0)
    y_gla = _dot((o_gla * _silu(proj(OFF_GLA_Z, GLA_V_WIDTH))).astype(_BF16), w_o_gla_ref[...])
    gates = jax.nn.sigmoid(scale_rows(_dot(h, w_gate_ref[...])) + b_gate_ref[...])
    gated_gla = gates[:, D_MODEL:] * y_gla
    gate_sb = gates[:, :D_MODEL]
    sb_zs = _silu(proj(OFF_SB_Z, SB_WIDTH))

    def finish():
        o_sb = jnp.concatenate([jnp.concatenate([acc_ref[s, pair] for pair in range(SB_PAIRS)], axis=1)
                                for s in range(STEP_BLOCKS)], axis=0)
        y_sb = _dot((o_sb * sb_zs).astype(_BF16), w_o_sb_ref[...])
        merged = gate_sb * y_sb + gated_gla
        res = x + _dot(merged.astype(_BF16), w_out_ref[...])
        out_ref[0] = _rms(res, g_final_ref[...])

    finish()

    @pl.when(jnp.logical_and(step >= 1, jnp.min(nc_ref[...]) < SB_DEAD_LOG2))
    def _():
        for s in range(STEP_BLOCKS):
            _sb_far_blocks(k_all, vm_all, first_block + s, qs_ref.at[s], nc_ref.at[s], acc_ref.at[s])
        finish()


def _resident(shape):
    return pl.BlockSpec(shape, lambda b, i: (0,) * len(shape), pipeline_mode=pl.Buffered(1))


@jax.jit
def kernel(x, g_pre, w_in, b_gate, w_alpha_up, b_alpha_up, g_gla_norm, w_o_sb, w_o_gla, w_out, g_final):
    batch, seq, d_model = x.shape
    depth = w_in.shape[0]
    assert d_model == D_MODEL and seq % STEP_ROWS == 0 and depth == 1
    w_up = jnp.pad(w_alpha_up[0], ((0, LR_PAD - GLA_RANK), (0, 0)))
    w_in_t = jnp.transpose(w_in, (0, 2, 1))

    row_block = pl.BlockSpec((1, STEP_ROWS, D_MODEL), lambda b, i: (b, i, 0))
    in_hbm = pl.BlockSpec(memory_space=pl.ANY)
    return pl.pallas_call(
        _block_kernel,
        grid=(batch, seq // STEP_ROWS),
        in_specs=[
            row_block,
            _resident((1, D_MODEL)),
            in_hbm,
            _resident((1, 2 * D_MODEL)),
            _resident((LR_PAD, GLA_K_WIDTH)),
            _resident((1, GLA_K_WIDTH)),
            _resident((1, GLA_VAL_DIM)),
            in_hbm,
            in_hbm,
            in_hbm,
            _resident((1, D_MODEL)),
        ],
        out_specs=row_block,
        out_shape=jax.ShapeDtypeStruct(x.shape, x.dtype),
        scratch_shapes=[
            pltpu.VMEM((D_MODEL, OFF_LR), _BF16),
            pltpu.VMEM((D_MODEL, GLA_K_WIDTH), _BF16),
            pltpu.VMEM((D_MODEL, 2 * D_MODEL), _BF16),
            pltpu.VMEM((SB_WIDTH, D_MODEL), _BF16),
            pltpu.VMEM((GLA_V_WIDTH, D_MODEL), _BF16),
            pltpu.VMEM((D_MODEL, D_MODEL), _BF16),
            pltpu.VMEM((WEIGHT_BUFFERS, WEIGHT_CHUNK_ROWS, D_MODEL), _F32),
            pltpu.SemaphoreType.DMA((WEIGHT_BUFFERS,)),
            pltpu.VMEM((seq + TQ, SB_WIDTH), _BF16),
            pltpu.VMEM((2, seq + TQ, SB_WIDTH), _BF16),
            pltpu.VMEM((GLA_HEADS // 2, GLA_VAL_DIM, LANES), _F32),
            pltpu.VMEM((STEP_BLOCKS, SB_PAIRS, 2 * TQ, LANES), _BF16),
            pltpu.VMEM((STEP_BLOCKS, SB_PAIRS, 2 * TQ, LANES), _F32),
            pltpu.VMEM((STEP_BLOCKS, SB_PAIRS, TQ, LANES), _F32),
        ],
        compiler_params=pltpu.CompilerParams(
            dimension_semantics=("arbitrary", "arbitrary"), vmem_limit_bytes=VMEM_LIMIT_BYTES),
        name="hybrid_mixer_block",
    )(x, g_pre, w_in_t, b_gate, w_up, b_alpha_up, g_gla_norm, w_o_sb, w_o_gla, w_out, g_final.reshape(1, D_MODEL))
```

```python
import math

import jax
import jax.numpy as jnp
from jax import lax
from jax.experimental import pallas as pl
from jax.experimental.pallas import tpu as pltpu

D_MODEL = 1024
CHUNK = 64
SB_HEAD_DIM = 64
SB_WIDTH = D_MODEL // 2
GLA_HEADS = 4
GLA_K_WIDTH = D_MODEL // 4
GLA_V_WIDTH = D_MODEL // 2
GLA_KEY_DIM = GLA_K_WIDTH // GLA_HEADS
GLA_VAL_DIM = GLA_V_WIDTH // GLA_HEADS
GLA_RANK = 16
GLA_GATE_TEMP = 16.0
EPS = 1e-6
LOG2_E = math.log2(math.e)
EXP2_CLAMP = 126.0
SB_DEAD_LOG2 = 151.0

LANES = 128
SUBLANES = 8
SB_PAIRS = SB_WIDTH // LANES
SB_GROUP = 2
TQ = 256
STEP_BLOCKS = 2
STEP_ROWS = STEP_BLOCKS * TQ
N_CHUNKS = TQ // CHUNK
LR_PAD = LANES
WEIGHT_CHUNK_ROWS = 512
WEIGHT_BUFFERS = 3

OFF_SB_Q = 0
OFF_SB_K = OFF_SB_Q + SB_WIDTH
OFF_SB_V = OFF_SB_K + SB_WIDTH
OFF_SB_Z = OFF_SB_V + SB_WIDTH
OFF_GLA_Q = OFF_SB_Z + SB_WIDTH
OFF_GLA_K = OFF_GLA_Q + GLA_K_WIDTH
OFF_GLA_V = OFF_GLA_K + GLA_K_WIDTH
OFF_GLA_Z = OFF_GLA_V + GLA_V_WIDTH
OFF_LR = OFF_GLA_Z + GLA_V_WIDTH

VMEM_LIMIT_BYTES = 56 * 1024 * 1024

_F32 = jnp.float32
_BF16 = jnp.bfloat16
_NT = (((1,), (1,)), ((), ()))
_TN = (((0,), (0,)), ((), ()))


def _dot(a, b):
    return jnp.dot(a, b, preferred_element_type=_F32)


def _dot_nt(a, b):
    return lax.dot_general(a, b, _NT, preferred_element_type=_F32)


def _log_sigmoid(x):
    return jnp.minimum(x, 0.0) - jnp.log(1.0 + jnp.exp(-jnp.abs(x)))


def _silu(x):
    return x * jax.nn.sigmoid(x)


def _rms(x, gain):
    return x * lax.rsqrt(jnp.mean(x * x, axis=-1, keepdims=True) + EPS) * gain


def _sb_neg_log2_miss(zz):
    return jnp.maximum(jnp.log(1.0 + jnp.exp2(jnp.minimum(zz, EXP2_CLAMP))) * LOG2_E, zz)


def _sb_incl():
    row = lax.broadcasted_iota(jnp.int32, (TQ, TQ), 0)
    col = lax.broadcasted_iota(jnp.int32, (TQ, TQ), 1)
    return (row >= col).astype(_BF16)


def _sb_near_blocks(q, k_all, vm_all, step, qs_ref, nc_ref, acc_ref):
    n_heads = 2 * SB_GROUP
    width = SB_GROUP * LANES
    row = lax.broadcasted_iota(jnp.int32, (n_heads * TQ, TQ), 0) & (TQ - 1)
    col = lax.broadcasted_iota(jnp.int32, (n_heads * TQ, TQ), 1)
    causal = col < row
    incl = _sb_incl()
    keys = pl.ds(pl.multiple_of(step * TQ, TQ), 2 * TQ)
    head_of_lane = lax.broadcasted_iota(jnp.int32, (1, width), 1) // SB_HEAD_DIM
    zero_v = jnp.zeros((2 * TQ, LANES), _BF16)
    for group in range(SB_PAIRS // SB_GROUP):
        cols = slice(group * width, (group + 1) * width)
        q_group = q[:, cols]
        qs = jnp.concatenate([jnp.where(head_of_lane == hd, q_group, 0) for hd in range(n_heads)], axis=0)
        zz = _dot_nt(qs, k_all[keys, cols])
        zz_prev, zz_own = zz[:, :TQ], zz[:, TQ:]
        nl_prev = _sb_neg_log2_miss(zz_prev)
        nl_own = jnp.where(causal, _sb_neg_log2_miss(zz_own), 0.0)
        tot_own = _dot(nl_own.astype(_BF16), incl)
        tot_prev = _dot(nl_prev.astype(_BF16), incl) + tot_own[:, 0:1]
        w_prev = jnp.exp2(zz_prev - tot_prev)
        w_own = jnp.where(causal, jnp.exp2(zz_own - tot_own), 0.0)
        carried = jnp.broadcast_to(tot_prev[:, 0:1], (n_heads * TQ, LANES))
        w_cat = jnp.concatenate(
            [blk[hd * TQ:(hd + 1) * TQ] for hd in range(n_heads) for blk in (w_prev, w_own)], axis=1).astype(_BF16)
        v_cat = jnp.concatenate([
            jnp.concatenate([vm_all[hd % 2, keys, (group * SB_GROUP + p) * LANES:(group * SB_GROUP + p + 1) * LANES]
                             if p == hd // 2 else zero_v for p in range(SB_GROUP)], axis=1)
            for hd in range(n_heads)], axis=0)
        out = _dot(w_cat, v_cat)
        for p in range(SB_GROUP):
            pair = group * SB_GROUP + p
            heads = slice(2 * p * TQ, (2 * p + 2) * TQ)
            qs_ref[pair] = qs[heads, p * LANES:(p + 1) * LANES]
            nc_ref[pair] = carried[heads]
            acc_ref[pair] = out[:, p * LANES:(p + 1) * LANES]


def _sb_far_blocks(k_all, vm_all, step, qs_ref, nc_ref, acc_ref):
    def more_blocks(carry):
        t, live = carry
        return jnp.logical_and(t < step - 1, live)

    def earlier_block(carry):
        t, _ = carry
        keys = pl.ds(pl.multiple_of((step - 1 - t) * TQ, TQ), TQ)
        incl = _sb_incl()
        for pair in range(SB_PAIRS):
            cols = slice(pair * LANES, (pair + 1) * LANES)
            zz = _dot_nt(qs_ref[pair], k_all[keys, cols])
            nl = _sb_neg_log2_miss(zz)
            tot = _dot(nl.astype(_BF16), incl) + jnp.concatenate([nc_ref[pair]] * (TQ // LANES), axis=1)
            w = jnp.exp2(zz - tot)
            w_cat = jnp.concatenate([w[:TQ], w[TQ:]], axis=1).astype(_BF16)
            v_cat = jnp.concatenate([vm_all[0, keys, cols], vm_all[1, keys, cols]], axis=0)
            acc_ref[pair] += _dot(w_cat, v_cat)
            nc_ref[pair] = jnp.broadcast_to(tot[:, 0:1], (2 * TQ, LANES))
        return t + 1, jnp.min(nc_ref[...]) < SB_DEAD_LOG2

    lax.while_loop(more_blocks, earlier_block, (jnp.int32(0), jnp.min(nc_ref[...]) < SB_DEAD_LOG2))


def _chunk_block_diag(x):
    zero = jnp.zeros((CHUNK, LANES), x.dtype)
    out_rows = []
    for r in range(x.shape[0] // CHUNK):
        blocks = [x[r * CHUNK:(r + 1) * CHUNK] if c == r % N_CHUNKS else zero for c in range(N_CHUNKS)]
        out_rows.append(jnp.concatenate(blocks, axis=1))
    return jnp.concatenate(out_rows, axis=0)


def _chunk_prefix_sum(x):
    sub = lax.broadcasted_iota(jnp.int32, (SUBLANES, 1), 0)
    groups = []
    for g in range(TQ // SUBLANES):
        blk = x[g * SUBLANES:(g + 1) * SUBLANES]
        for shift in (1, 2, 4):
            blk = blk + jnp.where(sub >= shift, pltpu.roll(blk, shift, axis=0), 0.0)
        if g % (CHUNK // SUBLANES):
            blk = blk + jnp.broadcast_to(groups[-1][SUBLANES - 1:], blk.shape)
        groups.append(blk)
    return jnp.concatenate(groups, axis=0)


def _gla(q, k, v, log_alpha, state_ref, gain):
    cum = _chunk_prefix_sum(log_alpha)
    chunk_total = [cum[(c + 1) * CHUNK - 1:(c + 1) * CHUNK] for c in range(N_CHUNKS)]
    cum_last = jnp.concatenate([jnp.broadcast_to(t, (CHUNK, GLA_K_WIDTH)) for t in chunk_total], axis=0)
    q_dec = (q * jnp.exp(cum) * (GLA_KEY_DIM ** -0.5)).astype(_BF16)
    k_inv = (k * jnp.exp(-cum)).astype(_BF16)
    k_end = (k * jnp.exp(cum_last - cum)).astype(_BF16)
    decay = [jnp.exp(t) for t in chunk_total]
    v_bf = v.astype(_BF16)

    row_all = lax.broadcasted_iota(jnp.int32, (GLA_HEADS * TQ, TQ), 0) & (TQ - 1)
    col_all = lax.broadcasted_iota(jnp.int32, (GLA_HEADS * TQ, TQ), 1)
    causal_all = jnp.logical_and(row_all // CHUNK == col_all // CHUNK, col_all <= row_all)
    head_of_lane = lax.broadcasted_iota(jnp.int32, (1, GLA_K_WIDTH), 1) // GLA_KEY_DIM
    qs_all = jnp.concatenate([jnp.where(head_of_lane == hd, q_dec, 0) for hd in range(GLA_HEADS)], axis=0)
    scores_all = jnp.where(causal_all, _dot_nt(qs_all, k_inv), 0.0).astype(_BF16)
    first_head = lax.broadcasted_iota(jnp.int32, (1, LANES), 1) < GLA_KEY_DIM
    first_head_rows = lax.broadcasted_iota(jnp.int32, (LANES, 1), 0) < GLA_KEY_DIM
    zero_v = jnp.zeros((TQ, GLA_VAL_DIM), _BF16)
    outs = []
    for pair in range(GLA_HEADS // 2):
        cols = slice(pair * LANES, (pair + 1) * LANES)
        v_pair = v_bf[:, 2 * pair * GLA_VAL_DIM:(2 * pair + 2) * GLA_VAL_DIM]
        scores = jnp.concatenate([scores_all[2 * pair * TQ:(2 * pair + 1) * TQ],
                                  scores_all[(2 * pair + 1) * TQ:(2 * pair + 2) * TQ]], axis=1)
        v_diag = jnp.concatenate([jnp.concatenate([v_pair[:, :GLA_VAL_DIM], zero_v], axis=1),
                                  jnp.concatenate([zero_v, v_pair[:, GLA_VAL_DIM:]], axis=1)], axis=0)
        kv_all = lax.dot_general(v_pair, _chunk_block_diag(k_end[:, cols]), _TN, preferred_element_type=_F32)
        states = [state_ref[pair]]
        for c in range(N_CHUNKS):
            blk = slice(c * LANES, (c + 1) * LANES)
            kv = jnp.where(first_head, kv_all[:GLA_VAL_DIM, blk], kv_all[GLA_VAL_DIM:, blk])
            states.append(states[-1] * decay[c][:, cols] + kv)
        state_ref[pair] = states[-1]
        state_rows = []
        for c in range(N_CHUNKS):
            turned = states[c].T
            state_rows.append(jnp.concatenate([jnp.where(first_head_rows, turned, 0.0),
                                               jnp.where(first_head_rows, 0.0, turned)], axis=1))
        rhs = jnp.concatenate([v_diag, jnp.concatenate(state_rows, axis=0).astype(_BF16)], axis=0)
        lhs = jnp.concatenate([scores, _chunk_block_diag(q_dec[:, cols])], axis=1)
        out_pair = _dot(lhs, rhs)
        outs.append(_rms(out_pair[:, :GLA_VAL_DIM], gain))
        outs.append(_rms(out_pair[:, GLA_VAL_DIM:], gain))
    return jnp.concatenate(outs, axis=1)


def _load_weights(w_in_t_hbm, w_up_ref, w_o_sb_hbm, w_o_gla_hbm, w_out_hbm, w_in_ref, w_alpha_ref, w_gate_ref,
                  w_o_sb_ref, w_o_gla_ref, w_out_ref, stage_ref, sem_ref):
    rows = WEIGHT_CHUNK_ROWS
    jobs = []

    def into_columns(dst, block):
        def consume(chunk):
            dst[:, block * rows:(block + 1) * rows] = chunk.T.astype(_BF16)
        return consume

    def into_rows(dst, block):
        def consume(chunk):
            dst[block * rows:(block + 1) * rows, :] = chunk.astype(_BF16)
        return consume

    def low_rank(chunk):
        lane = lax.broadcasted_iota(jnp.int32, (1, LR_PAD), 1)
        w_lr = jnp.where(lane < GLA_RANK, chunk[:LR_PAD].T, 0.0)
        w_alpha_ref[...] = jnp.dot(w_lr, w_up_ref[...], precision=lax.Precision.HIGHEST,
                                   preferred_element_type=_F32).astype(_BF16)

    w_in_t = w_in_t_hbm.at[0]
    for block in range(OFF_LR // rows):
        jobs.append((w_in_t, block * rows, into_columns(w_in_ref, block)))
    jobs.append((w_in_t, OFF_LR, low_rank))
    for block in range(2 * D_MODEL // rows):
        jobs.append((w_in_t, OFF_LR + GLA_RANK + block * rows, into_columns(w_gate_ref, block)))
    for src, dst in ((w_o_sb_hbm, w_o_sb_ref), (w_o_gla_hbm, w_o_gla_ref), (w_out_hbm, w_out_ref)):
        for block in range(dst.shape[0] // rows):
            jobs.append((src.at[0], block * rows, into_rows(dst, block)))

    n_buf = stage_ref.shape[0]

    def copy(j):
        src, row0, _ = jobs[j]
        return pltpu.make_async_copy(src.at[pl.ds(row0, rows), :], stage_ref.at[j % n_buf], sem_ref.at[j % n_buf])

    for j in range(min(n_buf - 1, len(jobs))):
        copy(j).start()
    for j in range(len(jobs)):
        if j + n_buf - 1 < len(jobs):
            copy(j + n_buf - 1).start()
        copy(j).wait()
        jobs[j][2](stage_ref[j % n_buf])


def _block_kernel(x_ref, g_pre_ref, w_in_t_hbm, b_gate_ref, w_up_ref, b_up_ref, g_gla_ref,
                  w_o_sb_hbm, w_o_gla_hbm, w_out_hbm, g_final_ref, out_ref,
                  w_in_ref, w_alpha_ref, w_gate_ref, w_o_sb_ref, w_o_gla_ref, w_out_ref, stage_ref, sem_ref,
                  k_all, vm_all, state_ref, qs_ref, nc_ref, acc_ref):
    step = pl.program_id(1)

    @pl.when(jnp.logical_and(pl.program_id(0) == 0, step == 0))
    def _():
        _load_weights(w_in_t_hbm, w_up_ref, w_o_sb_hbm, w_o_gla_hbm, w_out_hbm, w_in_ref, w_alpha_ref, w_gate_ref,
                      w_o_sb_ref, w_o_gla_ref, w_out_ref, stage_ref, sem_ref)

    @pl.when(step == 0)
    def _():
        state_ref[...] = jnp.zeros_like(state_ref)
        k_all[0:TQ, :] = jnp.zeros((TQ, SB_WIDTH), _BF16)
        vm_all[:, 0:TQ, :] = jnp.zeros((2, TQ, SB_WIDTH), _BF16)

    x = x_ref[0]
    h = (x * g_pre_ref[...]).astype(_BF16)
    inv_rms = jnp.broadcast_to(lax.rsqrt(jnp.mean(x * x, axis=-1, keepdims=True) + EPS), (STEP_ROWS, LANES))

    def scale_rows(p):
        return p * jnp.concatenate([inv_rms] * (p.shape[1] // LANES), axis=1)

    def proj(offset, width):
        return scale_rows(_dot(h, w_in_ref[:, offset:offset + width]))

    first_block = step * STEP_BLOCKS
    block_rows = [slice(s * TQ, (s + 1) * TQ) for s in range(STEP_BLOCKS)]
    rows = pl.ds(pl.multiple_of((first_block + 1) * TQ, TQ), STEP_ROWS)
    k_all[rows, :] = proj(OFF_SB_K, SB_WIDTH).astype(_BF16)
    sb_v = proj(OFF_SB_V, SB_WIDTH).astype(_BF16)
    first_head = (lax.broadcasted_iota(jnp.int32, (1, SB_WIDTH), 1) & (LANES - 1)) < SB_HEAD_DIM
    vm_all[0, rows, :] = jnp.where(first_head, sb_v, 0)
    vm_all[1, rows, :] = jnp.where(first_head, 0, sb_v)
    sb_q = (proj(OFF_SB_Q, SB_WIDTH) * (LOG2_E * SB_HEAD_DIM ** -0.5)).astype(_BF16)
    for s in range(STEP_BLOCKS):
        _sb_near_blocks(sb_q[block_rows[s]], k_all, vm_all, first_block + s, qs_ref.at[s], nc_ref.at[s], acc_ref.at[s])

    alpha_logit = scale_rows(_dot(h, w_alpha_ref[...])) + b_up_ref[...]
    log_alpha = _log_sigmoid(alpha_logit) / GLA_GATE_TEMP
    gla_q, gla_k, gla_v = proj(OFF_GLA_Q, GLA_K_WIDTH), proj(OFF_GLA_K, GLA_K_WIDTH), proj(OFF_GLA_V, GLA_V_WIDTH)
    o_gla = jnp.concatenate(
        [_gla(gla_q[r], gla_k[r], gla_v[r], log_alpha[r], state_ref, g_gla_ref[...]) for r in block_rows], axis=0)
    y_gla = _dot((o_gla * _silu(proj(OFF_GLA_Z, GLA_V_WIDTH))).astype(_BF16), w_o_gla_ref[...])
    gates = jax.nn.sigmoid(scale_rows(_dot(h, w_gate_ref[...])) + b_gate_ref[...])
    gated_gla = gates[:, D_MODEL:] * y_gla
    gate_sb = gates[:, :D_MODEL]
    sb_zs = _silu(proj(OFF_SB_Z, SB_WIDTH))

    def finish():
        o_sb = jnp.concatenate([jnp.concatenate([acc_ref[s, pair] for pair in range(SB_PAIRS)], axis=1)
                                for s in range(STEP_BLOCKS)], axis=0)
        y_sb = _dot((o_sb * sb_zs).astype(_BF16), w_o_sb_ref[...])
        merged = gate_sb * y_sb + gated_gla
        res = x + _dot(merged.astype(_BF16), w_out_ref[...])
        out_ref[0] = _rms(res, g_final_ref[...])

    finish()

    @pl.when(jnp.logical_and(step >= 1, jnp.min(nc_ref[...]) < SB_DEAD_LOG2))
    def _():
        for s in range(STEP_BLOCKS):
            _sb_far_blocks(k_all, vm_all, first_block + s, qs_ref.at[s], nc_ref.at[s], acc_ref.at[s])
        finish()


def _resident(shape):
    return pl.BlockSpec(shape, lambda b, i: (0,) * len(shape), pipeline_mode=pl.Buffered(1))


@jax.jit
def kernel(x, g_pre, w_in, b_gate, w_alpha_up, b_alpha_up, g_gla_norm, w_o_sb, w_o_gla, w_out, g_final):
    batch, seq, d_model = x.shape
    depth = w_in.shape[0]
    assert d_model == D_MODEL and seq % STEP_ROWS == 0 and depth == 1
    w_up = jnp.pad(w_alpha_up[0], ((0, LR_PAD - GLA_RANK), (0, 0)))
    w_in_t = jnp.transpose(w_in, (0, 2, 1))

    row_block = pl.BlockSpec((1, STEP_ROWS, D_MODEL), lambda b, i: (b, i, 0))
    in_hbm = pl.BlockSpec(memory_space=pl.ANY)
    return pl.pallas_call(
        _block_kernel,
        grid=(batch, seq // STEP_ROWS),
        in_specs=[
            row_block,
            _resident((1, D_MODEL)),
            in_hbm,
            _resident((1, 2 * D_MODEL)),
            _resident((LR_PAD, GLA_K_WIDTH)),
            _resident((1, GLA_K_WIDTH)),
            _resident((1, GLA_VAL_DIM)),
            in_hbm,
            in_hbm,
            in_hbm,
            _resident((1, D_MODEL)),
        ],
        out_specs=row_block,
        out_shape=jax.ShapeDtypeStruct(x.shape, x.dtype),
        scratch_shapes=[
            pltpu.VMEM((D_MODEL, OFF_LR), _BF16),
            pltpu.VMEM((D_MODEL, GLA_K_WIDTH), _BF16),
            pltpu.VMEM((D_MODEL, 2 * D_MODEL), _BF16),
            pltpu.VMEM((SB_WIDTH, D_MODEL), _BF16),
            pltpu.VMEM((GLA_V_WIDTH, D_MODEL), _BF16),
            pltpu.VMEM((D_MODEL, D_MODEL), _BF16),
            pltpu.VMEM((WEIGHT_BUFFERS, WEIGHT_CHUNK_ROWS, D_MODEL), _F32),
            pltpu.SemaphoreType.DMA((WEIGHT_BUFFERS,)),
            pltpu.VMEM((seq + TQ, SB_WIDTH), _BF16),
            pltpu.VMEM((2, seq + TQ, SB_WIDTH), _BF16),
            pltpu.VMEM((GLA_HEADS // 2, GLA_VAL_DIM, LANES), _F32),
            pltpu.VMEM((STEP_BLOCKS, SB_PAIRS, 2 * TQ, LANES), _BF16),
            pltpu.VMEM((STEP_BLOCKS, SB_PAIRS, 2 * TQ, LANES), _F32),
            pltpu.VMEM((STEP_BLOCKS, SB_PAIRS, TQ, LANES), _F32),
        ],
        compiler_params=pltpu.CompilerParams(
            dimension_semantics=("arbitrary", "arbitrary"), vmem_limit_bytes=VMEM_LIMIT_BYTES),
        name="hybrid_mixer_block",
    )(x, g_pre, w_in_t, b_gate, w_up, b_alpha_up, g_gla_norm, w_o_sb, w_o_gla, w_out, g_final.reshape(1, D_MODEL))
```

```python
import math

import jax
import jax.numpy as jnp
from jax import lax
from jax.experimental import pallas as pl
from jax.experimental.pallas import tpu as pltpu

D_MODEL = 1024
CHUNK = 64
SB_HEAD_DIM = 64
SB_WIDTH = D_MODEL // 2
GLA_HEADS = 4
GLA_K_WIDTH = D_MODEL // 4
GLA_V_WIDTH = D_MODEL // 2
GLA_KEY_DIM = GLA_K_WIDTH // GLA_HEADS
GLA_VAL_DIM = GLA_V_WIDTH // GLA_HEADS
GLA_RANK = 16
GLA_GATE_TEMP = 16.0
EPS = 1e-6
LOG2_E = math.log2(math.e)
EXP2_CLAMP = 126.0
SB_DEAD_LOG2 = 151.0

LANES = 128
SUBLANES = 8
SB_PAIRS = SB_WIDTH // LANES
SB_GROUP = 2
TQ = 256
STEP_BLOCKS = 2
STEP_ROWS = STEP_BLOCKS * TQ
N_CHUNKS = TQ // CHUNK
LR_PAD = LANES
WEIGHT_CHUNK_ROWS = 512
WEIGHT_BUFFERS = 3

OFF_SB_Q = 0
OFF_SB_K = OFF_SB_Q + SB_WIDTH
OFF_SB_V = OFF_SB_K + SB_WIDTH
OFF_SB_Z = OFF_SB_V + SB_WIDTH
OFF_GLA_Q = OFF_SB_Z + SB_WIDTH
OFF_GLA_K = OFF_GLA_Q + GLA_K_WIDTH
OFF_GLA_V = OFF_GLA_K + GLA_K_WIDTH
OFF_GLA_Z = OFF_GLA_V + GLA_V_WIDTH
OFF_LR = OFF_GLA_Z + GLA_V_WIDTH

VMEM_LIMIT_BYTES = 56 * 1024 * 1024

_F32 = jnp.float32
_BF16 = jnp.bfloat16
_NT = (((1,), (1,)), ((), ()))


def _dot(a, b):
    return jnp.dot(a, b, preferred_element_type=_F32)


def _dot_nt(a, b):
    return lax.dot_general(a, b, _NT, preferred_element_type=_F32)


def _log_sigmoid(x):
    return jnp.minimum(x, 0.0) - jnp.log(1.0 + jnp.exp(-jnp.abs(x)))


def _silu(x):
    return x * jax.nn.sigmoid(x)


def _rms(x, gain):
    return x * lax.rsqrt(jnp.mean(x * x, axis=-1, keepdims=True) + EPS) * gain


def _sb_neg_log2_miss(zz):
    return jnp.maximum(jnp.log(1.0 + jnp.exp2(jnp.minimum(zz, EXP2_CLAMP))) * LOG2_E, zz)


def _sb_incl():
    row = lax.broadcasted_iota(jnp.int32, (TQ, TQ), 0)
    col = lax.broadcasted_iota(jnp.int32, (TQ, TQ), 1)
    return (row >= col).astype(_BF16)


def _sb_near_blocks(q, k_all, vm_all, step, qs_ref, nc_ref, acc_ref):
    n_heads = 2 * SB_GROUP
    width = SB_GROUP * LANES
    row = lax.broadcasted_iota(jnp.int32, (n_heads * TQ, TQ), 0) & (TQ - 1)
    col = lax.broadcasted_iota(jnp.int32, (n_heads * TQ, TQ), 1)
    causal = col < row
    incl = _sb_incl()
    keys = pl.ds(pl.multiple_of(step * TQ, TQ), 2 * TQ)
    head_of_lane = lax.broadcasted_iota(jnp.int32, (1, width), 1) // SB_HEAD_DIM
    zero_v = jnp.zeros((2 * TQ, LANES), _BF16)
    for group in range(SB_PAIRS // SB_GROUP):
        cols = slice(group * width, (group + 1) * width)
        q_group = q[:, cols]
        qs = jnp.concatenate([jnp.where(head_of_lane == hd, q_group, 0) for hd in range(n_heads)], axis=0)
        zz = _dot_nt(qs, k_all[keys, cols])
        zz_prev, zz_own = zz[:, :TQ], zz[:, TQ:]
        nl_prev = _sb_neg_log2_miss(zz_prev)
        nl_own = jnp.where(causal, _sb_neg_log2_miss(zz_own), 0.0)
        tot_own = _dot(nl_own.astype(_BF16), incl)
        tot_prev = _dot(nl_prev.astype(_BF16), incl) + tot_own[:, 0:1]
        w_prev = jnp.exp2(zz_prev - tot_prev)
        w_own = jnp.where(causal, jnp.exp2(zz_own - tot_own), 0.0)
        carried = jnp.broadcast_to(tot_prev[:, 0:1], (n_heads * TQ, LANES))
        w_cat = jnp.concatenate(
            [blk[hd * TQ:(hd + 1) * TQ] for hd in range(n_heads) for blk in (w_prev, w_own)], axis=1).astype(_BF16)
        v_cat = jnp.concatenate([
            jnp.concatenate([vm_all[hd % 2, keys, (group * SB_GROUP + p) * LANES:(group * SB_GROUP + p + 1) * LANES]
                             if p == hd // 2 else zero_v for p in range(SB_GROUP)], axis=1)
            for hd in range(n_heads)], axis=0)
        out = _dot(w_cat, v_cat)
        for p in range(SB_GROUP):
            pair = group * SB_GROUP + p
            heads = slice(2 * p * TQ, (2 * p + 2) * TQ)
            qs_ref[pair] = qs[heads, p * LANES:(p + 1) * LANES]
            nc_ref[pair] = carried[heads]
            acc_ref[pair] = out[:, p * LANES:(p + 1) * LANES]


def _sb_far_blocks(k_all, vm_all, step, qs_ref, nc_ref, acc_ref):
    def more_blocks(carry):
        t, live = carry
        return jnp.logical_and(t < step - 1, live)

    def earlier_block(carry):
        t, _ = carry
        keys = pl.ds(pl.multiple_of((step - 1 - t) * TQ, TQ), TQ)
        incl = _sb_incl()
        for pair in range(SB_PAIRS):
            cols = slice(pair * LANES, (pair + 1) * LANES)
            zz = _dot_nt(qs_ref[pair], k_all[keys, cols])
            nl = _sb_neg_log2_miss(zz)
            tot = _dot(nl.astype(_BF16), incl) + jnp.concatenate([nc_ref[pair]] * (TQ // LANES), axis=1)
            w = jnp.exp2(zz - tot)
            w_cat = jnp.concatenate([w[:TQ], w[TQ:]], axis=1).astype(_BF16)
            v_cat = jnp.concatenate([vm_all[0, keys, cols], vm_all[1, keys, cols]], axis=0)
            acc_ref[pair] += _dot(w_cat, v_cat)
            nc_ref[pair] = jnp.broadcast_to(tot[:, 0:1], (2 * TQ, LANES))
        return t + 1, jnp.min(nc_ref[...]) < SB_DEAD_LOG2

    lax.while_loop(more_blocks, earlier_block, (jnp.int32(0), jnp.min(nc_ref[...]) < SB_DEAD_LOG2))


def _chunk_block_diag(x):
    zero = jnp.zeros((CHUNK, LANES), x.dtype)
    out_rows = []
    for r in range(x.shape[0] // CHUNK):
        blocks = [x[r * CHUNK:(r + 1) * CHUNK] if c == r % N_CHUNKS else zero for c in range(N_CHUNKS)]
        out_rows.append(jnp.concatenate(blocks, axis=1))
    return jnp.concatenate(out_rows, axis=0)


def _chunk_prefix_sum(x):
    sub = lax.broadcasted_iota(jnp.int32, (SUBLANES, 1), 0)
    groups = []
    for g in range(TQ // SUBLANES):
        blk = x[g * SUBLANES:(g + 1) * SUBLANES]
        for shift in (1, 2, 4):
            blk = blk + jnp.where(sub >= shift, pltpu.roll(blk, shift, axis=0), 0.0)
        if g % (CHUNK // SUBLANES):
            blk = blk + jnp.broadcast_to(groups[-1][SUBLANES - 1:], blk.shape)
        groups.append(blk)
    return jnp.concatenate(groups, axis=0)


def _gla(q, k, v, log_alpha, state_ref, gain):
    cum = _chunk_prefix_sum(log_alpha)
    chunk_total = [cum[(c + 1) * CHUNK - 1:(c + 1) * CHUNK] for c in range(N_CHUNKS)]
    cum_last = jnp.concatenate([jnp.broadcast_to(t, (CHUNK, GLA_K_WIDTH)) for t in chunk_total], axis=0)
    q_dec = (q * jnp.exp(cum) * (GLA_KEY_DIM ** -0.5)).astype(_BF16)
    k_inv = (k * jnp.exp(-cum)).astype(_BF16)
    k_end_f32 = k * jnp.exp(cum_last - cum)
    chunk_of_time = lax.broadcasted_iota(jnp.int32, (1, TQ), 1) // CHUNK
    v_bf = v.astype(_BF16)

    row_all = lax.broadcasted_iota(jnp.int32, (GLA_HEADS * TQ, TQ), 0) & (TQ - 1)
    col_all = lax.broadcasted_iota(jnp.int32, (GLA_HEADS * TQ, TQ), 1)
    causal_all = jnp.logical_and(row_all // CHUNK == col_all // CHUNK, col_all <= row_all)
    head_of_lane = lax.broadcasted_iota(jnp.int32, (1, GLA_K_WIDTH), 1) // GLA_KEY_DIM
    qs_all = jnp.concatenate([jnp.where(head_of_lane == hd, q_dec, 0) for hd in range(GLA_HEADS)], axis=0)
    scores_all = jnp.where(causal_all, _dot_nt(qs_all, k_inv), 0.0).astype(_BF16)
    first_head_rows = lax.broadcasted_iota(jnp.int32, (LANES, 1), 0) < GLA_KEY_DIM
    zero_v = jnp.zeros((TQ, GLA_VAL_DIM), _BF16)
    outs = []
    for pair in range(GLA_HEADS // 2):
        cols = slice(pair * LANES, (pair + 1) * LANES)
        v_pair = v_bf[:, 2 * pair * GLA_VAL_DIM:(2 * pair + 2) * GLA_VAL_DIM]
        scores = jnp.concatenate([scores_all[2 * pair * TQ:(2 * pair + 1) * TQ],
                                  scores_all[(2 * pair + 1) * TQ:(2 * pair + 2) * TQ]], axis=1)
        v_diag = jnp.concatenate([jnp.concatenate([v_pair[:, :GLA_VAL_DIM], zero_v], axis=1),
                                  jnp.concatenate([zero_v, v_pair[:, GLA_VAL_DIM:]], axis=1)], axis=0)
        k_end_t = k_end_f32[:, cols].T
        kv_lhs = jnp.concatenate([jnp.where(chunk_of_time == c, k_end_t, 0.0) for c in range(N_CHUNKS)],
                                 axis=0).astype(_BF16)
        kv_all = _dot(kv_lhs, v_pair)
        totals = jnp.concatenate([jnp.broadcast_to(t[:, cols], (LANES // N_CHUNKS, LANES)) for t in chunk_total],
                                 axis=0).T
        states = [state_ref[pair]]
        for c in range(N_CHUNKS):
            blk = slice(c * LANES, (c + 1) * LANES)
            kv = jnp.where(first_head_rows, kv_all[blk, :GLA_VAL_DIM], kv_all[blk, GLA_VAL_DIM:])
            decay_col = jnp.exp(totals[:, c * (LANES // N_CHUNKS):c * (LANES // N_CHUNKS) + 1])
            states.append(states[-1] * decay_col + kv)
        state_ref[pair] = states[-1]
        state_rows = []
        for c in range(N_CHUNKS):
            state_rows.append(jnp.concatenate([jnp.where(first_head_rows, states[c], 0.0),
                                               jnp.where(first_head_rows, 0.0, states[c])], axis=1))
        rhs = jnp.concatenate([v_diag, jnp.concatenate(state_rows, axis=0).astype(_BF16)], axis=0)
        lhs = jnp.concatenate([scores, _chunk_block_diag(q_dec[:, cols])], axis=1)
        out_pair = _dot(lhs, rhs)
        outs.append(_rms(out_pair[:, :GLA_VAL_DIM], gain))
        outs.append(_rms(out_pair[:, GLA_VAL_DIM:], gain))
    return jnp.concatenate(outs, axis=1)


def _load_weights(w_in_t_hbm, w_up_ref, w_o_sb_hbm, w_o_gla_hbm, w_out_hbm, w_in_ref, w_alpha_ref, w_gate_ref,
                  w_o_sb_ref, w_o_gla_ref, w_out_ref, stage_ref, sem_ref):
    rows = WEIGHT_CHUNK_ROWS
    jobs = []

    def into_columns(dst, block):
        def consume(chunk):
            dst[:, block * rows:(block + 1) * rows] = chunk.T.astype(_BF16)
        return consume

    def into_rows(dst, block):
        def consume(chunk):
            dst[block * rows:(block + 1) * rows, :] = chunk.astype(_BF16)
        return consume

    def low_rank(chunk):
        lane = lax.broadcasted_iota(jnp.int32, (1, LR_PAD), 1)
        w_lr = jnp.where(lane < GLA_RANK, chunk[:LR_PAD].T, 0.0)
        w_alpha_ref[...] = jnp.dot(w_lr, w_up_ref[...], precision=lax.Precision.HIGHEST,
                                   preferred_element_type=_F32).astype(_BF16)

    w_in_t = w_in_t_hbm.at[0]
    for block in range(OFF_LR // rows):
        jobs.append((w_in_t, block * rows, into_columns(w_in_ref, block)))
    jobs.append((w_in_t, OFF_LR, low_rank))
    for block in range(2 * D_MODEL // rows):
        jobs.append((w_in_t, OFF_LR + GLA_RANK + block * rows, into_columns(w_gate_ref, block)))
    for src, dst in ((w_o_sb_hbm, w_o_sb_ref), (w_o_gla_hbm, w_o_gla_ref), (w_out_hbm, w_out_ref)):
        for block in range(dst.shape[0] // rows):
            jobs.append((src.at[0], block * rows, into_rows(dst, block)))

    n_buf = stage_ref.shape[0]

    def copy(j):
        src, row0, _ = jobs[j]
        return pltpu.make_async_copy(src.at[pl.ds(row0, rows), :], stage_ref.at[j % n_buf], sem_ref.at[j % n_buf])

    for j in range(min(n_buf - 1, len(jobs))):
        copy(j).start()
    for j in range(len(jobs)):
        if j + n_buf - 1 < len(jobs):
            copy(j + n_buf - 1).start()
        copy(j).wait()
        jobs[j][2](stage_ref[j % n_buf])


def _block_kernel(x_ref, g_pre_ref, w_in_t_hbm, b_gate_ref, w_up_ref, b_up_ref, g_gla_ref,
                  w_o_sb_hbm, w_o_gla_hbm, w_out_hbm, g_final_ref, out_ref,
                  w_in_ref, w_alpha_ref, w_gate_ref, w_o_sb_ref, w_o_gla_ref, w_out_ref, stage_ref, sem_ref,
                  k_all, vm_all, state_ref, qs_ref, nc_ref, acc_ref):
    step = pl.program_id(1)

    @pl.when(jnp.logical_and(pl.program_id(0) == 0, step == 0))
    def _():
        _load_weights(w_in_t_hbm, w_up_ref, w_o_sb_hbm, w_o_gla_hbm, w_out_hbm, w_in_ref, w_alpha_ref, w_gate_ref,
                      w_o_sb_ref, w_o_gla_ref, w_out_ref, stage_ref, sem_ref)

    @pl.when(step == 0)
    def _():
        state_ref[...] = jnp.zeros_like(state_ref)
        k_all[0:TQ, :] = jnp.zeros((TQ, SB_WIDTH), _BF16)
        vm_all[:, 0:TQ, :] = jnp.zeros((2, TQ, SB_WIDTH), _BF16)

    x = x_ref[0]
    h = (x * g_pre_ref[...]).astype(_BF16)
    inv_rms = jnp.broadcast_to(lax.rsqrt(jnp.mean(x * x, axis=-1, keepdims=True) + EPS), (STEP_ROWS, LANES))

    def scale_rows(p):
        return p * jnp.concatenate([inv_rms] * (p.shape[1] // LANES), axis=1)

    def proj(offset, width):
        return scale_rows(_dot(h, w_in_ref[:, offset:offset + width]))

    first_block = step * STEP_BLOCKS
    block_rows = [slice(s * TQ, (s + 1) * TQ) for s in range(STEP_BLOCKS)]
    rows = pl.ds(pl.multiple_of((first_block + 1) * TQ, TQ), STEP_ROWS)
    k_all[rows, :] = proj(OFF_SB_K, SB_WIDTH).astype(_BF16)
    sb_v = proj(OFF_SB_V, SB_WIDTH).astype(_BF16)
    first_head = (lax.broadcasted_iota(jnp.int32, (1, SB_WIDTH), 1) & (LANES - 1)) < SB_HEAD_DIM
    vm_all[0, rows, :] = jnp.where(first_head, sb_v, 0)
    vm_all[1, rows, :] = jnp.where(first_head, 0, sb_v)
    sb_q = (proj(OFF_SB_Q, SB_WIDTH) * (LOG2_E * SB_HEAD_DIM ** -0.5)).astype(_BF16)
    for s in range(STEP_BLOCKS):
        _sb_near_blocks(sb_q[block_rows[s]], k_all, vm_all, first_block + s, qs_ref.at[s], nc_ref.at[s], acc_ref.at[s])

    alpha_logit = scale_rows(_dot(h, w_alpha_ref[...])) + b_up_ref[...]
    log_alpha = _log_sigmoid(alpha_logit) / GLA_GATE_TEMP
    gla_q, gla_k, gla_v = proj(OFF_GLA_Q, GLA_K_WIDTH), proj(OFF_GLA_K, GLA_K_WIDTH), proj(OFF_GLA_V, GLA_V_WIDTH)
    o_gla = jnp.concatenate(
        [_gla(gla_q[r], gla_k[r], gla_v[r], log_alpha[r], state_ref, g_gla_ref[...]) for r in block_rows], axis=0)
    y_gla = _dot((o_gla * _silu(proj(OFF_GLA_Z, GLA_V_WIDTH))).astype(_BF16), w_o_gla_ref[...])
    gates = jax.nn.sigmoid(scale_rows(_dot(h, w_gate_ref[...])) + b_gate_ref[...])
    gated_gla = gates[:, D_MODEL:] * y_gla
    gate_sb = gates[:, :D_MODEL]
    sb_zs = _silu(proj(OFF_SB_Z, SB_WIDTH))

    def finish():
        o_sb = jnp.concatenate([jnp.concatenate([acc_ref[s, pair] for pair in range(SB_PAIRS)], axis=1)
                                for s in range(STEP_BLOCKS)], axis=0)
        y_sb = _dot((o_sb * sb_zs).astype(_BF16), w_o_sb_ref[...])
        merged = gate_sb * y_sb + gated_gla
        res = x + _dot(merged.astype(_BF16), w_out_ref[...])
        out_ref[0] = _rms(res, g_final_ref[...])

    finish()

    @pl.when(jnp.logical_and(step >= 1, jnp.min(nc_ref[...]) < SB_DEAD_LOG2))
    def _():
        for s in range(STEP_BLOCKS):
            _sb_far_blocks(k_all, vm_all, first_block + s, qs_ref.at[s], nc_ref.at[s], acc_ref.at[s])
        finish()


def _resident(shape):
    return pl.BlockSpec(shape, lambda b, i: (0,) * len(shape), pipeline_mode=pl.Buffered(1))


@jax.jit
def kernel(x, g_pre, w_in, b_gate, w_alpha_up, b_alpha_up, g_gla_norm, w_o_sb, w_o_gla, w_out, g_final):
    batch, seq, d_model = x.shape
    depth = w_in.shape[0]
    assert d_model == D_MODEL and seq % STEP_ROWS == 0 and depth == 1
    w_up = jnp.pad(w_alpha_up[0], ((0, LR_PAD - GLA_RANK), (0, 0)))
    w_in_t = jnp.transpose(w_in, (0, 2, 1))

    row_block = pl.BlockSpec((1, STEP_ROWS, D_MODEL), lambda b, i: (b, i, 0))
    in_hbm = pl.BlockSpec(memory_space=pl.ANY)
    return pl.pallas_call(
        _block_kernel,
        grid=(batch, seq // STEP_ROWS),
        in_specs=[
            row_block,
            _resident((1, D_MODEL)),
            in_hbm,
            _resident((1, 2 * D_MODEL)),
            _resident((LR_PAD, GLA_K_WIDTH)),
            _resident((1, GLA_K_WIDTH)),
            _resident((1, GLA_VAL_DIM)),
            in_hbm,
            in_hbm,
            in_hbm,
            _resident((1, D_MODEL)),
        ],
        out_specs=row_block,
        out_shape=jax.ShapeDtypeStruct(x.shape, x.dtype),
        scratch_shapes=[
            pltpu.VMEM((D_MODEL, OFF_LR), _BF16),
            pltpu.VMEM((D_MODEL, GLA_K_WIDTH), _BF16),
            pltpu.VMEM((D_MODEL, 2 * D_MODEL), _BF16),
            pltpu.VMEM((SB_WIDTH, D_MODEL), _BF16),
            pltpu.VMEM((GLA_V_WIDTH, D_MODEL), _BF16),
            pltpu.VMEM((D_MODEL, D_MODEL), _BF16),
            pltpu.VMEM((WEIGHT_BUFFERS, WEIGHT_CHUNK_ROWS, D_MODEL), _F32),
            pltpu.SemaphoreType.DMA((WEIGHT_BUFFERS,)),
            pltpu.VMEM((seq + TQ, SB_WIDTH), _BF16),
            pltpu.VMEM((2, seq + TQ, SB_WIDTH), _BF16),
            pltpu.VMEM((GLA_HEADS // 2, GLA_VAL_DIM, LANES), _F32),
            pltpu.VMEM((STEP_BLOCKS, SB_PAIRS, 2 * TQ, LANES), _BF16),
            pltpu.VMEM((STEP_BLOCKS, SB_PAIRS, 2 * TQ, LANES), _F32),
            pltpu.VMEM((STEP_BLOCKS, SB_PAIRS, TQ, LANES), _F32),
        ],
        compiler_params=pltpu.CompilerParams(
            dimension_semantics=("arbitrary", "arbitrary"), vmem_limit_bytes=VMEM_LIMIT_BYTES),
        name="hybrid_mixer_block",
    )(x, g_pre, w_in_t, b_gate, w_up, b_alpha_up, g_gla_norm, w_o_sb, w_o_gla, w_out, g_final.reshape(1, D_MODEL))
```

```python
import math

import jax
import jax.numpy as jnp
from jax import lax
from jax.experimental import pallas as pl
from jax.experimental.pallas import tpu as pltpu

D_MODEL = 1024
CHUNK = 64
SB_HEAD_DIM = 64
SB_WIDTH = D_MODEL // 2
GLA_HEADS = 4
GLA_K_WIDTH = D_MODEL // 4
GLA_V_WIDTH = D_MODEL // 2
GLA_KEY_DIM = GLA_K_WIDTH // GLA_HEADS
GLA_VAL_DIM = GLA_V_WIDTH // GLA_HEADS
GLA_RANK = 16
GLA_GATE_TEMP = 16.0
EPS = 1e-6
LOG2_E = math.log2(math.e)
EXP2_CLAMP = 126.0
SB_DEAD_LOG2 = 151.0

LANES = 128
SUBLANES = 8
SB_PAIRS = SB_WIDTH // LANES
SB_GROUP = 2
TQ = 256
STEP_BLOCKS = 2
STEP_ROWS = STEP_BLOCKS * TQ
N_CHUNKS = TQ // CHUNK
LR_PAD = LANES
WEIGHT_CHUNK_ROWS = 512
WEIGHT_BUFFERS = 3

OFF_SB_Q = 0
OFF_SB_K = OFF_SB_Q + SB_WIDTH
OFF_SB_V = OFF_SB_K + SB_WIDTH
OFF_SB_Z = OFF_SB_V + SB_WIDTH
OFF_GLA_Q = OFF_SB_Z + SB_WIDTH
OFF_GLA_K = OFF_GLA_Q + GLA_K_WIDTH
OFF_GLA_V = OFF_GLA_K + GLA_K_WIDTH
OFF_GLA_Z = OFF_GLA_V + GLA_V_WIDTH
OFF_LR = OFF_GLA_Z + GLA_V_WIDTH

VMEM_LIMIT_BYTES = 56 * 1024 * 1024

_F32 = jnp.float32
_BF16 = jnp.bfloat16
_NT = (((1,), (1,)), ((), ()))


def _dot(a, b):
    return jnp.dot(a, b, preferred_element_type=_F32)


def _dot_nt(a, b):
    return lax.dot_general(a, b, _NT, preferred_element_type=_F32)


def _log_sigmoid(x):
    return jnp.minimum(x, 0.0) - jnp.log(1.0 + jnp.exp(-jnp.abs(x)))


def _silu(x):
    return x * jax.nn.sigmoid(x)


def _rms(x, gain):
    return x * lax.rsqrt(jnp.mean(x * x, axis=-1, keepdims=True) + EPS) * gain


def _sb_neg_log2_miss(zz):
    return jnp.maximum(jnp.log(1.0 + jnp.exp2(jnp.minimum(zz, EXP2_CLAMP))) * LOG2_E, zz)


def _sb_incl():
    row = lax.broadcasted_iota(jnp.int32, (TQ, TQ), 0)
    col = lax.broadcasted_iota(jnp.int32, (TQ, TQ), 1)
    return (row >= col).astype(_BF16)


def _sb_near_blocks(q, k_all, vm_all, step, qs_ref, nc_ref, acc_ref):
    n_heads = 2 * SB_GROUP
    width = SB_GROUP * LANES
    row = lax.broadcasted_iota(jnp.int32, (n_heads * TQ, TQ), 0) & (TQ - 1)
    col = lax.broadcasted_iota(jnp.int32, (n_heads * TQ, TQ), 1)
    causal = col < row
    incl = _sb_incl()
    keys = pl.ds(pl.multiple_of(step * TQ, TQ), 2 * TQ)
    head_of_lane = lax.broadcasted_iota(jnp.int32, (1, width), 1) // SB_HEAD_DIM
    zero_v = jnp.zeros((2 * TQ, LANES), _BF16)
    for group in range(SB_PAIRS // SB_GROUP):
        cols = slice(group * width, (group + 1) * width)
        q_group = q[:, cols]
        qs = jnp.concatenate([jnp.where(head_of_lane == hd, q_group, 0) for hd in range(n_heads)], axis=0)
        zz = _dot_nt(qs, k_all[keys, cols])
        zz_prev, zz_own = zz[:, :TQ], zz[:, TQ:]
        nl_prev = _sb_neg_log2_miss(zz_prev)
        nl_own = jnp.where(causal, _sb_neg_log2_miss(zz_own), 0.0)
        tot_own = _dot(nl_own.astype(_BF16), incl)
        tot_prev = _dot(nl_prev.astype(_BF16), incl) + tot_own[:, 0:1]
        w_prev = jnp.exp2(zz_prev - tot_prev)
        w_own = jnp.where(causal, jnp.exp2(zz_own - tot_own), 0.0)
        carried = jnp.broadcast_to(tot_prev[:, 0:1], (n_heads * TQ, LANES))
        w_cat = jnp.concatenate(
            [blk[hd * TQ:(hd + 1) * TQ] for hd in range(n_heads) for blk in (w_prev, w_own)], axis=1).astype(_BF16)
        v_cat = jnp.concatenate([
            jnp.concatenate([vm_all[hd % 2, keys, (group * SB_GROUP + p) * LANES:(group * SB_GROUP + p + 1) * LANES]
                             if p == hd // 2 else zero_v for p in range(SB_GROUP)], axis=1)
            for hd in range(n_heads)], axis=0)
        out = _dot(w_cat, v_cat)
        for p in range(SB_GROUP):
            pair = group * SB_GROUP + p
            heads = slice(2 * p * TQ, (2 * p + 2) * TQ)
            qs_ref[pair] = qs[heads, p * LANES:(p + 1) * LANES]
            nc_ref[pair] = carried[heads]
            acc_ref[pair] = out[:, p * LANES:(p + 1) * LANES]


def _sb_far_blocks(k_all, vm_all, step, qs_ref, nc_ref, acc_ref):
    def more_blocks(carry):
        t, live = carry
        return jnp.logical_and(t < step - 1, live)

    def earlier_block(carry):
        t, _ = carry
        keys = pl.ds(pl.multiple_of((step - 1 - t) * TQ, TQ), TQ)
        incl = _sb_incl()
        for pair in range(SB_PAIRS):
            cols = slice(pair * LANES, (pair + 1) * LANES)
            zz = _dot_nt(qs_ref[pair], k_all[keys, cols])
            nl = _sb_neg_log2_miss(zz)
            tot = _dot(nl.astype(_BF16), incl) + jnp.concatenate([nc_ref[pair]] * (TQ // LANES), axis=1)
            w = jnp.exp2(zz - tot)
            w_cat = jnp.concatenate([w[:TQ], w[TQ:]], axis=1).astype(_BF16)
            v_cat = jnp.concatenate([vm_all[0, keys, cols], vm_all[1, keys, cols]], axis=0)
            acc_ref[pair] += _dot(w_cat, v_cat)
            nc_ref[pair] = jnp.broadcast_to(tot[:, 0:1], (2 * TQ, LANES))
        return t + 1, jnp.min(nc_ref[...]) < SB_DEAD_LOG2

    lax.while_loop(more_blocks, earlier_block, (jnp.int32(0), jnp.min(nc_ref[...]) < SB_DEAD_LOG2))


def _chunk_block_diag(x):
    zero = jnp.zeros((CHUNK, LANES), x.dtype)
    out_rows = []
    for r in range(x.shape[0] // CHUNK):
        blocks = [x[r * CHUNK:(r + 1) * CHUNK] if c == r % N_CHUNKS else zero for c in range(N_CHUNKS)]
        out_rows.append(jnp.concatenate(blocks, axis=1))
    return jnp.concatenate(out_rows, axis=0)


def _chunk_prefix_sum(x):
    sub = lax.broadcasted_iota(jnp.int32, (SUBLANES, 1), 0)
    groups = []
    for g in range(TQ // SUBLANES):
        blk = x[g * SUBLANES:(g + 1) * SUBLANES]
        for shift in (1, 2, 4):
            blk = blk + jnp.where(sub >= shift, pltpu.roll(blk, shift, axis=0), 0.0)
        if g % (CHUNK // SUBLANES):
            blk = blk + jnp.broadcast_to(groups[-1][SUBLANES - 1:], blk.shape)
        groups.append(blk)
    return jnp.concatenate(groups, axis=0)


def _gla(q, k, v, log_alpha, state_ref, gain):
    cum = _chunk_prefix_sum(log_alpha)
    chunk_total = [cum[(c + 1) * CHUNK - 1:(c + 1) * CHUNK] for c in range(N_CHUNKS)]
    cum_last = jnp.concatenate([jnp.broadcast_to(t, (CHUNK, GLA_K_WIDTH)) for t in chunk_total], axis=0)
    q_dec = (q * jnp.exp(cum) * (GLA_KEY_DIM ** -0.5)).astype(_BF16)
    k_inv = (k * jnp.exp(-cum)).astype(_BF16)
    k_end_f32 = k * jnp.exp(cum_last - cum)
    chunk_of_time = lax.broadcasted_iota(jnp.int32, (1, TQ), 1) // CHUNK
    v_bf = v.astype(_BF16)

    row_all = lax.broadcasted_iota(jnp.int32, (GLA_HEADS * TQ, TQ), 0) & (TQ - 1)
    col_all = lax.broadcasted_iota(jnp.int32, (GLA_HEADS * TQ, TQ), 1)
    causal_all = jnp.logical_and(row_all // CHUNK == col_all // CHUNK, col_all <= row_all)
    head_of_lane = lax.broadcasted_iota(jnp.int32, (1, GLA_K_WIDTH), 1) // GLA_KEY_DIM
    qs_all = jnp.concatenate([jnp.where(head_of_lane == hd, q_dec, 0) for hd in range(GLA_HEADS)], axis=0)
    scores_all = jnp.where(causal_all, _dot_nt(qs_all, k_inv), 0.0).astype(_BF16)
    first_head_rows = lax.broadcasted_iota(jnp.int32, (LANES, 1), 0) < GLA_KEY_DIM
    zero_v = jnp.zeros((TQ, GLA_VAL_DIM), _BF16)
    outs = []
    for pair in range(GLA_HEADS // 2):
        cols = slice(pair * LANES, (pair + 1) * LANES)
        v_pair = v_bf[:, 2 * pair * GLA_VAL_DIM:(2 * pair + 2) * GLA_VAL_DIM]
        scores = jnp.concatenate([scores_all[2 * pair * TQ:(2 * pair + 1) * TQ],
                                  scores_all[(2 * pair + 1) * TQ:(2 * pair + 2) * TQ]], axis=1)
        v_diag = jnp.concatenate([jnp.concatenate([v_pair[:, :GLA_VAL_DIM], zero_v], axis=1),
                                  jnp.concatenate([zero_v, v_pair[:, GLA_VAL_DIM:]], axis=1)], axis=0)
        k_end_t = k_end_f32[:, cols].T
        kv_lhs = jnp.concatenate([jnp.where(chunk_of_time == c, k_end_t, 0.0) for c in range(N_CHUNKS)],
                                 axis=0).astype(_BF16)
        kv_all = _dot(kv_lhs, v_pair)
        totals = jnp.concatenate([jnp.broadcast_to(t[:, cols], (LANES // N_CHUNKS, LANES)) for t in chunk_total],
                                 axis=0).T
        states = [state_ref[pair]]
        for c in range(N_CHUNKS):
            blk = slice(c * LANES, (c + 1) * LANES)
            kv = jnp.where(first_head_rows, kv_all[blk, :GLA_VAL_DIM], kv_all[blk, GLA_VAL_DIM:])
            decay_col = jnp.exp(totals[:, c * (LANES // N_CHUNKS):c * (LANES // N_CHUNKS) + 1])
            states.append(states[-1] * decay_col + kv)
        state_ref[pair] = states[-1]
        state_rows = []
        for c in range(N_CHUNKS):
            state_rows.append(jnp.concatenate([jnp.where(first_head_rows, states[c], 0.0),
                                               jnp.where(first_head_rows, 0.0, states[c])], axis=1))
        rhs = jnp.concatenate([v_diag, jnp.concatenate(state_rows, axis=0).astype(_BF16)], axis=0)
        lhs = jnp.concatenate([scores, _chunk_block_diag(q_dec[:, cols])], axis=1)
        out_pair = _dot(lhs, rhs)
        outs.append(_rms(out_pair[:, :GLA_VAL_DIM], gain))
        outs.append(_rms(out_pair[:, GLA_VAL_DIM:], gain))
    return jnp.concatenate(outs, axis=1)


def _load_weights(w_in_t_hbm, w_up_ref, w_o_sb_hbm, w_o_gla_hbm, w_out_hbm, w_in_ref, w_alpha_ref, w_gate_ref,
                  w_o_sb_ref, w_o_gla_ref, w_out_ref, stage_ref, sem_ref):
    rows = WEIGHT_CHUNK_ROWS
    jobs = []

    def into_columns(dst, block):
        def consume(chunk):
            dst[:, block * rows:(block + 1) * rows] = chunk.T.astype(_BF16)
        return consume

    def into_rows(dst, block):
        def consume(chunk):
            dst[block * rows:(block + 1) * rows, :] = chunk.astype(_BF16)
        return consume

    def low_rank(chunk):
        lane = lax.broadcasted_iota(jnp.int32, (1, LR_PAD), 1)
        w_lr = jnp.where(lane < GLA_RANK, chunk[:LR_PAD].T, 0.0)
        w_alpha_ref[...] = jnp.dot(w_lr, w_up_ref[...], precision=lax.Precision.HIGHEST,
                                   preferred_element_type=_F32).astype(_BF16)

    w_in_t = w_in_t_hbm.at[0]
    for block in range(OFF_LR // rows):
        jobs.append((w_in_t, block * rows, into_columns(w_in_ref, block)))
    jobs.append((w_in_t, OFF_LR, low_rank))
    for block in range(2 * D_MODEL // rows):
        jobs.append((w_in_t, OFF_LR + GLA_RANK + block * rows, into_columns(w_gate_ref, block)))
    for src, dst in ((w_o_sb_hbm, w_o_sb_ref), (w_o_gla_hbm, w_o_gla_ref), (w_out_hbm, w_out_ref)):
        for block in range(dst.shape[0] // rows):
            jobs.append((src.at[0], block * rows, into_rows(dst, block)))

    n_buf = stage_ref.shape[0]

    def copy(j):
        src, row0, _ = jobs[j]
        return pltpu.make_async_copy(src.at[pl.ds(row0, rows), :], stage_ref.at[j % n_buf], sem_ref.at[j % n_buf])

    for j in range(min(n_buf - 1, len(jobs))):
        copy(j).start()
    for j in range(len(jobs)):
        if j + n_buf - 1 < len(jobs):
            copy(j + n_buf - 1).start()
        copy(j).wait()
        jobs[j][2](stage_ref[j % n_buf])


def _block_kernel(x_ref, g_pre_ref, w_in_t_hbm, b_gate_ref, w_up_ref, b_up_ref, g_gla_ref,
                  w_o_sb_hbm, w_o_gla_hbm, w_out_hbm, g_final_ref, out_ref,
                  w_in_ref, w_alpha_ref, w_gate_ref, w_o_sb_ref, w_o_gla_ref, w_out_ref, stage_ref, sem_ref,
                  k_all, vm_all, state_ref, qs_ref, nc_ref, acc_ref):
    step = pl.program_id(1)

    @pl.when(jnp.logical_and(pl.program_id(0) == 0, step == 0))
    def _():
        _load_weights(w_in_t_hbm, w_up_ref, w_o_sb_hbm, w_o_gla_hbm, w_out_hbm, w_in_ref, w_alpha_ref, w_gate_ref,
                      w_o_sb_ref, w_o_gla_ref, w_out_ref, stage_ref, sem_ref)

    @pl.when(step == 0)
    def _():
        state_ref[...] = jnp.zeros_like(state_ref)
        k_all[0:TQ, :] = jnp.zeros((TQ, SB_WIDTH), _BF16)
        vm_all[:, 0:TQ, :] = jnp.zeros((2, TQ, SB_WIDTH), _BF16)

    x = x_ref[0]
    h = (x * g_pre_ref[...]).astype(_BF16)
    inv_rms = jnp.broadcast_to(lax.rsqrt(jnp.mean(x * x, axis=-1, keepdims=True) + EPS), (STEP_ROWS, LANES))

    def scale_rows(p):
        return p * jnp.concatenate([inv_rms] * (p.shape[1] // LANES), axis=1)

    def proj(offset, width):
        return scale_rows(_dot(h, w_in_ref[:, offset:offset + width]))

    first_block = step * STEP_BLOCKS
    block_rows = [slice(s * TQ, (s + 1) * TQ) for s in range(STEP_BLOCKS)]
    rows = pl.ds(pl.multiple_of((first_block + 1) * TQ, TQ), STEP_ROWS)
    k_all[rows, :] = proj(OFF_SB_K, SB_WIDTH).astype(_BF16)
    sb_v = proj(OFF_SB_V, SB_WIDTH).astype(_BF16)
    first_head = (lax.broadcasted_iota(jnp.int32, (1, SB_WIDTH), 1) & (LANES - 1)) < SB_HEAD_DIM
    vm_all[0, rows, :] = jnp.where(first_head, sb_v, 0)
    vm_all[1, rows, :] = jnp.where(first_head, 0, sb_v)
    sb_q = (proj(OFF_SB_Q, SB_WIDTH) * (LOG2_E * SB_HEAD_DIM ** -0.5)).astype(_BF16)
    for s in range(STEP_BLOCKS):
        _sb_near_blocks(sb_q[block_rows[s]], k_all, vm_all, first_block + s, qs_ref.at[s], nc_ref.at[s], acc_ref.at[s])

    alpha_logit = scale_rows(_dot(h, w_alpha_ref[...])) + b_up_ref[...]
    log_alpha = _log_sigmoid(alpha_logit) / GLA_GATE_TEMP
    gla_q, gla_k, gla_v = proj(OFF_GLA_Q, GLA_K_WIDTH), proj(OFF_GLA_K, GLA_K_WIDTH), proj(OFF_GLA_V, GLA_V_WIDTH)
    o_gla = jnp.concatenate(
        [_gla(gla_q[r], gla_k[r], gla_v[r], log_alpha[r], state_ref, g_gla_ref[...]) for r in block_rows], axis=0)
    y_gla = _dot((o_gla * _silu(proj(OFF_GLA_Z, GLA_V_WIDTH))).astype(_BF16), w_o_gla_ref[...])
    gates = jax.nn.sigmoid(scale_rows(_dot(h, w_gate_ref[...])) + b_gate_ref[...])
    gated_gla = gates[:, D_MODEL:] * y_gla
    gate_sb = gates[:, :D_MODEL]
    sb_zs = _silu(proj(OFF_SB_Z, SB_WIDTH))

    def finish():
        o_sb = jnp.concatenate([jnp.concatenate([acc_ref[s, pair] for pair in range(SB_PAIRS)], axis=1)
                                for s in range(STEP_BLOCKS)], axis=0)
        y_sb = _dot((o_sb * sb_zs).astype(_BF16), w_o_sb_ref[...])
        merged = gate_sb * y_sb + gated_gla
        res = x + _dot(merged.astype(_BF16), w_out_ref[...])
        out_ref[0] = _rms(res, g_final_ref[...])

    finish()

    @pl.when(jnp.logical_and(step >= 1, jnp.min(nc_ref[...]) < SB_DEAD_LOG2))
    def _():
        for s in range(STEP_BLOCKS):
            _sb_far_blocks(k_all, vm_all, first_block + s, qs_ref.at[s], nc_ref.at[s], acc_ref.at[s])
        finish()


def _resident(shape):
    return pl.BlockSpec(shape, lambda b, i: (0,) * len(shape), pipeline_mode=pl.Buffered(1))


@jax.jit
def kernel(x, g_pre, w_in, b_gate, w_alpha_up, b_alpha_up, g_gla_norm, w_o_sb, w_o_gla, w_out, g_final):
    batch, seq, d_model = x.shape
    depth = w_in.shape[0]
    assert d_model == D_MODEL and seq % STEP_ROWS == 0 and depth == 1
    w_up = jnp.pad(w_alpha_up[0], ((0, LR_PAD - GLA_RANK), (0, 0)))
    w_in_t = jnp.transpose(w_in, (0, 2, 1))

    row_block = pl.BlockSpec((1, STEP_ROWS, D_MODEL), lambda b, i: (b, i, 0))
    in_hbm = pl.BlockSpec(memory_space=pl.ANY)
    return pl.pallas_call(
        _block_kernel,
        grid=(batch, seq // STEP_ROWS),
        in_specs=[
            row_block,
            _resident((1, D_MODEL)),
            in_hbm,
            _resident((1, 2 * D_MODEL)),
            _resident((LR_PAD, GLA_K_WIDTH)),
            _resident((1, GLA_K_WIDTH)),
            _resident((1, GLA_VAL_DIM)),
            in_hbm,
            in_hbm,
            in_hbm,
            _resident((1, D_MODEL)),
        ],
        out_specs=row_block,
        out_shape=jax.ShapeDtypeStruct(x.shape, x.dtype),
        scratch_shapes=[
            pltpu.VMEM((D_MODEL, OFF_LR), _BF16),
            pltpu.VMEM((D_MODEL, GLA_K_WIDTH), _BF16),
            pltpu.VMEM((D_MODEL, 2 * D_MODEL), _BF16),
            pltpu.VMEM((SB_WIDTH, D_MODEL), _BF16),
            pltpu.VMEM((GLA_V_WIDTH, D_MODEL), _BF16),
            pltpu.VMEM((D_MODEL, D_MODEL), _BF16),
            pltpu.VMEM((WEIGHT_BUFFERS, WEIGHT_CHUNK_ROWS, D_MODEL), _F32),
            pltpu.SemaphoreType.DMA((WEIGHT_BUFFERS,)),
            pltpu.VMEM((seq + TQ, SB_WIDTH), _BF16),
            pltpu.VMEM((2, seq + TQ, SB_WIDTH), _BF16),
            pltpu.VMEM((GLA_HEADS // 2, LANES, GLA_VAL_DIM), _F32),
            pltpu.VMEM((STEP_BLOCKS, SB_PAIRS, 2 * TQ, LANES), _BF16),
            pltpu.VMEM((STEP_BLOCKS, SB_PAIRS, 2 * TQ, LANES), _F32),
            pltpu.VMEM((STEP_BLOCKS, SB_PAIRS, TQ, LANES), _F32),
        ],
        compiler_params=pltpu.CompilerParams(
            dimension_semantics=("arbitrary", "arbitrary"), vmem_limit_bytes=VMEM_LIMIT_BYTES),
        name="hybrid_mixer_block",
    )(x, g_pre, w_in_t, b_gate, w_up, b_alpha_up, g_gla_norm, w_o_sb, w_o_gla, w_out, g_final.reshape(1, D_MODEL))
```

```python
import math

import jax
import jax.numpy as jnp
from jax import lax
from jax.experimental import pallas as pl
from jax.experimental.pallas import tpu as pltpu

D_MODEL = 1024
CHUNK = 64
SB_HEAD_DIM = 64
SB_WIDTH = D_MODEL // 2
GLA_HEADS = 4
GLA_K_WIDTH = D_MODEL // 4
GLA_V_WIDTH = D_MODEL // 2
GLA_KEY_DIM = GLA_K_WIDTH // GLA_HEADS
GLA_VAL_DIM = GLA_V_WIDTH // GLA_HEADS
GLA_RANK = 16
GLA_GATE_TEMP = 16.0
EPS = 1e-6
LOG2_E = math.log2(math.e)
EXP2_CLAMP = 126.0
SB_DEAD_LOG2 = 151.0

LANES = 128
SUBLANES = 8
SB_PAIRS = SB_WIDTH // LANES
SB_GROUP = 2
TQ = 256
STEP_BLOCKS = 2
STEP_ROWS = STEP_BLOCKS * TQ
N_CHUNKS = TQ // CHUNK
LR_PAD = LANES
WEIGHT_CHUNK_ROWS = 512
WEIGHT_BUFFERS = 3

OFF_SB_Q = 0
OFF_SB_K = OFF_SB_Q + SB_WIDTH
OFF_SB_V = OFF_SB_K + SB_WIDTH
OFF_SB_Z = OFF_SB_V + SB_WIDTH
OFF_GLA_Q = OFF_SB_Z + SB_WIDTH
OFF_GLA_K = OFF_GLA_Q + GLA_K_WIDTH
OFF_GLA_V = OFF_GLA_K + GLA_K_WIDTH
OFF_GLA_Z = OFF_GLA_V + GLA_V_WIDTH
OFF_LR = OFF_GLA_Z + GLA_V_WIDTH

VMEM_LIMIT_BYTES = 56 * 1024 * 1024

_F32 = jnp.float32
_BF16 = jnp.bfloat16
_NT = (((1,), (1,)), ((), ()))


def _dot(a, b):
    return jnp.dot(a, b, preferred_element_type=_F32)


def _dot_nt(a, b):
    return lax.dot_general(a, b, _NT, preferred_element_type=_F32)


def _log_sigmoid(x):
    return jnp.minimum(x, 0.0) - jnp.log(1.0 + jnp.exp(-jnp.abs(x)))


def _silu(x):
    return x * jax.nn.sigmoid(x)


def _rms(x, gain):
    return x * lax.rsqrt(jnp.mean(x * x, axis=-1, keepdims=True) + EPS) * gain


def _sb_neg_log2_miss(zz):
    return jnp.maximum(jnp.log(1.0 + jnp.exp2(jnp.minimum(zz, EXP2_CLAMP))) * LOG2_E, zz)


def _sb_incl():
    row = lax.broadcasted_iota(jnp.int32, (TQ, TQ), 0)
    col = lax.broadcasted_iota(jnp.int32, (TQ, TQ), 1)
    return (row >= col).astype(_BF16)


def _sb_near_blocks(q, k_all, vm_all, first_block, qs_ref, nc_ref, acc_ref):
    assert STEP_BLOCKS == 2
    n_heads = 2 * SB_GROUP
    width = SB_GROUP * LANES
    row = lax.broadcasted_iota(jnp.int32, (n_heads * TQ, TQ), 0) & (TQ - 1)
    col = lax.broadcasted_iota(jnp.int32, (n_heads * TQ, TQ), 1)
    causal = col < row
    incl = _sb_incl()
    head_of_lane = lax.broadcasted_iota(jnp.int32, (1, width), 1) // SB_HEAD_DIM
    zero_v = jnp.zeros((TQ, LANES), _BF16)

    def side_by_side(w):
        return jnp.concatenate([w[hd * TQ:(hd + 1) * TQ] for hd in range(n_heads)], axis=1).astype(_BF16)

    for group in range(SB_PAIRS // SB_GROUP):
        cols = slice(group * width, (group + 1) * width)

        def values(block):
            keys = pl.ds(pl.multiple_of(block * TQ, TQ), TQ)
            return jnp.concatenate([
                jnp.concatenate([vm_all[hd % 2, keys, (group * SB_GROUP + p) * LANES:(group * SB_GROUP + p + 1) * LANES]
                                 if p == hd // 2 else zero_v for p in range(SB_GROUP)], axis=1)
                for hd in range(n_heads)], axis=0)

        w_prev, w_own = [], []
        for s in range(STEP_BLOCKS):
            keys = pl.ds(pl.multiple_of((first_block + s) * TQ, TQ), 2 * TQ)
            q_group = q[s * TQ:(s + 1) * TQ, cols]
            qs = jnp.concatenate([jnp.where(head_of_lane == hd, q_group, 0) for hd in range(n_heads)], axis=0)
            zz = _dot_nt(qs, k_all[keys, cols])
            zz_prev, zz_own = zz[:, :TQ], zz[:, TQ:]
            nl_prev = _sb_neg_log2_miss(zz_prev)
            nl_own = jnp.where(causal, _sb_neg_log2_miss(zz_own), 0.0)
            tot_own = _dot(nl_own.astype(_BF16), incl)
            tot_prev = _dot(nl_prev.astype(_BF16), incl) + tot_own[:, 0:1]
            w_prev.append(side_by_side(jnp.exp2(zz_prev - tot_prev)))
            w_own.append(side_by_side(jnp.where(causal, jnp.exp2(zz_own - tot_own), 0.0)))
            carried = jnp.broadcast_to(tot_prev[:, 0:1], (n_heads * TQ, LANES))
            for p in range(SB_GROUP):
                heads = slice(2 * p * TQ, (2 * p + 2) * TQ)
                qs_ref[s, group * SB_GROUP + p] = qs[heads, p * LANES:(p + 1) * LANES]
                nc_ref[s, group * SB_GROUP + p] = carried[heads]
        shared = _dot(jnp.concatenate([w_own[0], w_prev[1]], axis=0), values(first_block + 1))
        outs = (_dot(w_prev[0], values(first_block)) + shared[:TQ],
                shared[TQ:] + _dot(w_own[1], values(first_block + 2)))
        for s in range(STEP_BLOCKS):
            for p in range(SB_GROUP):
                acc_ref[s, group * SB_GROUP + p] = outs[s][:, p * LANES:(p + 1) * LANES]


def _sb_far_blocks(k_all, vm_all, step, qs_ref, nc_ref, acc_ref):
    def more_blocks(carry):
        t, live = carry
        return jnp.logical_and(t < step - 1, live)

    def earlier_block(carry):
        t, _ = carry
        keys = pl.ds(pl.multiple_of((step - 1 - t) * TQ, TQ), TQ)
        incl = _sb_incl()
        for pair in range(SB_PAIRS):
            cols = slice(pair * LANES, (pair + 1) * LANES)
            zz = _dot_nt(qs_ref[pair], k_all[keys, cols])
            nl = _sb_neg_log2_miss(zz)
            tot = _dot(nl.astype(_BF16), incl) + jnp.concatenate([nc_ref[pair]] * (TQ // LANES), axis=1)
            w = jnp.exp2(zz - tot)
            w_cat = jnp.concatenate([w[:TQ], w[TQ:]], axis=1).astype(_BF16)
            v_cat = jnp.concatenate([vm_all[0, keys, cols], vm_all[1, keys, cols]], axis=0)
            acc_ref[pair] += _dot(w_cat, v_cat)
            nc_ref[pair] = jnp.broadcast_to(tot[:, 0:1], (2 * TQ, LANES))
        return t + 1, jnp.min(nc_ref[...]) < SB_DEAD_LOG2

    lax.while_loop(more_blocks, earlier_block, (jnp.int32(0), jnp.min(nc_ref[...]) < SB_DEAD_LOG2))


def _chunk_block_diag(x):
    zero = jnp.zeros((CHUNK, LANES), x.dtype)
    out_rows = []
    for r in range(x.shape[0] // CHUNK):
        blocks = [x[r * CHUNK:(r + 1) * CHUNK] if c == r % N_CHUNKS else zero for c in range(N_CHUNKS)]
        out_rows.append(jnp.concatenate(blocks, axis=1))
    return jnp.concatenate(out_rows, axis=0)


def _chunk_prefix_sum(x):
    sub = lax.broadcasted_iota(jnp.int32, (SUBLANES, 1), 0)
    groups = []
    for g in range(TQ // SUBLANES):
        blk = x[g * SUBLANES:(g + 1) * SUBLANES]
        for shift in (1, 2, 4):
            blk = blk + jnp.where(sub >= shift, pltpu.roll(blk, shift, axis=0), 0.0)
        if g % (CHUNK // SUBLANES):
            blk = blk + jnp.broadcast_to(groups[-1][SUBLANES - 1:], blk.shape)
        groups.append(blk)
    return jnp.concatenate(groups, axis=0)


def _gla(q, k, v, log_alpha, state_ref, gain):
    cum = _chunk_prefix_sum(log_alpha)
    chunk_total = [cum[(c + 1) * CHUNK - 1:(c + 1) * CHUNK] for c in range(N_CHUNKS)]
    cum_last = jnp.concatenate([jnp.broadcast_to(t, (CHUNK, GLA_K_WIDTH)) for t in chunk_total], axis=0)
    q_dec = (q * jnp.exp(cum) * (GLA_KEY_DIM ** -0.5)).astype(_BF16)
    k_inv = (k * jnp.exp(-cum)).astype(_BF16)
    k_end_f32 = k * jnp.exp(cum_last - cum)
    chunk_of_time = lax.broadcasted_iota(jnp.int32, (1, TQ), 1) // CHUNK
    v_bf = v.astype(_BF16)

    row_all = lax.broadcasted_iota(jnp.int32, (GLA_HEADS * TQ, TQ), 0) & (TQ - 1)
    col_all = lax.broadcasted_iota(jnp.int32, (GLA_HEADS * TQ, TQ), 1)
    causal_all = jnp.logical_and(row_all // CHUNK == col_all // CHUNK, col_all <= row_all)
    head_of_lane = lax.broadcasted_iota(jnp.int32, (1, GLA_K_WIDTH), 1) // GLA_KEY_DIM
    qs_all = jnp.concatenate([jnp.where(head_of_lane == hd, q_dec, 0) for hd in range(GLA_HEADS)], axis=0)
    scores_all = jnp.where(causal_all, _dot_nt(qs_all, k_inv), 0.0).astype(_BF16)
    first_head_rows = lax.broadcasted_iota(jnp.int32, (LANES, 1), 0) < GLA_KEY_DIM
    zero_v = jnp.zeros((TQ, GLA_VAL_DIM), _BF16)
    outs = []
    for pair in range(GLA_HEADS // 2):
        cols = slice(pair * LANES, (pair + 1) * LANES)
        v_pair = v_bf[:, 2 * pair * GLA_VAL_DIM:(2 * pair + 2) * GLA_VAL_DIM]
        scores = jnp.concatenate([scores_all[2 * pair * TQ:(2 * pair + 1) * TQ],
                                  scores_all[(2 * pair + 1) * TQ:(2 * pair + 2) * TQ]], axis=1)
        v_diag = jnp.concatenate([jnp.concatenate([v_pair[:, :GLA_VAL_DIM], zero_v], axis=1),
                                  jnp.concatenate([zero_v, v_pair[:, GLA_VAL_DIM:]], axis=1)], axis=0)
        k_end_t = k_end_f32[:, cols].T
        kv_lhs = jnp.concatenate([jnp.where(chunk_of_time == c, k_end_t, 0.0) for c in range(N_CHUNKS)],
                                 axis=0).astype(_BF16)
        kv_all = _dot(kv_lhs, v_pair)
        totals = jnp.concatenate([jnp.broadcast_to(t[:, cols], (LANES // N_CHUNKS, LANES)) for t in chunk_total],
                                 axis=0).T
        states = [state_ref[pair]]
        for c in range(N_CHUNKS):
            blk = slice(c * LANES, (c + 1) * LANES)
            kv = jnp.where(first_head_rows, kv_all[blk, :GLA_VAL_DIM], kv_all[blk, GLA_VAL_DIM:])
            decay_col = jnp.exp(totals[:, c * (LANES // N_CHUNKS):c * (LANES // N_CHUNKS) + 1])
            states.append(states[-1] * decay_col + kv)
        state_ref[pair] = states[-1]
        state_rows = []
        for c in range(N_CHUNKS):
            state_rows.append(jnp.concatenate([jnp.where(first_head_rows, states[c], 0.0),
                                               jnp.where(first_head_rows, 0.0, states[c])], axis=1))
        rhs = jnp.concatenate([v_diag, jnp.concatenate(state_rows, axis=0).astype(_BF16)], axis=0)
        lhs = jnp.concatenate([scores, _chunk_block_diag(q_dec[:, cols])], axis=1)
        out_pair = _dot(lhs, rhs)
        outs.append(_rms(out_pair[:, :GLA_VAL_DIM], gain))
        outs.append(_rms(out_pair[:, GLA_VAL_DIM:], gain))
    return jnp.concatenate(outs, axis=1)


def _load_weights(w_in_t_hbm, w_up_ref, w_o_sb_hbm, w_o_gla_hbm, w_out_hbm, w_in_ref, w_alpha_ref, w_gate_ref,
                  w_o_sb_ref, w_o_gla_ref, w_out_ref, stage_ref, sem_ref):
    rows = WEIGHT_CHUNK_ROWS
    jobs = []

    def into_columns(dst, block):
        def consume(chunk):
            dst[:, block * rows:(block + 1) * rows] = chunk.T.astype(_BF16)
        return consume

    def into_rows(dst, block):
        def consume(chunk):
            dst[block * rows:(block + 1) * rows, :] = chunk.astype(_BF16)
        return consume

    def low_rank(chunk):
        lane = lax.broadcasted_iota(jnp.int32, (1, LR_PAD), 1)
        w_lr = jnp.where(lane < GLA_RANK, chunk[:LR_PAD].T, 0.0)
        w_alpha_ref[...] = jnp.dot(w_lr, w_up_ref[...], precision=lax.Precision.HIGHEST,
                                   preferred_element_type=_F32).astype(_BF16)

    w_in_t = w_in_t_hbm.at[0]
    for block in range(OFF_LR // rows):
        jobs.append((w_in_t, block * rows, into_columns(w_in_ref, block)))
    jobs.append((w_in_t, OFF_LR, low_rank))
    for block in range(2 * D_MODEL // rows):
        jobs.append((w_in_t, OFF_LR + GLA_RANK + block * rows, into_columns(w_gate_ref, block)))
    for src, dst in ((w_o_sb_hbm, w_o_sb_ref), (w_o_gla_hbm, w_o_gla_ref), (w_out_hbm, w_out_ref)):
        for block in range(dst.shape[0] // rows):
            jobs.append((src.at[0], block * rows, into_rows(dst, block)))

    n_buf = stage_ref.shape[0]

    def copy(j):
        src, row0, _ = jobs[j]
        return pltpu.make_async_copy(src.at[pl.ds(row0, rows), :], stage_ref.at[j % n_buf], sem_ref.at[j % n_buf])

    for j in range(min(n_buf - 1, len(jobs))):
        copy(j).start()
    for j in range(len(jobs)):
        if j + n_buf - 1 < len(jobs):
            copy(j + n_buf - 1).start()
        copy(j).wait()
        jobs[j][2](stage_ref[j % n_buf])


def _block_kernel(x_ref, g_pre_ref, w_in_t_hbm, b_gate_ref, w_up_ref, b_up_ref, g_gla_ref,
                  w_o_sb_hbm, w_o_gla_hbm, w_out_hbm, g_final_ref, out_ref,
                  w_in_ref, w_alpha_ref, w_gate_ref, w_o_sb_ref, w_o_gla_ref, w_out_ref, stage_ref, sem_ref,
                  k_all, vm_all, state_ref, qs_ref, nc_ref, acc_ref):
    step = pl.program_id(1)

    @pl.when(jnp.logical_and(pl.program_id(0) == 0, step == 0))
    def _():
        _load_weights(w_in_t_hbm, w_up_ref, w_o_sb_hbm, w_o_gla_hbm, w_out_hbm, w_in_ref, w_alpha_ref, w_gate_ref,
                      w_o_sb_ref, w_o_gla_ref, w_out_ref, stage_ref, sem_ref)

    @pl.when(step == 0)
    def _():
        state_ref[...] = jnp.zeros_like(state_ref)
        k_all[0:TQ, :] = jnp.zeros((TQ, SB_WIDTH), _BF16)
        vm_all[:, 0:TQ, :] = jnp.zeros((2, TQ, SB_WIDTH), _BF16)

    x = x_ref[0]
    h = (x * g_pre_ref[...]).astype(_BF16)
    inv_rms = jnp.broadcast_to(lax.rsqrt(jnp.mean(x * x, axis=-1, keepdims=True) + EPS), (STEP_ROWS, LANES))

    def scale_rows(p):
        return p * jnp.concatenate([inv_rms] * (p.shape[1] // LANES), axis=1)

    def proj(offset, width):
        return scale_rows(_dot(h, w_in_ref[:, offset:offset + width]))

    first_block = step * STEP_BLOCKS
    block_rows = [slice(s * TQ, (s + 1) * TQ) for s in range(STEP_BLOCKS)]
    rows = pl.ds(pl.multiple_of((first_block + 1) * TQ, TQ), STEP_ROWS)
    k_all[rows, :] = proj(OFF_SB_K, SB_WIDTH).astype(_BF16)
    sb_v = proj(OFF_SB_V, SB_WIDTH).astype(_BF16)
    first_head = (lax.broadcasted_iota(jnp.int32, (1, SB_WIDTH), 1) & (LANES - 1)) < SB_HEAD_DIM
    vm_all[0, rows, :] = jnp.where(first_head, sb_v, 0)
    vm_all[1, rows, :] = jnp.where(first_head, 0, sb_v)
    sb_q = (proj(OFF_SB_Q, SB_WIDTH) * (LOG2_E * SB_HEAD_DIM ** -0.5)).astype(_BF16)
    _sb_near_blocks(sb_q, k_all, vm_all, first_block, qs_ref, nc_ref, acc_ref)

    alpha_logit = scale_rows(_dot(h, w_alpha_ref[...])) + b_up_ref[...]
    log_alpha = _log_sigmoid(alpha_logit) / GLA_GATE_TEMP
    gla_q, gla_k, gla_v = proj(OFF_GLA_Q, GLA_K_WIDTH), proj(OFF_GLA_K, GLA_K_WIDTH), proj(OFF_GLA_V, GLA_V_WIDTH)
    o_gla = jnp.concatenate(
        [_gla(gla_q[r], gla_k[r], gla_v[r], log_alpha[r], state_ref, g_gla_ref[...]) for r in block_rows], axis=0)
    y_gla = _dot((o_gla * _silu(proj(OFF_GLA_Z, GLA_V_WIDTH))).astype(_BF16), w_o_gla_ref[...])
    gates = jax.nn.sigmoid(scale_rows(_dot(h, w_gate_ref[...])) + b_gate_ref[...])
    gated_gla = gates[:, D_MODEL:] * y_gla
    gate_sb = gates[:, :D_MODEL]
    sb_zs = _silu(proj(OFF_SB_Z, SB_WIDTH))

    def finish():
        o_sb = jnp.concatenate([jnp.concatenate([acc_ref[s, pair] for pair in range(SB_PAIRS)], axis=1)
                                for s in range(STEP_BLOCKS)], axis=0)
        y_sb = _dot((o_sb * sb_zs).astype(_BF16), w_o_sb_ref[...])
        merged = gate_sb * y_sb + gated_gla
        res = x + _dot(merged.astype(_BF16), w_out_ref[...])
        out_ref[0] = _rms(res, g_final_ref[...])

    finish()

    @pl.when(jnp.logical_and(step >= 1, jnp.min(nc_ref[...]) < SB_DEAD_LOG2))
    def _():
        for s in range(STEP_BLOCKS):
            _sb_far_blocks(k_all, vm_all, first_block + s, qs_ref.at[s], nc_ref.at[s], acc_ref.at[s])
        finish()


def _resident(shape):
    return pl.BlockSpec(shape, lambda b, i: (0,) * len(shape), pipeline_mode=pl.Buffered(1))


@jax.jit
def kernel(x, g_pre, w_in, b_gate, w_alpha_up, b_alpha_up, g_gla_norm, w_o_sb, w_o_gla, w_out, g_final):
    batch, seq, d_model = x.shape
    depth = w_in.shape[0]
    assert d_model == D_MODEL and seq % STEP_ROWS == 0 and depth == 1
    w_up = jnp.pad(w_alpha_up[0], ((0, LR_PAD - GLA_RANK), (0, 0)))
    w_in_t = jnp.transpose(w_in, (0, 2, 1))

    row_block = pl.BlockSpec((1, STEP_ROWS, D_MODEL), lambda b, i: (b, i, 0))
    in_hbm = pl.BlockSpec(memory_space=pl.ANY)
    return pl.pallas_call(
        _block_kernel,
        grid=(batch, seq // STEP_ROWS),
        in_specs=[
            row_block,
            _resident((1, D_MODEL)),
            in_hbm,
            _resident((1, 2 * D_MODEL)),
            _resident((LR_PAD, GLA_K_WIDTH)),
            _resident((1, GLA_K_WIDTH)),
            _resident((1, GLA_VAL_DIM)),
            in_hbm,
            in_hbm,
            in_hbm,
            _resident((1, D_MODEL)),
        ],
        out_specs=row_block,
        out_shape=jax.ShapeDtypeStruct(x.shape, x.dtype),
        scratch_shapes=[
            pltpu.VMEM((D_MODEL, OFF_LR), _BF16),
            pltpu.VMEM((D_MODEL, GLA_K_WIDTH), _BF16),
            pltpu.VMEM((D_MODEL, 2 * D_MODEL), _BF16),
            pltpu.VMEM((SB_WIDTH, D_MODEL), _BF16),
            pltpu.VMEM((GLA_V_WIDTH, D_MODEL), _BF16),
            pltpu.VMEM((D_MODEL, D_MODEL), _BF16),
            pltpu.VMEM((WEIGHT_BUFFERS, WEIGHT_CHUNK_ROWS, D_MODEL), _F32),
            pltpu.SemaphoreType.DMA((WEIGHT_BUFFERS,)),
            pltpu.VMEM((seq + TQ, SB_WIDTH), _BF16),
            pltpu.VMEM((2, seq + TQ, SB_WIDTH), _BF16),
            pltpu.VMEM((GLA_HEADS // 2, LANES, GLA_VAL_DIM), _F32),
            pltpu.VMEM((STEP_BLOCKS, SB_PAIRS, 2 * TQ, LANES), _BF16),
            pltpu.VMEM((STEP_BLOCKS, SB_PAIRS, 2 * TQ, LANES), _F32),
            pltpu.VMEM((STEP_BLOCKS, SB_PAIRS, TQ, LANES), _F32),
        ],
        compiler_params=pltpu.CompilerParams(
            dimension_semantics=("arbitrary", "arbitrary"), vmem_limit_bytes=VMEM_LIMIT_BYTES),
        name="hybrid_mixer_block",
    )(x, g_pre, w_in_t, b_gate, w_up, b_alpha_up, g_gla_norm, w_o_sb, w_o_gla, w_out, g_final.reshape(1, D_MODEL))
```

```python
import math

import jax
import jax.numpy as jnp
from jax import lax
from jax.experimental import pallas as pl
from jax.experimental.pallas import tpu as pltpu

D_MODEL = 1024
CHUNK = 64
SB_HEAD_DIM = 64
SB_WIDTH = D_MODEL // 2
GLA_HEADS = 4
GLA_K_WIDTH = D_MODEL // 4
GLA_V_WIDTH = D_MODEL // 2
GLA_KEY_DIM = GLA_K_WIDTH // GLA_HEADS
GLA_VAL_DIM = GLA_V_WIDTH // GLA_HEADS
GLA_RANK = 16
GLA_GATE_TEMP = 16.0
EPS = 1e-6
LOG2_E = math.log2(math.e)
EXP2_CLAMP = 126.0
SB_DEAD_LOG2 = 151.0

LANES = 128
SUBLANES = 8
SB_PAIRS = SB_WIDTH // LANES
SB_GROUP = 2
TQ = 256
STEP_BLOCKS = 2
STEP_ROWS = STEP_BLOCKS * TQ
N_CHUNKS = TQ // CHUNK
LR_PAD = LANES
WEIGHT_CHUNK_ROWS = 512
WEIGHT_BUFFERS = 3

OFF_SB_Q = 0
OFF_SB_K = OFF_SB_Q + SB_WIDTH
OFF_SB_V = OFF_SB_K + SB_WIDTH
OFF_SB_Z = OFF_SB_V + SB_WIDTH
OFF_GLA_Q = OFF_SB_Z + SB_WIDTH
OFF_GLA_K = OFF_GLA_Q + GLA_K_WIDTH
OFF_GLA_V = OFF_GLA_K + GLA_K_WIDTH
OFF_GLA_Z = OFF_GLA_V + GLA_V_WIDTH
OFF_LR = OFF_GLA_Z + GLA_V_WIDTH

VMEM_LIMIT_BYTES = 56 * 1024 * 1024

_F32 = jnp.float32
_BF16 = jnp.bfloat16
_NT = (((1,), (1,)), ((), ()))


def _dot(a, b):
    return jnp.dot(a, b, preferred_element_type=_F32)


def _dot_nt(a, b):
    return lax.dot_general(a, b, _NT, preferred_element_type=_F32)


def _log_sigmoid(x):
    return jnp.minimum(x, 0.0) - jnp.log(1.0 + jnp.exp(-jnp.abs(x)))


def _silu(x):
    return x * jax.nn.sigmoid(x)


def _rms(x, gain):
    return x * lax.rsqrt(jnp.mean(x * x, axis=-1, keepdims=True) + EPS) * gain


def _sb_neg_log2_miss(zz):
    return jnp.maximum(jnp.log(1.0 + jnp.exp2(jnp.minimum(zz, EXP2_CLAMP))) * LOG2_E, zz)


def _sb_incl():
    row = lax.broadcasted_iota(jnp.int32, (TQ, TQ), 0)
    col = lax.broadcasted_iota(jnp.int32, (TQ, TQ), 1)
    return (row >= col).astype(_BF16)


def _sb_near_blocks(q, k_all, vm_all, step, qs_ref, nc_ref, acc_ref):
    n_heads = 2 * SB_GROUP
    width = SB_GROUP * LANES
    row = lax.broadcasted_iota(jnp.int32, (n_heads * TQ, TQ), 0) & (TQ - 1)
    col = lax.broadcasted_iota(jnp.int32, (n_heads * TQ, TQ), 1)
    causal = col < row
    incl = _sb_incl()
    keys = pl.ds(pl.multiple_of(step * TQ, TQ), 2 * TQ)
    head_of_lane = lax.broadcasted_iota(jnp.int32, (1, width), 1) // SB_HEAD_DIM
    zero_v = jnp.zeros((2 * TQ, LANES), _BF16)
    for group in range(SB_PAIRS // SB_GROUP):
        cols = slice(group * width, (group + 1) * width)
        q_group = q[:, cols]
        qs = jnp.concatenate([jnp.where(head_of_lane == hd, q_group, 0) for hd in range(n_heads)], axis=0)
        zz = _dot_nt(qs, k_all[keys, cols])
        zz_prev, zz_own = zz[:, :TQ], zz[:, TQ:]
        nl_prev = _sb_neg_log2_miss(zz_prev)
        nl_own = jnp.where(causal, _sb_neg_log2_miss(zz_own), 0.0)
        tot_own = _dot(nl_own.astype(_BF16), incl)
        tot_prev = _dot(nl_prev.astype(_BF16), incl) + tot_own[:, 0:1]
        w_prev = jnp.exp2(zz_prev - tot_prev)
        w_own = jnp.where(causal, jnp.exp2(zz_own - tot_own), 0.0)
        carried = jnp.broadcast_to(tot_prev[:, 0:1], (n_heads * TQ, LANES))
        w_cat = jnp.concatenate(
            [blk[hd * TQ:(hd + 1) * TQ] for hd in range(n_heads) for blk in (w_prev, w_own)], axis=1).astype(_BF16)
        v_cat = jnp.concatenate([
            jnp.concatenate([vm_all[hd % 2, keys, (group * SB_GROUP + p) * LANES:(group * SB_GROUP + p + 1) * LANES]
                             if p == hd // 2 else zero_v for p in range(SB_GROUP)], axis=1)
            for hd in range(n_heads)], axis=0)
        out = _dot(w_cat, v_cat)
        for p in range(SB_GROUP):
            pair = group * SB_GROUP + p
            heads = slice(2 * p * TQ, (2 * p + 2) * TQ)
            qs_ref[pair] = qs[heads, p * LANES:(p + 1) * LANES]
            nc_ref[pair] = carried[heads]
            acc_ref[pair] = out[:, p * LANES:(p + 1) * LANES]


def _sb_far_blocks(k_all, vm_all, step, qs_ref, nc_ref, acc_ref):
    def more_blocks(carry):
        t, live = carry
        return jnp.logical_and(t < step - 1, live)

    def earlier_block(carry):
        t, _ = carry
        keys = pl.ds(pl.multiple_of((step - 1 - t) * TQ, TQ), TQ)
        incl = _sb_incl()
        for pair in range(SB_PAIRS):
            cols = slice(pair * LANES, (pair + 1) * LANES)
            zz = _dot_nt(qs_ref[pair], k_all[keys, cols])
            nl = _sb_neg_log2_miss(zz)
            tot = _dot(nl.astype(_BF16), incl) + jnp.concatenate([nc_ref[pair]] * (TQ // LANES), axis=1)
            w = jnp.exp2(zz - tot)
            w_cat = jnp.concatenate([w[:TQ], w[TQ:]], axis=1).astype(_BF16)
            v_cat = jnp.concatenate([vm_all[0, keys, cols], vm_all[1, keys, cols]], axis=0)
            acc_ref[pair] += _dot(w_cat, v_cat)
            nc_ref[pair] = jnp.broadcast_to(tot[:, 0:1], (2 * TQ, LANES))
        return t + 1, jnp.min(nc_ref[...]) < SB_DEAD_LOG2

    lax.while_loop(more_blocks, earlier_block, (jnp.int32(0), jnp.min(nc_ref[...]) < SB_DEAD_LOG2))


def _chunk_block_diag(x):
    zero = jnp.zeros((CHUNK, LANES), x.dtype)
    out_rows = []
    for r in range(x.shape[0] // CHUNK):
        blocks = [x[r * CHUNK:(r + 1) * CHUNK] if c == r % N_CHUNKS else zero for c in range(N_CHUNKS)]
        out_rows.append(jnp.concatenate(blocks, axis=1))
    return jnp.concatenate(out_rows, axis=0)


def _chunk_prefix_sum(x):
    sub = lax.broadcasted_iota(jnp.int32, (SUBLANES, 1), 0)
    groups = []
    for g in range(TQ // SUBLANES):
        blk = x[g * SUBLANES:(g + 1) * SUBLANES]
        for shift in (1, 2, 4):
            blk = blk + jnp.where(sub >= shift, pltpu.roll(blk, shift, axis=0), 0.0)
        if g % (CHUNK // SUBLANES):
            blk = blk + jnp.broadcast_to(groups[-1][SUBLANES - 1:], blk.shape)
        groups.append(blk)
    return jnp.concatenate(groups, axis=0)


def _gla(q, k, v, log_alpha, state_ref, gain):
    cum = _chunk_prefix_sum(log_alpha)
    chunk_total = [cum[(c + 1) * CHUNK - 1:(c + 1) * CHUNK] for c in range(N_CHUNKS)]
    cum_last = jnp.concatenate([jnp.broadcast_to(t, (CHUNK, GLA_K_WIDTH)) for t in chunk_total], axis=0)
    q_dec = (q * jnp.exp(cum) * (GLA_KEY_DIM ** -0.5)).astype(_BF16)
    k_inv_f32 = k * jnp.exp(-cum)
    k_inv = k_inv_f32.astype(_BF16)
    k_end_f32 = k_inv_f32 * jnp.exp(cum_last)
    chunk_of_time = lax.broadcasted_iota(jnp.int32, (1, TQ), 1) // CHUNK
    v_bf = v.astype(_BF16)

    row_all = lax.broadcasted_iota(jnp.int32, (GLA_HEADS * TQ, TQ), 0) & (TQ - 1)
    col_all = lax.broadcasted_iota(jnp.int32, (GLA_HEADS * TQ, TQ), 1)
    causal_all = jnp.logical_and(row_all // CHUNK == col_all // CHUNK, col_all <= row_all)
    head_of_lane = lax.broadcasted_iota(jnp.int32, (1, GLA_K_WIDTH), 1) // GLA_KEY_DIM
    qs_all = jnp.concatenate([jnp.where(head_of_lane == hd, q_dec, 0) for hd in range(GLA_HEADS)], axis=0)
    scores_all = jnp.where(causal_all, _dot_nt(qs_all, k_inv), 0.0).astype(_BF16)
    first_head_rows = lax.broadcasted_iota(jnp.int32, (LANES, 1), 0) < GLA_KEY_DIM
    zero_v = jnp.zeros((TQ, GLA_VAL_DIM), _BF16)
    outs = []
    for pair in range(GLA_HEADS // 2):
        cols = slice(pair * LANES, (pair + 1) * LANES)
        v_pair = v_bf[:, 2 * pair * GLA_VAL_DIM:(2 * pair + 2) * GLA_VAL_DIM]
        scores = jnp.concatenate([scores_all[2 * pair * TQ:(2 * pair + 1) * TQ],
                                  scores_all[(2 * pair + 1) * TQ:(2 * pair + 2) * TQ]], axis=1)
        v_diag = jnp.concatenate([jnp.concatenate([v_pair[:, :GLA_VAL_DIM], zero_v], axis=1),
                                  jnp.concatenate([zero_v, v_pair[:, GLA_VAL_DIM:]], axis=1)], axis=0)
        k_end_t = k_end_f32[:, cols].T
        kv_lhs = jnp.concatenate([jnp.where(chunk_of_time == c, k_end_t, 0.0) for c in range(N_CHUNKS)],
                                 axis=0).astype(_BF16)
        kv_all = _dot(kv_lhs, v_pair)
        totals = jnp.concatenate([jnp.broadcast_to(t[:, cols], (LANES // N_CHUNKS, LANES)) for t in chunk_total],
                                 axis=0).T
        states = [state_ref[pair]]
        for c in range(N_CHUNKS):
            blk = slice(c * LANES, (c + 1) * LANES)
            kv = jnp.where(first_head_rows, kv_all[blk, :GLA_VAL_DIM], kv_all[blk, GLA_VAL_DIM:])
            decay_col = jnp.exp(totals[:, c * (LANES // N_CHUNKS):c * (LANES // N_CHUNKS) + 1])
            states.append(states[-1] * decay_col + kv)
        state_ref[pair] = states[-1]
        state_rows = []
        for c in range(N_CHUNKS):
            state_rows.append(jnp.concatenate([jnp.where(first_head_rows, states[c], 0.0),
                                               jnp.where(first_head_rows, 0.0, states[c])], axis=1))
        rhs = jnp.concatenate([v_diag, jnp.concatenate(state_rows, axis=0).astype(_BF16)], axis=0)
        lhs = jnp.concatenate([scores, _chunk_block_diag(q_dec[:, cols])], axis=1)
        out_pair = _dot(lhs, rhs)
        outs.append(_rms(out_pair[:, :GLA_VAL_DIM], gain))
        outs.append(_rms(out_pair[:, GLA_VAL_DIM:], gain))
    return jnp.concatenate(outs, axis=1)


def _load_weights(w_in_t_hbm, w_up_ref, w_o_sb_hbm, w_o_gla_hbm, w_out_hbm, w_in_ref, w_alpha_ref, w_gate_ref,
                  w_o_sb_ref, w_o_gla_ref, w_out_ref, stage_ref, sem_ref):
    rows = WEIGHT_CHUNK_ROWS
    jobs = []

    def into_columns(dst, block):
        def consume(chunk):
            dst[:, block * rows:(block + 1) * rows] = chunk.T.astype(_BF16)
        return consume

    def into_rows(dst, block):
        def consume(chunk):
            dst[block * rows:(block + 1) * rows, :] = chunk.astype(_BF16)
        return consume

    def low_rank(chunk):
        lane = lax.broadcasted_iota(jnp.int32, (1, LR_PAD), 1)
        w_lr = jnp.where(lane < GLA_RANK, chunk[:LR_PAD].T, 0.0)
        w_alpha_ref[...] = jnp.dot(w_lr, w_up_ref[...], precision=lax.Precision.HIGHEST,
                                   preferred_element_type=_F32).astype(_BF16)

    w_in_t = w_in_t_hbm.at[0]
    for block in range(OFF_LR // rows):
        jobs.append((w_in_t, block * rows, into_columns(w_in_ref, block)))
    jobs.append((w_in_t, OFF_LR, low_rank))
    for block in range(2 * D_MODEL // rows):
        jobs.append((w_in_t, OFF_LR + GLA_RANK + block * rows, into_columns(w_gate_ref, block)))
    for src, dst in ((w_o_sb_hbm, w_o_sb_ref), (w_o_gla_hbm, w_o_gla_ref), (w_out_hbm, w_out_ref)):
        for block in range(dst.shape[0] // rows):
            jobs.append((src.at[0], block * rows, into_rows(dst, block)))

    n_buf = stage_ref.shape[0]

    def copy(j):
        src, row0, _ = jobs[j]
        return pltpu.make_async_copy(src.at[pl.ds(row0, rows), :], stage_ref.at[j % n_buf], sem_ref.at[j % n_buf])

    for j in range(min(n_buf - 1, len(jobs))):
        copy(j).start()
    for j in range(len(jobs)):
        if j + n_buf - 1 < len(jobs):
            copy(j + n_buf - 1).start()
        copy(j).wait()
        jobs[j][2](stage_ref[j % n_buf])


def _block_kernel(x_ref, g_pre_ref, w_in_t_hbm, b_gate_ref, w_up_ref, b_up_ref, g_gla_ref,
                  w_o_sb_hbm, w_o_gla_hbm, w_out_hbm, g_final_ref, out_ref,
                  w_in_ref, w_alpha_ref, w_gate_ref, w_o_sb_ref, w_o_gla_ref, w_out_ref, stage_ref, sem_ref,
                  k_all, vm_all, state_ref, qs_ref, nc_ref, acc_ref):
    step = pl.program_id(1)

    @pl.when(jnp.logical_and(pl.program_id(0) == 0, step == 0))
    def _():
        _load_weights(w_in_t_hbm, w_up_ref, w_o_sb_hbm, w_o_gla_hbm, w_out_hbm, w_in_ref, w_alpha_ref, w_gate_ref,
                      w_o_sb_ref, w_o_gla_ref, w_out_ref, stage_ref, sem_ref)

    @pl.when(step == 0)
    def _():
        state_ref[...] = jnp.zeros_like(state_ref)
        k_all[0:TQ, :] = jnp.zeros((TQ, SB_WIDTH), _BF16)
        vm_all[:, 0:TQ, :] = jnp.zeros((2, TQ, SB_WIDTH), _BF16)

    x = x_ref[0]
    h = (x * g_pre_ref[...]).astype(_BF16)
    inv_rms = jnp.broadcast_to(lax.rsqrt(jnp.mean(x * x, axis=-1, keepdims=True) + EPS), (STEP_ROWS, LANES))

    def scale_rows(p):
        return p * jnp.concatenate([inv_rms] * (p.shape[1] // LANES), axis=1)

    def proj(offset, width):
        return scale_rows(_dot(h, w_in_ref[:, offset:offset + width]))

    first_block = step * STEP_BLOCKS
    block_rows = [slice(s * TQ, (s + 1) * TQ) for s in range(STEP_BLOCKS)]
    rows = pl.ds(pl.multiple_of((first_block + 1) * TQ, TQ), STEP_ROWS)
    k_all[rows, :] = proj(OFF_SB_K, SB_WIDTH).astype(_BF16)
    sb_v = proj(OFF_SB_V, SB_WIDTH).astype(_BF16)
    first_head = (lax.broadcasted_iota(jnp.int32, (1, SB_WIDTH), 1) & (LANES - 1)) < SB_HEAD_DIM
    vm_all[0, rows, :] = jnp.where(first_head, sb_v, 0)
    vm_all[1, rows, :] = jnp.where(first_head, 0, sb_v)
    sb_q = (proj(OFF_SB_Q, SB_WIDTH) * (LOG2_E * SB_HEAD_DIM ** -0.5)).astype(_BF16)
    for s in range(STEP_BLOCKS):
        _sb_near_blocks(sb_q[block_rows[s]], k_all, vm_all, first_block + s, qs_ref.at[s], nc_ref.at[s], acc_ref.at[s])

    alpha_logit = scale_rows(_dot(h, w_alpha_ref[...])) + b_up_ref[...]
    log_alpha = _log_sigmoid(alpha_logit) / GLA_GATE_TEMP
    gla_q, gla_k, gla_v = proj(OFF_GLA_Q, GLA_K_WIDTH), proj(OFF_GLA_K, GLA_K_WIDTH), proj(OFF_GLA_V, GLA_V_WIDTH)
    o_gla = jnp.concatenate(
        [_gla(gla_q[r], gla_k[r], gla_v[r], log_alpha[r], state_ref, g_gla_ref[...]) for r in block_rows], axis=0)
    y_gla = _dot((o_gla * _silu(proj(OFF_GLA_Z, GLA_V_WIDTH))).astype(_BF16), w_o_gla_ref[...])
    gates = jax.nn.sigmoid(scale_rows(_dot(h, w_gate_ref[...])) + b_gate_ref[...])
    gated_gla = gates[:, D_MODEL:] * y_gla
    gate_sb = gates[:, :D_MODEL]
    sb_zs = _silu(proj(OFF_SB_Z, SB_WIDTH))

    def finish():
        o_sb = jnp.concatenate([jnp.concatenate([acc_ref[s, pair] for pair in range(SB_PAIRS)], axis=1)
                                for s in range(STEP_BLOCKS)], axis=0)
        y_sb = _dot((o_sb * sb_zs).astype(_BF16), w_o_sb_ref[...])
        merged = gate_sb * y_sb + gated_gla
        res = x + _dot(merged.astype(_BF16), w_out_ref[...])
        out_ref[0] = _rms(res, g_final_ref[...])

    finish()

    @pl.when(jnp.logical_and(step >= 1, jnp.min(nc_ref[...]) < SB_DEAD_LOG2))
    def _():
        for s in range(STEP_BLOCKS):
            _sb_far_blocks(k_all, vm_all, first_block + s, qs_ref.at[s], nc_ref.at[s], acc_ref.at[s])
        finish()


def _resident(shape):
    return pl.BlockSpec(shape, lambda b, i: (0,) * len(shape), pipeline_mode=pl.Buffered(1))


@jax.jit
def kernel(x, g_pre, w_in, b_gate, w_alpha_up, b_alpha_up, g_gla_norm, w_o_sb, w_o_gla, w_out, g_final):
    batch, seq, d_model = x.shape
    depth = w_in.shape[0]
    assert d_model == D_MODEL and seq % STEP_ROWS == 0 and depth == 1
    w_up = jnp.pad(w_alpha_up[0], ((0, LR_PAD - GLA_RANK), (0, 0)))
    w_in_t = jnp.transpose(w_in, (0, 2, 1))

    row_block = pl.BlockSpec((1, STEP_ROWS, D_MODEL), lambda b, i: (b, i, 0))
    in_hbm = pl.BlockSpec(memory_space=pl.ANY)
    return pl.pallas_call(
        _block_kernel,
        grid=(batch, seq // STEP_ROWS),
        in_specs=[
            row_block,
            _resident((1, D_MODEL)),
            in_hbm,
            _resident((1, 2 * D_MODEL)),
            _resident((LR_PAD, GLA_K_WIDTH)),
            _resident((1, GLA_K_WIDTH)),
            _resident((1, GLA_VAL_DIM)),
            in_hbm,
            in_hbm,
            in_hbm,
            _resident((1, D_MODEL)),
        ],
        out_specs=row_block,
        out_shape=jax.ShapeDtypeStruct(x.shape, x.dtype),
        scratch_shapes=[
            pltpu.VMEM((D_MODEL, OFF_LR), _BF16),
            pltpu.VMEM((D_MODEL, GLA_K_WIDTH), _BF16),
            pltpu.VMEM((D_MODEL, 2 * D_MODEL), _BF16),
            pltpu.VMEM((SB_WIDTH, D_MODEL), _BF16),
            pltpu.VMEM((GLA_V_WIDTH, D_MODEL), _BF16),
            pltpu.VMEM((D_MODEL, D_MODEL), _BF16),
            pltpu.VMEM((WEIGHT_BUFFERS, WEIGHT_CHUNK_ROWS, D_MODEL), _F32),
            pltpu.SemaphoreType.DMA((WEIGHT_BUFFERS,)),
            pltpu.VMEM((seq + TQ, SB_WIDTH), _BF16),
            pltpu.VMEM((2, seq + TQ, SB_WIDTH), _BF16),
            pltpu.VMEM((GLA_HEADS // 2, LANES, GLA_VAL_DIM), _F32),
            pltpu.VMEM((STEP_BLOCKS, SB_PAIRS, 2 * TQ, LANES), _BF16),
            pltpu.VMEM((STEP_BLOCKS, SB_PAIRS, 2 * TQ, LANES), _F32),
            pltpu.VMEM((STEP_BLOCKS, SB_PAIRS, TQ, LANES), _F32),
        ],
        compiler_params=pltpu.CompilerParams(
            dimension_semantics=("arbitrary", "arbitrary"), vmem_limit_bytes=VMEM_LIMIT_BYTES),
        name="hybrid_mixer_block",
    )(x, g_pre, w_in_t, b_gate, w_up, b_alpha_up, g_gla_norm, w_o_sb, w_o_gla, w_out, g_final.reshape(1, D_MODEL))
```
